```python
import jax
import jax.numpy as jnp
from jax import lax
import numpy as np

D_MODEL = 1024
BATCH = 8
SEQ = 4096
DEPTH = 4

GRID_W = 64
CTX_LEN = 256
N_MIXERS = 2
EXPAND = 2
D_INNER = EXPAND * D_MODEL
RW_HEAD = 64
RW_HEADS = D_INNER // RW_HEAD
R_DECAY = 64
R_AAA = 64
R_MV = 32
HG_DK = 128
HG_HEADS = D_INNER // HG_DK
HG_DV = D_INNER // HG_HEADS
CHUNK = 64
N_RW = (DEPTH + 1) // 2
N_HG = DEPTH // 2
NORM_EPS = 1e-6
LN_X_EPS = 64e-5

kernel_name = "hybrid_rwkv7_hgrn2_dit_trunk"


def _rmsnorm(x, g):
    xf = x.astype(jnp.float32)
    y = xf * lax.rsqrt(jnp.mean(xf * xf, axis=-1, keepdims=True) + NORM_EPS)
    return (y * g.astype(jnp.float32)).astype(x.dtype)


def _heads(t, n_heads):
    return t.reshape(t.shape[0], t.shape[1], n_heads, t.shape[2] // n_heads)


def _qshift_grid(h, rows):
    b, s, d = h.shape
    q = d // 4
    g = jnp.pad(h.reshape(b, rows, GRID_W, d), ((0, 0), (1, 1), (1, 1), (0, 0)))
    left = g[:, 1:-1, :-2, :q]
    right = g[:, 1:-1, 2:, q:2 * q]
    up = g[:, :-2, 1:-1, 2 * q:3 * q]
    down = g[:, 2:, 1:-1, 3 * q:]
    return jnp.concatenate([left, right, up, down], axis=-1).reshape(b, s, d)


def _shift_seq(h):
    half = h.shape[-1] // 2
    prev = jnp.pad(h[:, :-1, :half], ((0, 0), (1, 0), (0, 0)))
    nxt = jnp.pad(h[:, 1:, half:], ((0, 0), (0, 1), (0, 0)))
    return jnp.concatenate([prev, nxt], axis=-1)


def _rwkv7_features(h, hs, mix, proj, w0, w1, w2, a0, a1, a2, k_k, k_a, v_res, v_first):
    f32 = jnp.float32
    xx = hs - h
    xr, xw, xk, xv, xa, xg = [h + xx * mix[n] for n in range(6)]
    r = xr @ proj[0]
    k = xk @ proj[1]
    v = xv @ proj[2]
    z = jax.nn.silu(xg @ proj[3])
    v_raw = v
    if v_res is not None:
        v0, v1, v2 = v_res
        v = v + (v_first - v) * jax.nn.sigmoid(v0 + (xv @ v1) @ v2)
    kk = _heads(k * k_k, RW_HEADS).astype(f32)
    kk = kk * lax.rsqrt(jnp.maximum(jnp.sum(kk * kk, axis=-1, keepdims=True), 1e-24))
    dirs = []
    for d in range(2):
        u = (w0[d] + jnp.tanh(xw @ w1[d]) @ w2[d]).astype(f32)
        log_w = -jnp.exp(-jax.nn.softplus(-u) - 0.5)
        a = jax.nn.sigmoid((a0[d] + (xa @ a1[d]) @ a2[d]).astype(f32))
        k_d = k.astype(f32) * (1.0 + (a - 1.0) * k_a.astype(f32))
        dirs.append((_heads(log_w, RW_HEADS), _heads(a, RW_HEADS), _heads(k_d, RW_HEADS)))
    return (_heads(r, RW_HEADS).astype(f32), _heads(v, RW_HEADS).astype(f32), z, kk, dirs, v_raw)


def _rwkv7_scan(r, log_w, k, v, kk, a, s0, reverse, emit):
    def step(s, xs):
        w_t, k_t, v_t, kk_t, a_t = xs[:5]
        sa = jnp.einsum('bhvk,bhk->bhv', s, -kk_t)
        s = (s * jnp.exp(w_t)[:, :, None, :] + sa[..., None] * (kk_t * a_t)[:, :, None, :]
             + v_t[..., None] * k_t[:, :, None, :])
        y = jnp.einsum('bhvk,bhk->bhv', s, xs[5]) if emit else None
        return s, y
    xs = (log_w, k, v, kk, a) + ((r,) if emit else ())
    s_last, ys = lax.scan(step, s0, tuple(jnp.moveaxis(t, 1, 0) for t in xs), reverse=reverse)
    return (jnp.moveaxis(ys, 0, 1) if emit else None), s_last


def _rwkv7_out(ys, feats, r_k, ln_w, ln_b, w_o):
    r, v, z, _, dirs, _ = feats
    y = ys[0] + ys[1]
    mu = jnp.mean(y, axis=-1, keepdims=True)
    var = jnp.mean(jnp.square(y - mu), axis=-1, keepdims=True)
    y = (y - mu) * lax.rsqrt(var + LN_X_EPS)
    rk = r_k.astype(jnp.float32).reshape(RW_HEADS, RW_HEAD)
    bonus = sum(jnp.sum(r * kd * rk, axis=-1, keepdims=True) for (_, _, kd) in dirs) * v
    b, t = y.shape[:2]
    y = (y.reshape(b, t, D_INNER) * ln_w.astype(jnp.float32) + ln_b.astype(jnp.float32)
         + bonus.reshape(b, t, D_INNER))
    return (y.astype(z.dtype) * z) @ w_o


def _rwkv7_mixer(h_lat, h_ctx, rows, p, v_res, vf_lat, vf_ctx, ctx_out):
    mix, proj, w_o, w0, w1, w2, a0, a1, a2, k_k, k_a, r_k, ln_w, ln_b = p
    shared = (mix, proj, w0, w1, w2, a0, a1, a2, k_k, k_a, v_res)
    f_ctx = _rwkv7_features(h_ctx, _shift_seq(h_ctx), *shared, vf_ctx)
    f_lat = _rwkv7_features(h_lat, _qshift_grid(h_lat, rows), *shared, vf_lat)
    s0 = jnp.zeros((h_lat.shape[0], RW_HEADS, RW_HEAD, RW_HEAD), jnp.float32)

    def run(f, s_init, emit):
        r, v, _, kk, dirs, _ = f
        outs = [_rwkv7_scan(r, lw, kd, v, kk, a, s_init[d], d == 1, emit)
                for d, (lw, a, kd) in enumerate(dirs)]
        return [o[0] for o in outs], [o[1] for o in outs]

    ys_ctx, s_ctx = run(f_ctx, (s0, s0), ctx_out)
    ys_lat, _ = run(f_lat, s_ctx, True)
    y_lat = _rwkv7_out(ys_lat, f_lat, r_k, ln_w, ln_b, w_o)
    y_ctx = _rwkv7_out(ys_ctx, f_ctx, r_k, ln_w, ln_b, w_o) if ctx_out else None
    return y_lat, y_ctx, f_lat[5], f_ctx[5]


def _hgrn2_features(h, w_in, lb):
    f32 = jnp.float32
    q, f_fwd, f_bwd, i, g = jnp.split(h @ w_in, 5, axis=-1)
    lb = lb.astype(f32)
    dirs = []
    for fl in (f_fwd, f_bwd):
        f = lb + (1.0 - lb) * jax.nn.sigmoid(fl.astype(f32))
        dirs.append((_heads(1.0 - f, HG_HEADS), _heads(jnp.log(f), HG_HEADS)))
    return (_heads(jax.nn.silu(q), HG_HEADS).astype(f32), _heads(i, HG_HEADS).astype(f32), g, dirs)


def _to_chunks(t):
    b, s, h, d = t.shape
    return t.reshape(b, s // CHUNK, CHUNK, h, d).transpose(1, 0, 3, 2, 4)


def _hgrn2_chunk_scan(q, k, v, log_f, s0):
    mask = jnp.tril(jnp.ones((CHUNK, CHUNK), dtype=bool))[:, :, None]

    def step(s, xs):
        q_c, k_c, v_c, lf_c = xs
        cum = jnp.cumsum(lf_c, axis=2)
        diff = jnp.where(mask, cum[:, :, :, None, :] - cum[:, :, None, :, :], -jnp.inf)
        att = jnp.einsum('bhtk,bhsk,bhtsk->bhts', q_c, k_c, jnp.exp(diff))
        y = (jnp.einsum('bhts,bhsv->bhtv', att, v_c)
             + jnp.einsum('bhtk,bhkv->bhtv', q_c * jnp.exp(cum), s))
        last = cum[:, :, -1:, :]
        s = (jnp.exp(last[:, :, 0, :])[..., None] * s
             + jnp.einsum('bhsk,bhsv->bhkv', k_c * jnp.exp(last - cum), v_c))
        return s, y

    s_last, ys = lax.scan(step, s0, tuple(_to_chunks(t) for t in (q, k, v, log_f)))
    nc, b, h, c, d = ys.shape
    return ys.transpose(1, 0, 3, 2, 4).reshape(b, nc * c, h, d), s_last


def _hgrn2_final_state(k, v, log_f):
    cum = jnp.cumsum(log_f, axis=1)
    return jnp.einsum('bthk,bthv->bhkv', k * jnp.exp(cum[:, -1:] - cum), v)


def _hgrn2_out(y, g, gn, w_o):
    y = y * lax.rsqrt(jnp.mean(y * y, axis=-1, keepdims=True) + NORM_EPS) * gn.astype(jnp.float32)
    b, t = y.shape[:2]
    return (y.reshape(b, t, D_INNER).astype(g.dtype) * jax.nn.silu(g)) @ w_o


def _hgrn2_mixer(h_lat, h_ctx, w_in, w_o, gn, lb, ctx_out):
    rev = lambda t: t[:, ::-1]
    q_c, i_c, g_c, dirs_c = _hgrn2_features(h_ctx, w_in, lb)
    q_l, i_l, g_l, dirs_l = _hgrn2_features(h_lat, w_in, lb)
    (k_cf, lf_cf), (k_cb, lf_cb) = dirs_c
    (k_lf, lf_lf), (k_lb, lf_lb) = dirs_l
    if ctx_out:
        s0 = jnp.zeros((h_ctx.shape[0], HG_HEADS, HG_DK, HG_DV), jnp.float32)
        yc_f, s_f = _hgrn2_chunk_scan(q_c, k_cf, i_c, lf_cf, s0)
        yc_b, s_b = _hgrn2_chunk_scan(rev(q_c), rev(k_cb), rev(i_c), rev(lf_cb), s0)
        y_ctx = _hgrn2_out(yc_f + rev(yc_b), g_c, gn, w_o)
    else:
        s_f = _hgrn2_final_state(k_cf, i_c, lf_cf)
        s_b = _hgrn2_final_state(rev(k_cb), rev(i_c), rev(lf_cb))
        y_ctx = None
    yl_f, _ = _hgrn2_chunk_scan(q_l, k_lf, i_l, lf_lf, s_f)
    yl_b, _ = _hgrn2_chunk_scan(rev(q_l), rev(k_lb), rev(i_l), rev(lf_lb), s_b)
    y_lat = _hgrn2_out(yl_f + rev(yl_b), g_l, gn, w_o)
    return y_lat, y_ctx


def setup_inputs(seed: int = 0) -> dict:
    key = jax.random.key(seed)
    ks = iter(jax.random.split(key, 40))
    nrm = lambda shape, scale: scale * jax.random.normal(next(ks), shape, jnp.float32)
    D, DI = D_MODEL, D_INNER
    ramp = (jnp.arange(DI, dtype=jnp.float32) / (DI - 1)) ** 0.9
    return {
        "x": nrm((BATCH, SEQ, D), 1.0),
        "c": nrm((BATCH, D), 1.0),
        "ctx": nrm((BATCH, CTX_LEN, D), 1.0),
        "c_ctx": nrm((D,), 1.0),
        "mod_w": nrm((DEPTH, D, 3 * D), D ** -0.5),
        "mod_b": nrm((DEPTH, 3 * D), 0.01),
        "pre_g": 1.0 + nrm((DEPTH, D), 0.05),
        "post_g": 1.0 + nrm((DEPTH, D), 0.05),
        "rw_mix": jax.random.uniform(next(ks), (N_RW, 6, D), jnp.float32),
        "rw_proj": nrm((N_RW, 4, D, DI), D ** -0.5),
        "rw_wo": nrm((N_RW, DI, D), DI ** -0.5),
        "rw_w0": -6.0 + 5.0 * ramp + nrm((N_RW, 2, DI), 0.1),
        "rw_w1": nrm((N_RW, 2, D, R_DECAY), D ** -0.5),
        "rw_w2": nrm((N_RW, 2, R_DECAY, DI), 0.5 * R_DECAY ** -0.5),
        "rw_a0": nrm((N_RW, 2, DI), 0.1),
        "rw_a1": nrm((N_RW, 2, D, R_AAA), D ** -0.5),
        "rw_a2": nrm((N_RW, 2, R_AAA, DI), R_AAA ** -0.5),
        "rw_v0": nrm((N_RW - 1, DI), 0.1),
        "rw_v1": nrm((N_RW - 1, D, R_MV), D ** -0.5),
        "rw_v2": nrm((N_RW - 1, R_MV, DI), R_MV ** -0.5),
        "rw_kk": 0.85 + nrm((N_RW, DI), 0.05),
        "rw_ka": 1.0 + nrm((N_RW, DI), 0.05),
        "rw_rk": nrm((N_RW, DI), 0.1),
        "rw_lnw": 1.0 + nrm((N_RW, DI), 0.05),
        "rw_lnb": nrm((N_RW, DI), 0.01),
        "hg_win": nrm((N_HG, D, 5 * DI), D ** -0.5),
        "hg_wo": nrm((N_HG, DI, D), DI ** -0.5),
        "hg_gn": 1.0 + nrm((N_HG, HG_DV), 0.05),
        "hg_lb": nrm((DEPTH, DI), 0.1),
    }


def reference(x, c, ctx, c_ctx, mod_w, mod_b, pre_g, post_g, rw_mix, rw_proj, rw_wo, rw_w0, rw_w1,
              rw_w2, rw_a0, rw_a1, rw_a2, rw_v0, rw_v1, rw_v2, rw_kk, rw_ka, rw_rk, rw_lnw, rw_lnb,
              hg_win, hg_wo, hg_gn, hg_lb):
    rows = x.shape[1] // GRID_W
    sc = jax.nn.silu(c)
    scc = jax.nn.silu(c_ctx)
    p_lb = jax.nn.softmax(hg_lb.astype(jnp.float32), axis=0)
    lb_all = jnp.cumsum(p_lb, axis=0) - p_lb[0]
    vf_lat = None
    vf_ctx = None
    for i in range(DEPTH):
        ctx_out = i < DEPTH - 1
        shift, scale, gate = jnp.split(sc @ mod_w[i] + mod_b[i], 3, axis=-1)
        shift_c, scale_c, gate_c = jnp.split(scc @ mod_w[i] + mod_b[i], 3, axis=-1)
        h_lat = _rmsnorm(x, pre_g[i]) * (1.0 + scale[:, None]) + shift[:, None]
        h_ctx = _rmsnorm(ctx, pre_g[i]) * (1.0 + scale_c) + shift_c
        j = i // N_MIXERS
        if i % N_MIXERS == 0:
            p = (rw_mix[j], rw_proj[j], rw_wo[j], rw_w0[j], rw_w1[j], rw_w2[j], rw_a0[j], rw_a1[j],
                 rw_a2[j], rw_kk[j], rw_ka[j], rw_rk[j], rw_lnw[j], rw_lnb[j])
            v_res = None if j == 0 else (rw_v0[j - 1], rw_v1[j - 1], rw_v2[j - 1])
            y_lat, y_ctx, vfl, vfc = _rwkv7_mixer(h_lat, h_ctx, rows, p, v_res, vf_lat, vf_ctx, ctx_out)
            if j == 0:
                vf_lat, vf_ctx = vfl, vfc
        else:
            y_lat, y_ctx = _hgrn2_mixer(h_lat, h_ctx, hg_win[j], hg_wo[j], hg_gn[j], lb_all[i], ctx_out)
        x = x + gate[:, None] * _rmsnorm(y_lat, post_g[i])
        if ctx_out:
            ctx = ctx + gate_c * _rmsnorm(y_ctx, post_g[i])
    return x
```

```python
import functools
import math

import jax
import jax.numpy as jnp
from jax import lax
from jax.experimental import pallas as pl
from jax.experimental.pallas import tpu as pltpu

F32 = jnp.float32
BF16 = jnp.bfloat16
HI = lax.Precision.HIGHEST

GRID_W = 64
RW_HEAD = 64
HG_HEAD = 128
CHUNK = 64
TILE = 512
OTILE = 256
NORM_EPS = 1e-6
LN_X_EPS = 64e-5
LANES = 128
VMEM_LIMIT = 56 * 1024 * 1024


def _silu(x):
    return x * jax.nn.sigmoid(x)


def _dot(a, b):
    return jnp.dot(a, b, preferred_element_type=F32)


def _dot_nt(a, b):
    return lax.dot_general(a, b, (((1,), (1,)), ((), ())), preferred_element_type=F32)


def _dot_tn(a, b):
    return lax.dot_general(a, b, (((0,), (0,)), ((), ())), preferred_element_type=F32)


def _mod_kernel(s_ref, w_ref, b_ref, o_ref):
    s = _silu(s_ref[...])
    o_ref[0] = jnp.dot(s, w_ref[0], precision=HI, preferred_element_type=F32) + b_ref[0]


def _modulation(cond, mod_w, mod_b):
    depth, d, d3 = mod_w.shape
    rows = cond.shape[0]
    nb = d3 // d
    return pl.pallas_call(
        _mod_kernel,
        grid=(depth, nb),
        in_specs=[
            pl.BlockSpec((rows, d), lambda i, n: (0, 0)),
            pl.BlockSpec((1, d, d), lambda i, n: (i, 0, n)),
            pl.BlockSpec((1, 1, d), lambda i, n: (i, 0, n)),
        ],
        out_specs=pl.BlockSpec((1, rows, d), lambda i, n: (i, 0, n)),
        out_shape=jax.ShapeDtypeStruct((depth, rows, d3), F32),
        name="adaln_modulation",
        compiler_params=pltpu.CompilerParams(vmem_limit_bytes=VMEM_LIMIT),
    )(cond, mod_w, mod_b.reshape(depth, 1, d3))


def _adaln(xb, g, scale, shift):
    ms = jnp.mean(xb * xb, axis=-1, keepdims=True)
    return xb * lax.rsqrt(ms + NORM_EPS) * g * (1.0 + scale) + shift


def _head_sum_matrix(width, head):
    r = lax.broadcasted_iota(jnp.int32, (width, width), 0) // head
    c = lax.broadcasted_iota(jnp.int32, (width, width), 1) // head
    return (r == c).astype(F32)


def _rw_feat_kernel(n_lat_tiles, ctx_len, has_vres, *refs):
    (x_ref, xp_ref, xn_ref, mod_ref, g_ref, mix_ref, w1_ref, a1_ref, v1_ref,
     proj_ref, w2_ref, a2_ref, v2_ref, vec_ref, hs_ref) = refs[:15]
    pos = 15
    vf_ref = None
    if has_vres:
        vf_ref = refs[pos]
        pos += 1
    (r_o, v_o, kk_o, z_o, bv_o, lw_o, bb_o, kd_o) = refs[pos:pos + 8]
    hbuf, xm, tw, ta, tv = refs[pos + 8:]

    j = pl.program_id(1)
    n = pl.program_id(2)
    d = x_ref.shape[-1]
    q4 = d // 4
    nblk = TILE // GRID_W

    @pl.when(n == 0)
    def _prologue():
        m = mod_ref[0, 0]
        shift, scale = m[:, :d], m[:, d:2 * d]
        g = g_ref[...]
        is_lat = j < n_lat_tiles
        prev_ok = jnp.logical_and(j >= 1, is_lat)
        next_ok = j < n_lat_tiles - 1

        def norm_rows(rb, c):
            r0 = pl.multiple_of(rb * GRID_W, GRID_W)
            hbuf[pl.ds(r0 + GRID_W, GRID_W), :] = _adaln(x_ref[0, pl.ds(r0, GRID_W), :], g, scale, shift)
            return c

        lax.fori_loop(0, nblk, norm_rows, 0)
        hp = _adaln(xp_ref[0], g, scale, shift)
        hbuf[0:GRID_W, :] = jnp.where(prev_ok, hp, 0.0)
        hn = _adaln(xn_ref[0], g, scale, shift)
        hbuf[TILE + GRID_W:TILE + 2 * GRID_W, :] = jnp.where(next_ok, hn, 0.0)

        row = lax.broadcasted_iota(jnp.int32, (GRID_W, 1), 0)

        def store_mixed(r0, cs, h_c, hs_c):
            dd = hs_c - h_c
            for k in range(6):
                xm[k, pl.ds(r0, GRID_W), cs] = (h_c + dd * mix_ref[k:k + 1, cs]).astype(BF16)

        def mix_lat(rb, c):
            r0 = pl.multiple_of(rb * GRID_W, GRID_W)
            for qi in range(4):
                cs = slice(qi * q4, (qi + 1) * q4)
                h_c = hbuf[pl.ds(r0 + GRID_W, GRID_W), cs]
                if qi == 0:
                    hs_c = jnp.where(row == 0, 0.0, pltpu.roll(h_c, 1, axis=0))
                elif qi == 1:
                    hs_c = jnp.where(row == GRID_W - 1, 0.0, pltpu.roll(h_c, GRID_W - 1, axis=0))
                elif qi == 2:
                    hs_c = hbuf[pl.ds(r0, GRID_W), cs]
                else:
                    hs_c = hbuf[pl.ds(r0 + 2 * GRID_W, GRID_W), cs]
                store_mixed(r0, cs, h_c, hs_c)
            return c

        def mix_ctx(rb, c):
            r0 = pl.multiple_of(rb * GRID_W, GRID_W)
            t = row + rb * GRID_W
            for qi in range(4):
                cs = slice(qi * q4, (qi + 1) * q4)
                h_c = hbuf[pl.ds(r0 + GRID_W, GRID_W), cs]
                if qi < 2:
                    edge = hbuf[pl.ds(r0 + GRID_W - 1, 1), cs]
                    hs_c = jnp.where(row == 0, edge, pltpu.roll(h_c, 1, axis=0))
                else:
                    edge = hbuf[pl.ds(r0 + 2 * GRID_W, 1), cs]
                    hs_c = jnp.where(row == GRID_W - 1, edge, pltpu.roll(h_c, GRID_W - 1, axis=0))
                    hs_c = jnp.where(t == ctx_len - 1, 0.0, hs_c)
                store_mixed(r0, cs, h_c, hs_c)
            return c

        @pl.when(is_lat)
        def _():
            lax.fori_loop(0, nblk, mix_lat, 0)

        @pl.when(jnp.logical_not(is_lat))
        def _():
            lax.fori_loop(0, nblk, mix_ctx, 0)

        tw[...] = jnp.tanh(_dot(xm[1], w1_ref[...])).astype(BF16)
        ta[...] = _dot(xm[4], a1_ref[...]).astype(BF16)
        if has_vres:
            tv[...] = _dot(xm[3], v1_ref[...]).astype(BF16)

    vec = vec_ref[...]
    r = _dot(xm[0], proj_ref[0])
    k = _dot(xm[2], proj_ref[1])
    v = _dot(xm[3], proj_ref[2])
    z = _silu(_dot(xm[5], proj_ref[3]))
    if has_vres:
        gate = jax.nn.sigmoid(vec[4:5] + _dot(tv[...], v2_ref[...]))
        v = v + (vf_ref[0].astype(F32) - v) * gate
    hs = hs_ref[...]
    kk = k * vec[5:6]
    ss = jnp.dot(kk * kk, hs, precision=HI, preferred_element_type=F32)
    kk = kk * lax.rsqrt(jnp.maximum(ss, 1e-24))
    decay_scale = -math.exp(-0.5)
    ksum = None
    for dr in range(2):
        u = vec[dr:dr + 1] + _dot(tw[...], w2_ref[dr])
        lw_o[dr, 0] = decay_scale * jax.nn.sigmoid(u)
        a = jax.nn.sigmoid(vec[2 + dr:3 + dr] + _dot(ta[...], a2_ref[dr]))
        kd = k * (1.0 + (a - 1.0) * vec[6:7])
        bb_o[dr, 0] = (kk * a).astype(bb_o.dtype)
        kd_o[dr, 0] = kd.astype(kd_o.dtype)
        ksum = kd if ksum is None else ksum + kd
    bonus = jnp.dot(r * ksum * vec[7:8], hs, precision=HI, preferred_element_type=F32)
    r_o[0] = r.astype(r_o.dtype)
    v_o[0] = v.astype(v_o.dtype)
    kk_o[0] = kk.astype(kk_o.dtype)
    z_o[0] = z.astype(z_o.dtype)
    bv_o[0] = (bonus * v).astype(bv_o.dtype)


def _rw_features(xs, mod, pre_g, p, vf, *, n_lat_tiles, ctx_len, tn=256):
    b, tp, d = xs.shape
    di = p["proj"].shape[-1]
    nt = di // tn
    ntiles = tp // TILE
    nb64 = tp // GRID_W
    per = TILE // GRID_W
    has_vres = vf is not None

    def tok(bi, j, n):
        return (bi, j, 0)

    in_specs = [
        pl.BlockSpec((1, TILE, d), tok),
        pl.BlockSpec((1, GRID_W, d), lambda bi, j, n: (bi, jnp.maximum(j * per - 1, 0), 0)),
        pl.BlockSpec((1, GRID_W, d), lambda bi, j, n: (bi, jnp.minimum((j + 1) * per, nb64 - 1), 0)),
        pl.BlockSpec((1, 1, 1, mod.shape[-1]), lambda bi, j, n: (bi, j, 0, 0)),
        pl.BlockSpec((1, d), lambda bi, j, n: (0, 0)),
        pl.BlockSpec((6, d), lambda bi, j, n: (0, 0)),
        pl.BlockSpec((d, LANES), lambda bi, j, n: (0, 0)),
        pl.BlockSpec((d, LANES), lambda bi, j, n: (0, 0)),
        pl.BlockSpec((d, LANES), lambda bi, j, n: (0, 0)),
        pl.BlockSpec((4, d, tn), lambda bi, j, n: (0, 0, n)),
        pl.BlockSpec((2, LANES, tn), lambda bi, j, n: (0, 0, n)),
        pl.BlockSpec((2, LANES, tn), lambda bi, j, n: (0, 0, n)),
        pl.BlockSpec((LANES, tn), lambda bi, j, n: (0, n)),
        pl.BlockSpec((8, tn), lambda bi, j, n: (0, n)),
        pl.BlockSpec((tn, tn), lambda bi, j, n: (0, 0)),
    ]
    args = [xs, xs, xs, mod, pre_g, p["mix"], p["w1"], p["a1"], p["v1"], p["proj"], p["w2"], p["a2"],
            p["v2"], p["vec"], _head_sum_matrix(tn, RW_HEAD)]
    if has_vres:
        in_specs.append(pl.BlockSpec((1, TILE, tn), lambda bi, j, n: (bi, j, n)))
        args.append(vf)
    one = pl.BlockSpec((1, TILE, tn), lambda bi, j, n: (bi, j, n))
    two = pl.BlockSpec((2, 1, TILE, tn), lambda bi, j, n: (0, bi, j, n))
    s1 = lambda dt: jax.ShapeDtypeStruct((b, tp, di), dt)
    s2 = lambda dt: jax.ShapeDtypeStruct((2, b, tp, di), dt)
    return pl.pallas_call(
        functools.partial(_rw_feat_kernel, n_lat_tiles, ctx_len, has_vres),
        grid=(b, ntiles, nt),
        in_specs=in_specs,
        out_specs=[one, one, one, one, one, two, two, two],
        out_shape=[s1(BF16), s1(BF16), s1(BF16), s1(BF16), s1(BF16), s2(F32), s2(BF16), s2(BF16)],
        scratch_shapes=[
            pltpu.VMEM((TILE + 2 * GRID_W, d), F32),
            pltpu.VMEM((6, TILE, d), BF16),
            pltpu.VMEM((TILE, LANES), BF16),
            pltpu.VMEM((TILE, LANES), BF16),
            pltpu.VMEM((TILE, LANES), BF16),
        ],
        name="rwkv7_features",
        compiler_params=pltpu.CompilerParams(
            dimension_semantics=("parallel", "parallel", "arbitrary"), vmem_limit_bytes=VMEM_LIMIT),
    )(*args)


def _rw_chunk(r, v, kk, bb, kd, lw, h_state, rev):
    c = CHUNK
    half = RW_HEAD
    ri = lax.broadcasted_iota(jnp.int32, (c, c), 0)
    ci = lax.broadcasted_iota(jnp.int32, (c, c), 1)
    if rev:
        incl, strict = ci >= ri, ci > ri
    else:
        incl, strict = ci <= ri, ci < ri
    cum = jnp.dot(incl.astype(F32), lw, precision=HI, preferred_element_type=F32)
    tot = cum[0:1] if rev else cum[c - 1:c]
    e_in = jnp.exp(cum)
    e_prev = jnp.exp(cum - lw)
    e_inv = jnp.exp(-cum)
    e_out = jnp.exp(tot - cum)
    rt = (r * e_in).astype(BF16)
    at = (-kk * e_prev).astype(BF16)
    bi = (bb * e_inv).astype(BF16)
    ki = (kd * e_inv).astype(BF16)
    bh = (bb * e_out).astype(BF16)
    kh = (kd * e_out).astype(BF16)
    vb = v.astype(BF16)
    hb = h_state.astype(BF16)
    lane = lax.broadcasted_iota(jnp.int32, (1, LANES), 1)
    eye = (ri == ci).astype(F32)

    ah0 = _dot(at, hb)
    oh0 = _dot(rt, hb)
    u_full = jnp.zeros((c, LANES), F32)
    o_full = oh0
    t_mats, arb_mats = [], []
    for hx in range(2):
        mx = (lane // half) == hx
        at_x = jnp.where(mx, at, jnp.zeros_like(at))
        rt_x = jnp.where(mx, rt, jnp.zeros_like(rt))
        a_ab = jnp.where(strict, _dot_nt(at_x, bi), 0.0)
        a_ak = jnp.where(strict, _dot_nt(at_x, ki), 0.0)
        a_rb = jnp.where(incl, _dot_nt(rt_x, bi), 0.0)
        a_rk = jnp.where(incl, _dot_nt(rt_x, ki), 0.0)
        pm = eye
        m = 1
        while m < c:
            off = jnp.logical_and(ri // (2 * m) == ci // (2 * m), ri // m != ci // m)
            a_off = jnp.where(off, a_ab, 0.0).astype(BF16)
            pmb = pm.astype(BF16)
            pm = pm + _dot(_dot(pmb, a_off).astype(BF16), pmb)
            m *= 2
        x_x = ah0 + _dot(a_ak.astype(BF16), vb)
        u_x = _dot(pm.astype(BF16), x_x.astype(BF16))
        u_full = jnp.where(mx, u_x, u_full)
        o_x = _dot(a_rb.astype(BF16), u_x.astype(BF16)) + _dot(a_rk.astype(BF16), vb)
        o_full = o_full + jnp.where(mx, o_x, 0.0)
    upd = _dot_tn(bh, u_full.astype(BF16)) + _dot_tn(kh, vb)
    blk_r = lax.broadcasted_iota(jnp.int32, (LANES, LANES), 0) // half
    blk_c = lax.broadcasted_iota(jnp.int32, (LANES, LANES), 1) // half
    h_new = jnp.where(blk_r == blk_c, upd, 0.0)
    return o_full, h_new, tot


def _rw_scan_kernel(npairs, *refs):
    ins = refs[:12]
    yf_o, yb_o = refs[12:14]
    st = refs[14]
    s = pl.program_id(1)

    @pl.when(s == 0)
    def _():
        st[...] = jnp.zeros_like(st)

    def body(p, carry):
        ls = pl.ds(pl.multiple_of(p * LANES, LANES), LANES)
        for dr in range(2):
            r_r, v_r, kk_r, bb_r, kd_r, lw_r = ins[6 * dr:6 * dr + 6]
            h_state = st[dr, p]
            o, h_upd, tot = _rw_chunk(
                r_r[0, :, ls].astype(F32), v_r[0, :, ls].astype(F32), kk_r[0, :, ls].astype(F32),
                bb_r[0, 0, :, ls].astype(F32), kd_r[0, 0, :, ls].astype(F32), lw_r[0, 0, :, ls],
                h_state, rev=(dr == 1))
            dec = jnp.exp(tot)
            eye = (lax.broadcasted_iota(jnp.int32, (LANES, LANES), 0)
                   == lax.broadcasted_iota(jnp.int32, (LANES, LANES), 1)).astype(F32)
            dec_col = jnp.sum(eye * dec, axis=1, keepdims=True)
            st[dr, p] = h_state * dec_col + h_upd
            if dr == 0:
                yf_o[0, :, ls] = o
            else:
                yb_o[0, :, ls] = o
        return carry

    lax.fori_loop(0, npairs, body, 0)


def _scan_chunk_maps(n_lat_chunks, n_ctx_chunks):
    total = n_lat_chunks + n_ctx_chunks

    def fwd(s):
        return (s + n_lat_chunks) % total

    def bwd(s):
        return total - 1 - s

    return total, fwd, bwd


def _rw_scan(feats, *, n_lat_chunks, n_ctx_chunks):
    r, v, kk, lw, bb, kd = feats
    b, tp, di = r.shape
    npairs = di // LANES
    total, fwd, bwd = _scan_chunk_maps(n_lat_chunks, n_ctx_chunks)
    in_specs, args = [], []
    for dr, cm in enumerate((fwd, bwd)):
        one = pl.BlockSpec((1, CHUNK, di), lambda bi, s, cm=cm: (bi, cm(s), 0))
        two = pl.BlockSpec((1, 1, CHUNK, di), lambda bi, s, cm=cm, dr=dr: (dr, bi, cm(s), 0))
        in_specs += [one, one, one, two, two, two]
        args += [r, v, kk, bb, kd, lw]
    out_specs = [pl.BlockSpec((1, CHUNK, di), lambda bi, s, cm=cm: (bi, cm(s), 0)) for cm in (fwd, bwd)]
    return pl.pallas_call(
        functools.partial(_rw_scan_kernel, npairs),
        grid=(b, total),
        in_specs=in_specs,
        out_specs=out_specs,
        out_shape=[jax.ShapeDtypeStruct((b, tp, di), F32)] * 2,
        scratch_shapes=[pltpu.VMEM((2, npairs, LANES, LANES), F32)],
        name="rwkv7_scan",
        compiler_params=pltpu.CompilerParams(
            dimension_semantics=("parallel", "arbitrary"), vmem_limit_bytes=VMEM_LIMIT),
    )(*args)


def _residual_tail(yg_bf16, wo_ref, x_ref, mod_ref, pg_ref, o_ref):
    d = x_ref.shape[-1]
    o = _dot(yg_bf16, wo_ref[...])
    ms = jnp.mean(o * o, axis=-1, keepdims=True)
    o = o * lax.rsqrt(ms + NORM_EPS) * pg_ref[...]
    gate = mod_ref[0, 0][:, 2 * d:]
    o_ref[0] = x_ref[0] + gate * o


def _rw_out_kernel(yf_ref, yb_ref, z_ref, bv_ref, lnw_ref, lnb_ref, hs_ref, wo_ref, x_ref, mod_ref, pg_ref,
                   o_ref, yg):
    di = yf_ref.shape[-1]
    w = hs_ref.shape[0]
    inv = 1.0 / RW_HEAD
    for cb in range(di // w):
        cs = slice(cb * w, (cb + 1) * w)
        y = yf_ref[0, :, cs] + yb_ref[0, :, cs]
        mu = jnp.dot(y, hs_ref[...], precision=HI, preferred_element_type=F32) * inv
        yc = y - mu
        var = jnp.dot(yc * yc, hs_ref[...], precision=HI, preferred_element_type=F32) * inv
        yn = yc * lax.rsqrt(var + LN_X_EPS) * lnw_ref[:, cs] + lnb_ref[:, cs] + bv_ref[0, :, cs].astype(F32)
        yg[:, cs] = yn.astype(BF16) * z_ref[0, :, cs]
    _residual_tail(yg[...], wo_ref, x_ref, mod_ref, pg_ref, o_ref)


def _hg_out_kernel(yf_ref, yb_ref, sg_ref, gn_ref, wo_ref, x_ref, mod_ref, pg_ref, o_ref, yg):
    di = yf_ref.shape[-1]
    for hx in range(di // HG_HEAD):
        cs = slice(hx * HG_HEAD, (hx + 1) * HG_HEAD)
        y = yf_ref[0, :, cs] + yb_ref[0, :, cs]
        ms = jnp.mean(y * y, axis=-1, keepdims=True)
        yn = y * lax.rsqrt(ms + NORM_EPS) * gn_ref[...]
        yg[:, cs] = yn.astype(BF16) * sg_ref[0, :, cs]
    _residual_tail(yg[...], wo_ref, x_ref, mod_ref, pg_ref, o_ref)


def _mixer_out(name, kernel_fn, ys, extras, extra_specs, wo, xs, mod, post_g, n_tiles):
    b, tp, d = xs.shape
    di = wo.shape[0]
    per = TILE // OTILE
    tokw = pl.BlockSpec((1, OTILE, di), lambda bi, j: (bi, j, 0))
    tokd = pl.BlockSpec((1, OTILE, d), lambda bi, j: (bi, j, 0))
    in_specs = [tokw, tokw] + extra_specs + [
        pl.BlockSpec((di, d), lambda bi, j: (0, 0)),
        tokd,
        pl.BlockSpec((1, 1, 1, mod.shape[-1]), lambda bi, j: (bi, j // per, 0, 0)),
        pl.BlockSpec((1, d), lambda bi, j: (0, 0)),
    ]
    n_in = len(in_specs)
    return pl.pallas_call(
        kernel_fn,
        grid=(b, n_tiles),
        in_specs=in_specs,
        out_specs=tokd,
        out_shape=jax.ShapeDtypeStruct(xs.shape, F32),
        scratch_shapes=[pltpu.VMEM((OTILE, di), BF16)],
        input_output_aliases={n_in - 3: 0},
        name=name,
        compiler_params=pltpu.CompilerParams(
            dimension_semantics=("parallel", "parallel"), vmem_limit_bytes=VMEM_LIMIT),
    )(*ys, *extras, wo, xs, mod, post_g)


def _hg_feat_kernel(x_ref, mod_ref, g_ref, w_ref, lb_ref, q_o, i_o, sg_o, lf_o, kd_o, hbf):
    n = pl.program_id(2)
    d = x_ref.shape[-1]

    @pl.when(n == 0)
    def _prologue():
        m = mod_ref[0, 0]
        shift, scale = m[:, :d], m[:, d:2 * d]
        g = g_ref[...]

        def norm_rows(rb, c):
            r0 = pl.multiple_of(rb * GRID_W, GRID_W)
            hbf[pl.ds(r0, GRID_W), :] = _adaln(x_ref[0, pl.ds(r0, GRID_W), :], g, scale, shift).astype(BF16)
            return c

        lax.fori_loop(0, TILE // GRID_W, norm_rows, 0)

    h = hbf[...]
    lb = lb_ref[...]
    q_o[0] = _silu(_dot(h, w_ref[0])).astype(q_o.dtype)
    for dr in range(2):
        f = lb + (1.0 - lb) * jax.nn.sigmoid(_dot(h, w_ref[1 + dr]))
        kd_o[dr, 0] = (1.0 - f).astype(kd_o.dtype)
        lf_o[dr, 0] = jnp.log(f)
    i_o[0] = _dot(h, w_ref[3]).astype(i_o.dtype)
    sg_o[0] = _silu(_dot(h, w_ref[4])).astype(sg_o.dtype)


def _hg_features(xs, mod, pre_g, w_in5, lb, *, tn=256):
    b, tp, d = xs.shape
    di = w_in5.shape[-1]
    nt = di // tn
    one = pl.BlockSpec((1, TILE, tn), lambda bi, j, n: (bi, j, n))
    two = pl.BlockSpec((2, 1, TILE, tn), lambda bi, j, n: (0, bi, j, n))
    s1 = lambda dt: jax.ShapeDtypeStruct((b, tp, di), dt)
    s2 = lambda dt: jax.ShapeDtypeStruct((2, b, tp, di), dt)
    return pl.pallas_call(
        _hg_feat_kernel,
        grid=(b, tp // TILE, nt),
        in_specs=[
            pl.BlockSpec((1, TILE, d), lambda bi, j, n: (bi, j, 0)),
            pl.BlockSpec((1, 1, 1, mod.shape[-1]), lambda bi, j, n: (bi, j, 0, 0)),
            pl.BlockSpec((1, d), lambda bi, j, n: (0, 0)),
            pl.BlockSpec((5, d, tn), lambda bi, j, n: (0, 0, n)),
            pl.BlockSpec((1, tn), lambda bi, j, n: (0, n)),
        ],
        out_specs=[one, one, one, two, two],
        out_shape=[s1(BF16), s1(BF16), s1(BF16), s2(F32), s2(BF16)],
        scratch_shapes=[pltpu.VMEM((TILE, d), BF16)],
        name="hgrn2_features",
        compiler_params=pltpu.CompilerParams(
            dimension_semantics=("parallel", "parallel", "arbitrary"), vmem_limit_bytes=VMEM_LIMIT),
    )(xs, mod, pre_g, w_in5, lb)


def _hg_chunk(q, k, v, lf, s_state, rev):
    c = CHUNK
    sub = 8
    nv = c // sub
    row = lax.broadcasted_iota(jnp.int32, (c, 1), 0)
    ri = lax.broadcasted_iota(jnp.int32, (c, c), 0)
    ci = lax.broadcasted_iota(jnp.int32, (c, c), 1)
    p = lf
    att = jnp.zeros((c, c), F32)
    m = 1
    while m < c:
        in_hi = ((row // m) % 2) == 1
        late = jnp.logical_not(in_hi) if rev else in_hi
        if m >= sub:
            pieces = []
            for blk in range(c // (2 * m)):
                bidx = blk * 2 * m + (m if rev else m - 1)
                pieces.append(jnp.broadcast_to(p[bidx:bidx + 1, :], (2 * m, p.shape[1])))
            tt = pieces[0] if len(pieces) == 1 else jnp.concatenate(pieces, axis=0)
        else:
            p3 = p.reshape(nv, sub, p.shape[1])
            srow = lax.broadcasted_iota(jnp.int32, (1, sub, 1), 1)
            tt3 = None
            for blk in range(sub // (2 * m)):
                bidx = blk * 2 * m + (m if rev else m - 1)
                cand = jnp.broadcast_to(p3[:, bidx:bidx + 1, :], p3.shape)
                tt3 = cand if tt3 is None else jnp.where(srow // (2 * m) == blk, cand, tt3)
            tt = tt3.reshape(c, p.shape[1])
        ee = jnp.exp(jnp.where(late, p, tt - p))
        qt = jnp.where(late, q * ee, 0.0).astype(BF16)
        kt = jnp.where(late, 0.0, k * ee).astype(BF16)
        same = (ri // (2 * m)) == (ci // (2 * m))
        att = att + jnp.where(same, _dot_nt(qt, kt), 0.0)
        p = jnp.where(late, p + tt, p)
        m *= 2
    tot = p[0:1] if rev else p[c - 1:c]
    vb = v.astype(BF16)
    y = (_dot(att.astype(BF16), vb) + jnp.sum(q * k, axis=1, keepdims=True) * v
         + _dot((q * jnp.exp(p)).astype(BF16), s_state.astype(BF16)))
    upd = _dot_tn((k * jnp.exp(tot - p)).astype(BF16), vb)
    return y, upd, tot


def _hg_scan_kernel(nheads, *refs):
    ins = refs[:8]
    yf_o, yb_o = refs[8:10]
    st = refs[10]
    s = pl.program_id(1)

    @pl.when(s == 0)
    def _():
        st[...] = jnp.zeros_like(st)

    eye = (lax.broadcasted_iota(jnp.int32, (LANES, LANES), 0)
           == lax.broadcasted_iota(jnp.int32, (LANES, LANES), 1)).astype(F32)

    def body(hx, carry):
        ls = pl.ds(pl.multiple_of(hx * LANES, LANES), LANES)
        for dr in range(2):
            q_r, v_r, kd_r, lf_r = ins[4 * dr:4 * dr + 4]
            s_state = st[dr, hx]
            y, upd, tot = _hg_chunk(q_r[0, :, ls].astype(F32), kd_r[0, 0, :, ls].astype(F32),
                                    v_r[0, :, ls].astype(F32), lf_r[0, 0, :, ls], s_state, rev=(dr == 1))
            dec_col = jnp.sum(eye * jnp.exp(tot), axis=1, keepdims=True)
            st[dr, hx] = s_state * dec_col + upd
            if dr == 0:
                yf_o[0, :, ls] = y
            else:
                yb_o[0, :, ls] = y
        return carry

    lax.fori_loop(0, nheads, body, 0)


def _hg_scan(q, v, lf, kd, *, n_lat_chunks, n_ctx_chunks):
    b, tp, di = q.shape
    nheads = di // HG_HEAD
    total, fwd, bwd = _scan_chunk_maps(n_lat_chunks, n_ctx_chunks)
    in_specs, args = [], []
    for dr, cm in enumerate((fwd, bwd)):
        one = pl.BlockSpec((1, CHUNK, di), lambda bi, s, cm=cm: (bi, cm(s), 0))
        two = pl.BlockSpec((1, 1, CHUNK, di), lambda bi, s, cm=cm, dr=dr: (dr, bi, cm(s), 0))
        in_specs += [one, one, two, two]
        args += [q, v, kd, lf]
    out_specs = [pl.BlockSpec((1, CHUNK, di), lambda bi, s, cm=cm: (bi, cm(s), 0)) for cm in (fwd, bwd)]
    return pl.pallas_call(
        functools.partial(_hg_scan_kernel, nheads),
        grid=(b, total),
        in_specs=in_specs,
        out_specs=out_specs,
        out_shape=[jax.ShapeDtypeStruct((b, tp, di), F32)] * 2,
        scratch_shapes=[pltpu.VMEM((2, nheads, LANES, LANES), F32)],
        name="hgrn2_scan",
        compiler_params=pltpu.CompilerParams(
            dimension_semantics=("parallel", "arbitrary"), vmem_limit_bytes=VMEM_LIMIT),
    )(*args)


def _pad_rows(w, rows):
    return jnp.pad(w, ((0, rows - w.shape[0]), (0, 0)))


def _pad_cols(w, cols):
    return jnp.pad(w, ((0, 0), (0, cols - w.shape[1])))


def kernel(x, c, ctx, c_ctx, mod_w, mod_b, pre_g, post_g, rw_mix, rw_proj, rw_wo, rw_w0, rw_w1, rw_w2, rw_a0,
           rw_a1, rw_a2, rw_v0, rw_v1, rw_v2, rw_kk, rw_ka, rw_rk, rw_lnw, rw_lnb, hg_win, hg_wo, hg_gn, hg_lb):
    b, seq, d = x.shape
    ctx_len = ctx.shape[1]
    depth = mod_w.shape[0]
    di = rw_proj.shape[-1]
    assert seq % TILE == 0 and ctx_len % CHUNK == 0 and ctx_len <= TILE and seq % GRID_W == 0
    n_lat_tiles = seq // TILE
    tp = seq + TILE
    n_lat_chunks, n_ctx_chunks = seq // CHUNK, ctx_len // CHUNK

    xs = jnp.concatenate([x, ctx, jnp.zeros((b, tp - seq - ctx_len, d), x.dtype)], axis=1)

    rows = ((b + 1 + 7) // 8) * 8
    cond = jnp.concatenate([c, c_ctx[None], jnp.zeros((rows - b - 1, d), c.dtype)], axis=0)
    mod_all = _modulation(cond, mod_w, mod_b)
    mod_lat = jnp.broadcast_to(mod_all[:, :b, None, :], (depth, b, n_lat_tiles, 3 * d))
    mod_ctx = jnp.broadcast_to(mod_all[:, b:b + 1, None, :], (depth, b, 1, 3 * d))
    mod_t = jnp.concatenate([mod_lat, mod_ctx], axis=2)[:, :, :, None, :]

    p_lb = jax.nn.softmax(hg_lb.astype(F32), axis=0)
    lb_all = jnp.cumsum(p_lb, axis=0) - p_lb[0]

    vf = None
    for i in range(depth):
        ctx_out = i < depth - 1
        n_out_tiles = (seq + (ctx_len if ctx_out else 0)) // OTILE
        j = i // 2
        mod_i = mod_t[i]
        pg = pre_g[i][None]
        if i % 2 == 0:
            r64 = rw_w1.shape[-1]
            w2 = jnp.stack([jnp.pad(rw_w2[j, 0], ((0, LANES - r64), (0, 0))),
                            jnp.pad(rw_w2[j, 1], ((r64, LANES - 2 * r64), (0, 0)))])
            a2 = jnp.stack([jnp.pad(rw_a2[j, 0], ((0, LANES - r64), (0, 0))),
                            jnp.pad(rw_a2[j, 1], ((r64, LANES - 2 * r64), (0, 0)))])
            if j == 0:
                v0 = jnp.zeros((di,), F32)
                v1 = jnp.zeros((d, LANES), BF16)
                v2 = jnp.zeros((LANES, di), BF16)
            else:
                v0 = rw_v0[j - 1]
                v1 = _pad_cols(rw_v1[j - 1], LANES).astype(BF16)
                v2 = _pad_rows(rw_v2[j - 1], LANES).astype(BF16)
            p = {
                "mix": rw_mix[j],
                "proj": rw_proj[j].astype(BF16),
                "w1": jnp.concatenate([rw_w1[j, 0], rw_w1[j, 1]], axis=1).astype(BF16),
                "a1": jnp.concatenate([rw_a1[j, 0], rw_a1[j, 1]], axis=1).astype(BF16),
                "v1": v1,
                "w2": w2.astype(BF16),
                "a2": a2.astype(BF16),
                "v2": v2,
                "vec": jnp.stack([rw_w0[j, 0], rw_w0[j, 1], rw_a0[j, 0], rw_a0[j, 1], v0, rw_kk[j], rw_ka[j],
                                  rw_rk[j]]),
            }
            r, v, kk, z, bv, lw, bb, kd = _rw_features(xs, mod_i, pg, p, vf if j > 0 else None,
                                                       n_lat_tiles=n_lat_tiles, ctx_len=ctx_len)
            if j == 0:
                vf = v
            yf, yb = _rw_scan((r, v, kk, lw, bb, kd), n_lat_chunks=n_lat_chunks, n_ctx_chunks=n_ctx_chunks)
            hsw = 256
            tokw = pl.BlockSpec((1, OTILE, di), lambda bi, jj: (bi, jj, 0))
            extras = [z, bv, rw_lnw[j][None], rw_lnb[j][None], _head_sum_matrix(hsw, RW_HEAD)]
            especs = [tokw, tokw, pl.BlockSpec((1, di), lambda bi, jj: (0, 0)),
                      pl.BlockSpec((1, di), lambda bi, jj: (0, 0)), pl.BlockSpec((hsw, hsw), lambda bi, jj: (0, 0))]
            xs = _mixer_out("rwkv7_out", _rw_out_kernel,(yf, yb), extras, especs, rw_wo[j].astype(BF16), xs, mod_i,
                            post_g[i][None], n_out_tiles)
        else:
            w5 = hg_win[j].reshape(d, 5, di).transpose(1, 0, 2).astype(BF16)
            q, iv, sg, lf, kd = _hg_features(xs, mod_i, pg, w5, lb_all[i][None])
            yf, yb = _hg_scan(q, iv, lf, kd, n_lat_chunks=n_lat_chunks, n_ctx_chunks=n_ctx_chunks)
            tokw = pl.BlockSpec((1, OTILE, di), lambda bi, jj: (bi, jj, 0))
            extras = [sg, hg_gn[j][None]]
            especs = [tokw, pl.BlockSpec((1, HG_HEAD), lambda bi, jj: (0, 0))]
            xs = _mixer_out("hgrn2_out", _hg_out_kernel,(yf, yb), extras, especs, hg_wo[j].astype(BF16), xs, mod_i,
                            post_g[i][None], n_out_tiles)
    return xs[:, :seq]
```

```python
import functools
import math

import jax
import jax.numpy as jnp
from jax import lax
from jax.experimental import pallas as pl
from jax.experimental.pallas import tpu as pltpu

F32 = jnp.float32
BF16 = jnp.bfloat16
HI = lax.Precision.HIGHEST

GRID_W = 64
RW_HEAD = 64
HG_HEAD = 128
CHUNK = 64
TILE = 512
OTILE = 256
NORM_EPS = 1e-6
LN_X_EPS = 64e-5
LANES = 128
VMEM_LIMIT = 56 * 1024 * 1024


def _silu(x):
    return x * jax.nn.sigmoid(x)


def _dot(a, b):
    return jnp.dot(a, b, preferred_element_type=F32)


def _mod_kernel(s_ref, w_ref, b_ref, o_ref):
    s = _silu(s_ref[...])
    o_ref[0] = jnp.dot(s, w_ref[0], precision=HI, preferred_element_type=F32) + b_ref[0]


def _modulation(cond, mod_w, mod_b):
    depth, d, d3 = mod_w.shape
    rows = cond.shape[0]
    nb = d3 // d
    return pl.pallas_call(
        _mod_kernel,
        grid=(depth, nb),
        in_specs=[
            pl.BlockSpec((rows, d), lambda i, n: (0, 0)),
            pl.BlockSpec((1, d, d), lambda i, n: (i, 0, n)),
            pl.BlockSpec((1, 1, d), lambda i, n: (i, 0, n)),
        ],
        out_specs=pl.BlockSpec((1, rows, d), lambda i, n: (i, 0, n)),
        out_shape=jax.ShapeDtypeStruct((depth, rows, d3), F32),
        name="adaln_modulation",
        compiler_params=pltpu.CompilerParams(vmem_limit_bytes=VMEM_LIMIT),
    )(cond, mod_w, mod_b.reshape(depth, 1, d3))


def _adaln(xb, g, scale, shift):
    ms = jnp.mean(xb * xb, axis=-1, keepdims=True)
    return xb * lax.rsqrt(ms + NORM_EPS) * g * (1.0 + scale) + shift


def _store_groups(o_ref, lead, val):
    for gq in range(val.shape[1] // LANES):
        o_ref[lead + (gq,)] = val[:, gq * LANES:(gq + 1) * LANES].astype(o_ref.dtype)


def _load_groups(ref, lead):
    n = ref.shape[len(lead)]
    return jnp.concatenate([ref[lead + (gq,)] for gq in range(n)], axis=1)


def _head_sum_matrix(width, head):
    r = lax.broadcasted_iota(jnp.int32, (width, width), 0) // head
    c = lax.broadcasted_iota(jnp.int32, (width, width), 1) // head
    return (r == c).astype(F32)


def _rw_feat_kernel(n_lat_tiles, ctx_len, has_vres, *refs):
    (x_ref, xp_ref, xn_ref, mod_ref, g_ref, mix_ref, w1_ref, a1_ref, v1_ref,
     proj_ref, w2_ref, a2_ref, v2_ref, vec_ref, hs_ref) = refs[:15]
    pos = 15
    vf_ref = None
    if has_vres:
        vf_ref = refs[pos]
        pos += 1
    (r_o, v_o, kk_o, z_o, bv_o, lw_o, bb_o, kd_o) = refs[pos:pos + 8]
    hbuf, xm, tw, ta, tv = refs[pos + 8:]

    j = pl.program_id(1)
    n = pl.program_id(2)
    d = x_ref.shape[-1]
    q4 = d // 4
    nblk = TILE // GRID_W

    @pl.when(n == 0)
    def _prologue():
        m = mod_ref[0, 0]
        shift, scale = m[:, :d], m[:, d:2 * d]
        g = g_ref[...]
        is_lat = j < n_lat_tiles
        prev_ok = jnp.logical_and(j >= 1, is_lat)
        next_ok = j < n_lat_tiles - 1

        def norm_rows(rb, c):
            r0 = pl.multiple_of(rb * GRID_W, GRID_W)
            hbuf[pl.ds(r0 + GRID_W, GRID_W), :] = _adaln(x_ref[0, pl.ds(r0, GRID_W), :], g, scale, shift)
            return c

        lax.fori_loop(0, nblk, norm_rows, 0)
        hp = _adaln(xp_ref[0], g, scale, shift)
        hbuf[0:GRID_W, :] = jnp.where(prev_ok, hp, 0.0)
        hn = _adaln(xn_ref[0], g, scale, shift)
        hbuf[TILE + GRID_W:TILE + 2 * GRID_W, :] = jnp.where(next_ok, hn, 0.0)

        row = lax.broadcasted_iota(jnp.int32, (GRID_W, 1), 0)

        def store_mixed(r0, cs, h_c, hs_c):
            dd = hs_c - h_c
            for k in range(6):
                xm[k, pl.ds(r0, GRID_W), cs] = (h_c + dd * mix_ref[k:k + 1, cs]).astype(BF16)

        def mix_lat(rb, c):
            r0 = pl.multiple_of(rb * GRID_W, GRID_W)
            for qi in range(4):
                cs = slice(qi * q4, (qi + 1) * q4)
                h_c = hbuf[pl.ds(r0 + GRID_W, GRID_W), cs]
                if qi == 0:
                    hs_c = jnp.where(row == 0, 0.0, pltpu.roll(h_c, 1, axis=0))
                elif qi == 1:
                    hs_c = jnp.where(row == GRID_W - 1, 0.0, pltpu.roll(h_c, GRID_W - 1, axis=0))
                elif qi == 2:
                    hs_c = hbuf[pl.ds(r0, GRID_W), cs]
                else:
                    hs_c = hbuf[pl.ds(r0 + 2 * GRID_W, GRID_W), cs]
                store_mixed(r0, cs, h_c, hs_c)
            return c

        def mix_ctx(rb, c):
            r0 = pl.multiple_of(rb * GRID_W, GRID_W)
            t = row + rb * GRID_W
            for qi in range(4):
                cs = slice(qi * q4, (qi + 1) * q4)
                h_c = hbuf[pl.ds(r0 + GRID_W, GRID_W), cs]
                if qi < 2:
                    edge = hbuf[pl.ds(r0 + GRID_W - 1, 1), cs]
                    hs_c = jnp.where(row == 0, edge, pltpu.roll(h_c, 1, axis=0))
                else:
                    edge = hbuf[pl.ds(r0 + 2 * GRID_W, 1), cs]
                    hs_c = jnp.where(row == GRID_W - 1, edge, pltpu.roll(h_c, GRID_W - 1, axis=0))
                    hs_c = jnp.where(t == ctx_len - 1, 0.0, hs_c)
                store_mixed(r0, cs, h_c, hs_c)
            return c

        @pl.when(is_lat)
        def _():
            lax.fori_loop(0, nblk, mix_lat, 0)

        @pl.when(jnp.logical_not(is_lat))
        def _():
            lax.fori_loop(0, nblk, mix_ctx, 0)

        tw[...] = jnp.tanh(_dot(xm[1], w1_ref[...])).astype(BF16)
        ta[...] = _dot(xm[4], a1_ref[...]).astype(BF16)
        if has_vres:
            tv[...] = _dot(xm[3], v1_ref[...]).astype(BF16)

    vec = vec_ref[...]
    r = _dot(xm[0], proj_ref[0])
    k = _dot(xm[2], proj_ref[1])
    v = _dot(xm[3], proj_ref[2])
    z = _silu(_dot(xm[5], proj_ref[3]))
    if has_vres:
        gate = jax.nn.sigmoid(vec[4:5] + _dot(tv[...], v2_ref[...]))
        v = v + (_load_groups(vf_ref, (0,)).astype(F32) - v) * gate
    hs = hs_ref[...]
    kk = k * vec[5:6]
    ss = jnp.dot(kk * kk, hs, precision=HI, preferred_element_type=F32)
    kk = kk * lax.rsqrt(jnp.maximum(ss, 1e-24))
    decay_scale = -math.exp(-0.5)
    ksum = None
    for dr in range(2):
        u = vec[dr:dr + 1] + _dot(tw[...], w2_ref[dr])
        _store_groups(lw_o, (dr, 0), decay_scale * jax.nn.sigmoid(u))
        a = jax.nn.sigmoid(vec[2 + dr:3 + dr] + _dot(ta[...], a2_ref[dr]))
        kd = k * (1.0 + (a - 1.0) * vec[6:7])
        _store_groups(bb_o, (dr, 0), kk * a)
        _store_groups(kd_o, (dr, 0), kd)
        ksum = kd if ksum is None else ksum + kd
    bonus = jnp.dot(r * ksum * vec[7:8], hs, precision=HI, preferred_element_type=F32)
    _store_groups(r_o, (0,), r)
    _store_groups(v_o, (0,), v)
    _store_groups(kk_o, (0,), kk)
    _store_groups(z_o, (0,), z)
    _store_groups(bv_o, (0,), bonus * v)


def _group_major_specs(b, tp, di, tn):
    gq = tn // LANES
    one = pl.BlockSpec((1, gq, TILE, LANES), lambda bi, j, n: (bi, n, j, 0))
    two = pl.BlockSpec((2, 1, gq, TILE, LANES), lambda bi, j, n: (0, bi, n, j, 0))
    s1 = lambda dt: jax.ShapeDtypeStruct((b, di // LANES, tp, LANES), dt)
    s2 = lambda dt: jax.ShapeDtypeStruct((2, b, di // LANES, tp, LANES), dt)
    return one, two, s1, s2


def _rw_features(xs, mod, pre_g, p, vf, *, n_lat_tiles, ctx_len, tn=256):
    b, tp, d = xs.shape
    di = p["proj"].shape[-1]
    nt = di // tn
    ntiles = tp // TILE
    nb64 = tp // GRID_W
    per = TILE // GRID_W
    has_vres = vf is not None

    def tok(bi, j, n):
        return (bi, j, 0)

    in_specs = [
        pl.BlockSpec((1, TILE, d), tok),
        pl.BlockSpec((1, GRID_W, d), lambda bi, j, n: (bi, jnp.maximum(j * per - 1, 0), 0)),
        pl.BlockSpec((1, GRID_W, d), lambda bi, j, n: (bi, jnp.minimum((j + 1) * per, nb64 - 1), 0)),
        pl.BlockSpec((1, 1, 1, mod.shape[-1]), lambda bi, j, n: (bi, j, 0, 0)),
        pl.BlockSpec((1, d), lambda bi, j, n: (0, 0)),
        pl.BlockSpec((6, d), lambda bi, j, n: (0, 0)),
        pl.BlockSpec((d, LANES), lambda bi, j, n: (0, 0)),
        pl.BlockSpec((d, LANES), lambda bi, j, n: (0, 0)),
        pl.BlockSpec((d, LANES), lambda bi, j, n: (0, 0)),
        pl.BlockSpec((4, d, tn), lambda bi, j, n: (0, 0, n)),
        pl.BlockSpec((2, LANES, tn), lambda bi, j, n: (0, 0, n)),
        pl.BlockSpec((2, LANES, tn), lambda bi, j, n: (0, 0, n)),
        pl.BlockSpec((LANES, tn), lambda bi, j, n: (0, n)),
        pl.BlockSpec((8, tn), lambda bi, j, n: (0, n)),
        pl.BlockSpec((tn, tn), lambda bi, j, n: (0, 0)),
    ]
    args = [xs, xs, xs, mod, pre_g, p["mix"], p["w1"], p["a1"], p["v1"], p["proj"], p["w2"], p["a2"],
            p["v2"], p["vec"], _head_sum_matrix(tn, RW_HEAD)]
    one, two, s1, s2 = _group_major_specs(b, tp, di, tn)
    if has_vres:
        in_specs.append(one)
        args.append(vf)
    return pl.pallas_call(
        functools.partial(_rw_feat_kernel, n_lat_tiles, ctx_len, has_vres),
        grid=(b, ntiles, nt),
        in_specs=in_specs,
        out_specs=[one, one, one, one, one, two, two, two],
        out_shape=[s1(BF16), s1(BF16), s1(BF16), s1(BF16), s1(BF16), s2(F32), s2(BF16), s2(BF16)],
        scratch_shapes=[
            pltpu.VMEM((TILE + 2 * GRID_W, d), F32),
            pltpu.VMEM((6, TILE, d), BF16),
            pltpu.VMEM((TILE, LANES), BF16),
            pltpu.VMEM((TILE, LANES), BF16),
            pltpu.VMEM((TILE, LANES), BF16),
        ],
        name="rwkv7_features",
        compiler_params=pltpu.CompilerParams(
            dimension_semantics=("parallel", "parallel", "arbitrary"), vmem_limit_bytes=VMEM_LIMIT),
    )(*args)


def _bmm(a, b):
    return lax.dot_general(a, b, (((2,), (1,)), ((0,), (0,))), preferred_element_type=F32)


def _bmm_nt(a, b):
    return lax.dot_general(a, b, (((2,), (2,)), ((0,), (0,))), preferred_element_type=F32)


def _bmm_tn(a, b):
    return lax.dot_general(a, b, (((1,), (1,)), ((0,), (0,))), preferred_element_type=F32)


def _scan_chunk_maps(n_lat_chunks, n_ctx_chunks):
    total = n_lat_chunks + n_ctx_chunks

    def fwd(s):
        return (s + n_lat_chunks) % total

    def bwd(s):
        return total - 1 - s

    return total, fwd, bwd


def _scan_specs(b, ng, tp, n_lat_chunks, n_ctx_chunks, n_one, n_two):
    total, fwd, bwd = _scan_chunk_maps(n_lat_chunks, n_ctx_chunks)
    in_specs = []
    for dr, cm in enumerate((fwd, bwd)):
        one = pl.BlockSpec((1, ng, CHUNK, LANES), lambda bi, s, cm=cm: (bi, 0, cm(s), 0))
        two = pl.BlockSpec((1, 1, ng, CHUNK, LANES), lambda bi, s, cm=cm, dr=dr: (dr, bi, 0, cm(s), 0))
        in_specs += [one] * n_one + [two] * n_two
    out_specs = [pl.BlockSpec((1, ng, CHUNK, LANES), lambda bi, s, cm=cm: (bi, 0, cm(s), 0)) for cm in (fwd, bwd)]
    out_shape = [jax.ShapeDtypeStruct((b, ng, tp, LANES), F32)] * 2
    return total, in_specs, out_specs, out_shape


def _rw_chunk(r, v, kk, bb, kd, lw, st, rev):
    c = CHUNK
    half = RW_HEAD
    npair = r.shape[0]
    ri = lax.broadcasted_iota(jnp.int32, (1, c, c), 1)
    ci = lax.broadcasted_iota(jnp.int32, (1, c, c), 2)
    if rev:
        incl, strict = ci >= ri, ci > ri
    else:
        incl, strict = ci <= ri, ci < ri
    tri = incl[0].astype(F32)
    cum = jnp.stack([jnp.dot(tri, lw[p], precision=HI, preferred_element_type=F32) for p in range(npair)])
    tot = cum[:, 0:1] if rev else cum[:, c - 1:c]
    e_in = jnp.exp(cum)
    e_prev = jnp.exp(cum - lw)
    e_inv = jnp.exp(-cum)
    e_out = jnp.exp(tot - cum)
    rt = (r * e_in).astype(BF16)
    at = (-kk * e_prev).astype(BF16)
    bi = (bb * e_inv).astype(BF16)
    ki = (kd * e_inv).astype(BF16)
    bh = (bb * e_out).astype(BF16)
    kh = (kd * e_out).astype(BF16)
    vb = v.astype(BF16)
    sb = st.astype(BF16)
    lane = lax.broadcasted_iota(jnp.int32, (1, 1, LANES), 2)
    eye = (ri == ci).astype(F32)

    ah0 = _bmm_nt(at, sb)
    o_full = _bmm_nt(rt, sb)
    u_full = jnp.zeros((npair, c, LANES), F32)
    zero = jnp.zeros_like(at)
    for hx in range(2):
        mx = (lane // half) == hx
        lhs = jnp.concatenate([jnp.where(mx, at, zero), jnp.where(mx, rt, zero)], axis=1)
        m_b = _bmm_nt(lhs, bi)
        m_k = _bmm_nt(lhs, ki)
        a_ab = jnp.where(strict, m_b[:, :c], 0.0)
        a_ak = jnp.where(strict, m_k[:, :c], 0.0)
        a_rb = jnp.where(incl, m_b[:, c:], 0.0)
        a_rk = jnp.where(incl, m_k[:, c:], 0.0)
        pm = eye + jnp.where(ri // 2 == ci // 2, a_ab, 0.0)
        m = 2
        while m < c:
            off = jnp.logical_and(ri // (2 * m) == ci // (2 * m), ri // m != ci // m)
            a_off = jnp.where(off, a_ab, 0.0).astype(BF16)
            pmb = pm.astype(BF16)
            pm = pm + _bmm(_bmm(pmb, a_off).astype(BF16), pmb)
            m *= 2
        x_x = ah0 + _bmm(a_ak.astype(BF16), vb)
        u_x = _bmm(pm.astype(BF16), x_x.astype(BF16))
        u_full = jnp.where(mx, u_x, u_full)
        o_x = _bmm(a_rb.astype(BF16), u_x.astype(BF16)) + _bmm(a_rk.astype(BF16), vb)
        o_full = o_full + jnp.where(mx, o_x, 0.0)
    upd = _bmm_tn(u_full.astype(BF16), bh) + _bmm_tn(vb, kh)
    blk_r = lax.broadcasted_iota(jnp.int32, (1, LANES, LANES), 1) // half
    blk_c = lax.broadcasted_iota(jnp.int32, (1, LANES, LANES), 2) // half
    st_new = st * jnp.exp(tot) + jnp.where(blk_r == blk_c, upd, 0.0)
    return o_full, st_new


def _rw_scan_kernel(*refs):
    ins = refs[:12]
    y_os = refs[12:14]
    st = refs[14]

    @pl.when(pl.program_id(1) == 0)
    def _():
        st[...] = jnp.zeros_like(st)

    loaded = []
    for dr in range(2):
        r_r, v_r, kk_r, bb_r, kd_r, lw_r = ins[6 * dr:6 * dr + 6]
        loaded.append((r_r[0].astype(F32), v_r[0].astype(F32), kk_r[0].astype(F32), bb_r[0, 0].astype(F32),
                       kd_r[0, 0].astype(F32), lw_r[0, 0], st[dr]))
    results = [_rw_chunk(*loaded[dr], rev=(dr == 1)) for dr in range(2)]
    for dr in range(2):
        y_os[dr][0] = results[dr][0]
        st[dr] = results[dr][1]


def _rw_scan(feats, *, n_lat_chunks, n_ctx_chunks):
    r, v, kk, lw, bb, kd = feats
    b, ng, tp, _ = r.shape
    total, in_specs, out_specs, out_shape = _scan_specs(b, ng, tp, n_lat_chunks, n_ctx_chunks, 3, 3)
    return pl.pallas_call(
        _rw_scan_kernel,
        grid=(b, total),
        in_specs=in_specs,
        out_specs=out_specs,
        out_shape=out_shape,
        scratch_shapes=[pltpu.VMEM((2, ng, LANES, LANES), F32)],
        name="rwkv7_scan",
        compiler_params=pltpu.CompilerParams(
            dimension_semantics=("parallel", "arbitrary"), vmem_limit_bytes=VMEM_LIMIT),
    )(r, v, kk, bb, kd, lw, r, v, kk, bb, kd, lw)


def _residual_tail(yg_bf16, wo_ref, x_ref, mod_ref, pg_ref, o_ref):
    d = x_ref.shape[-1]
    o = _dot(yg_bf16, wo_ref[...])
    ms = jnp.mean(o * o, axis=-1, keepdims=True)
    o = o * lax.rsqrt(ms + NORM_EPS) * pg_ref[...]
    gate = mod_ref[0, 0][:, 2 * d:]
    o_ref[0] = x_ref[0] + gate * o


def _rw_out_kernel(yf_ref, yb_ref, z_ref, bv_ref, lnw_ref, lnb_ref, hs_ref, wo_ref, x_ref, mod_ref, pg_ref,
                   o_ref, yg):
    inv = 1.0 / RW_HEAD
    hs = hs_ref[...]
    for gq in range(yf_ref.shape[1]):
        cs = slice(gq * LANES, (gq + 1) * LANES)
        y = yf_ref[0, gq] + yb_ref[0, gq]
        mu = jnp.dot(y, hs, precision=HI, preferred_element_type=F32) * inv
        yc = y - mu
        var = jnp.dot(yc * yc, hs, precision=HI, preferred_element_type=F32) * inv
        yn = yc * lax.rsqrt(var + LN_X_EPS) * lnw_ref[:, cs] + lnb_ref[:, cs] + bv_ref[0, gq].astype(F32)
        yg[:, cs] = yn.astype(BF16) * z_ref[0, gq]
    _residual_tail(yg[...], wo_ref, x_ref, mod_ref, pg_ref, o_ref)


def _hg_out_kernel(yf_ref, yb_ref, sg_ref, gn_ref, wo_ref, x_ref, mod_ref, pg_ref, o_ref, yg):
    for hx in range(yf_ref.shape[1]):
        cs = slice(hx * LANES, (hx + 1) * LANES)
        y = yf_ref[0, hx] + yb_ref[0, hx]
        ms = jnp.mean(y * y, axis=-1, keepdims=True)
        yn = y * lax.rsqrt(ms + NORM_EPS) * gn_ref[...]
        yg[:, cs] = yn.astype(BF16) * sg_ref[0, hx]
    _residual_tail(yg[...], wo_ref, x_ref, mod_ref, pg_ref, o_ref)


def _out_group_spec(di):
    return pl.BlockSpec((1, di // LANES, OTILE, LANES), lambda bi, j: (bi, 0, j, 0))


def _mixer_out(name, kernel_fn, ys, extras, extra_specs, wo, xs, mod, post_g, n_tiles):
    b, tp, d = xs.shape
    di = wo.shape[0]
    per = TILE // OTILE
    tokd = pl.BlockSpec((1, OTILE, d), lambda bi, j: (bi, j, 0))
    in_specs = [_out_group_spec(di)] * 2 + extra_specs + [
        pl.BlockSpec((di, d), lambda bi, j: (0, 0)),
        tokd,
        pl.BlockSpec((1, 1, 1, mod.shape[-1]), lambda bi, j: (bi, j // per, 0, 0)),
        pl.BlockSpec((1, d), lambda bi, j: (0, 0)),
    ]
    n_in = len(in_specs)
    return pl.pallas_call(
        kernel_fn,
        grid=(b, n_tiles),
        in_specs=in_specs,
        out_specs=tokd,
        out_shape=jax.ShapeDtypeStruct(xs.shape, F32),
        scratch_shapes=[pltpu.VMEM((OTILE, di), BF16)],
        input_output_aliases={n_in - 3: 0},
        name=name,
        compiler_params=pltpu.CompilerParams(
            dimension_semantics=("parallel", "parallel"), vmem_limit_bytes=VMEM_LIMIT),
    )(*ys, *extras, wo, xs, mod, post_g)


def _hg_feat_kernel(x_ref, mod_ref, g_ref, w_ref, lb_ref, q_o, i_o, sg_o, lf_o, kd_o, hbf):
    n = pl.program_id(2)
    d = x_ref.shape[-1]

    @pl.when(n == 0)
    def _prologue():
        m = mod_ref[0, 0]
        shift, scale = m[:, :d], m[:, d:2 * d]
        g = g_ref[...]

        def norm_rows(rb, c):
            r0 = pl.multiple_of(rb * GRID_W, GRID_W)
            hbf[pl.ds(r0, GRID_W), :] = _adaln(x_ref[0, pl.ds(r0, GRID_W), :], g, scale, shift).astype(BF16)
            return c

        lax.fori_loop(0, TILE // GRID_W, norm_rows, 0)

    h = hbf[...]
    lb = lb_ref[...]
    _store_groups(q_o, (0,), _silu(_dot(h, w_ref[0])))
    for dr in range(2):
        f = lb + (1.0 - lb) * jax.nn.sigmoid(_dot(h, w_ref[1 + dr]))
        _store_groups(kd_o, (dr, 0), 1.0 - f)
        _store_groups(lf_o, (dr, 0), jnp.log(f))
    _store_groups(i_o, (0,), _dot(h, w_ref[3]))
    _store_groups(sg_o, (0,), _silu(_dot(h, w_ref[4])))


def _hg_features(xs, mod, pre_g, w_in5, lb, *, tn=256):
    b, tp, d = xs.shape
    di = w_in5.shape[-1]
    nt = di // tn
    one, two, s1, s2 = _group_major_specs(b, tp, di, tn)
    return pl.pallas_call(
        _hg_feat_kernel,
        grid=(b, tp // TILE, nt),
        in_specs=[
            pl.BlockSpec((1, TILE, d), lambda bi, j, n: (bi, j, 0)),
            pl.BlockSpec((1, 1, 1, mod.shape[-1]), lambda bi, j, n: (bi, j, 0, 0)),
            pl.BlockSpec((1, d), lambda bi, j, n: (0, 0)),
            pl.BlockSpec((5, d, tn), lambda bi, j, n: (0, 0, n)),
            pl.BlockSpec((1, tn), lambda bi, j, n: (0, n)),
        ],
        out_specs=[one, one, one, two, two],
        out_shape=[s1(BF16), s1(BF16), s1(BF16), s2(F32), s2(BF16)],
        scratch_shapes=[pltpu.VMEM((TILE, d), BF16)],
        name="hgrn2_features",
        compiler_params=pltpu.CompilerParams(
            dimension_semantics=("parallel", "parallel", "arbitrary"), vmem_limit_bytes=VMEM_LIMIT),
    )(xs, mod, pre_g, w_in5, lb)


def _hg_chunk(q, k, v, lf, st, rev):
    c = CHUNK
    sub = 8
    nh = q.shape[0]
    row = lax.broadcasted_iota(jnp.int32, (1, c, 1), 1)
    ri = lax.broadcasted_iota(jnp.int32, (1, c, c), 1)
    ci = lax.broadcasted_iota(jnp.int32, (1, c, c), 2)
    p = lf
    att = jnp.zeros((nh, c, c), F32)
    m = 1
    while m < c:
        in_hi = ((row // m) % 2) == 1
        late = jnp.logical_not(in_hi) if rev else in_hi
        if m >= sub:
            pieces = []
            for blk in range(c // (2 * m)):
                bidx = blk * 2 * m + (m if rev else m - 1)
                pieces.append(jnp.broadcast_to(p[:, bidx:bidx + 1, :], (nh, 2 * m, LANES)))
            tt = pieces[0] if len(pieces) == 1 else jnp.concatenate(pieces, axis=1)
        else:
            p4 = p.reshape(nh, c // sub, sub, LANES)
            srow = lax.broadcasted_iota(jnp.int32, (1, 1, sub, 1), 2)
            tt4 = None
            for blk in range(sub // (2 * m)):
                bidx = blk * 2 * m + (m if rev else m - 1)
                cand = jnp.broadcast_to(p4[:, :, bidx:bidx + 1, :], p4.shape)
                tt4 = cand if tt4 is None else jnp.where(srow // (2 * m) == blk, cand, tt4)
            tt = tt4.reshape(nh, c, LANES)
        ee = jnp.exp(jnp.where(late, p, tt - p))
        qt = jnp.where(late, q * ee, 0.0).astype(BF16)
        kt = jnp.where(late, 0.0, k * ee).astype(BF16)
        same = (ri // (2 * m)) == (ci // (2 * m))
        att = att + jnp.where(same, _bmm_nt(qt, kt), 0.0)
        p = jnp.where(late, p + tt, p)
        m *= 2
    tot = p[:, 0:1] if rev else p[:, c - 1:c]
    vb = v.astype(BF16)
    y = (_bmm(att.astype(BF16), vb) + jnp.sum(q * k, axis=2, keepdims=True) * v
         + _bmm_nt((q * jnp.exp(p)).astype(BF16), st.astype(BF16)))
    st_new = st * jnp.exp(tot) + _bmm_tn(vb, (k * jnp.exp(tot - p)).astype(BF16))
    return y, st_new


def _hg_scan_kernel(*refs):
    ins = refs[:8]
    y_os = refs[8:10]
    st = refs[10]

    @pl.when(pl.program_id(1) == 0)
    def _():
        st[...] = jnp.zeros_like(st)

    loaded = []
    for dr in range(2):
        q_r, v_r, kd_r, lf_r = ins[4 * dr:4 * dr + 4]
        loaded.append((q_r[0].astype(F32), kd_r[0, 0].astype(F32), v_r[0].astype(F32), lf_r[0, 0], st[dr]))
    results = [_hg_chunk(*loaded[dr], rev=(dr == 1)) for dr in range(2)]
    for dr in range(2):
        y_os[dr][0] = results[dr][0]
        st[dr] = results[dr][1]


def _hg_scan(q, v, lf, kd, *, n_lat_chunks, n_ctx_chunks):
    b, ng, tp, _ = q.shape
    total, in_specs, out_specs, out_shape = _scan_specs(b, ng, tp, n_lat_chunks, n_ctx_chunks, 2, 2)
    return pl.pallas_call(
        _hg_scan_kernel,
        grid=(b, total),
        in_specs=in_specs,
        out_specs=out_specs,
        out_shape=out_shape,
        scratch_shapes=[pltpu.VMEM((2, ng, LANES, LANES), F32)],
        name="hgrn2_scan",
        compiler_params=pltpu.CompilerParams(
            dimension_semantics=("parallel", "arbitrary"), vmem_limit_bytes=VMEM_LIMIT),
    )(q, v, kd, lf, q, v, kd, lf)


def _pad_rows(w, rows):
    return jnp.pad(w, ((0, rows - w.shape[0]), (0, 0)))


def _pad_cols(w, cols):
    return jnp.pad(w, ((0, 0), (0, cols - w.shape[1])))


def kernel(x, c, ctx, c_ctx, mod_w, mod_b, pre_g, post_g, rw_mix, rw_proj, rw_wo, rw_w0, rw_w1, rw_w2, rw_a0,
           rw_a1, rw_a2, rw_v0, rw_v1, rw_v2, rw_kk, rw_ka, rw_rk, rw_lnw, rw_lnb, hg_win, hg_wo, hg_gn, hg_lb):
    b, seq, d = x.shape
    ctx_len = ctx.shape[1]
    depth = mod_w.shape[0]
    di = rw_proj.shape[-1]
    assert seq % TILE == 0 and ctx_len % CHUNK == 0 and ctx_len <= TILE and seq % GRID_W == 0
    n_lat_tiles = seq // TILE
    tp = seq + TILE
    n_lat_chunks, n_ctx_chunks = seq // CHUNK, ctx_len // CHUNK

    xs = jnp.concatenate([x, ctx, jnp.zeros((b, tp - seq - ctx_len, d), x.dtype)], axis=1)

    rows = ((b + 1 + 7) // 8) * 8
    cond = jnp.concatenate([c, c_ctx[None], jnp.zeros((rows - b - 1, d), c.dtype)], axis=0)
    mod_all = _modulation(cond, mod_w, mod_b)
    mod_lat = jnp.broadcast_to(mod_all[:, :b, None, :], (depth, b, n_lat_tiles, 3 * d))
    mod_ctx = jnp.broadcast_to(mod_all[:, b:b + 1, None, :], (depth, b, 1, 3 * d))
    mod_t = jnp.concatenate([mod_lat, mod_ctx], axis=2)[:, :, :, None, :]

    p_lb = jax.nn.softmax(hg_lb.astype(F32), axis=0)
    lb_all = jnp.cumsum(p_lb, axis=0) - p_lb[0]

    vf = None
    for i in range(depth):
        ctx_out = i < depth - 1
        n_out_tiles = (seq + (ctx_len if ctx_out else 0)) // OTILE
        j = i // 2
        mod_i = mod_t[i]
        pg = pre_g[i][None]
        if i % 2 == 0:
            r64 = rw_w1.shape[-1]
            w2 = jnp.stack([jnp.pad(rw_w2[j, 0], ((0, LANES - r64), (0, 0))),
                            jnp.pad(rw_w2[j, 1], ((r64, LANES - 2 * r64), (0, 0)))])
            a2 = jnp.stack([jnp.pad(rw_a2[j, 0], ((0, LANES - r64), (0, 0))),
                            jnp.pad(rw_a2[j, 1], ((r64, LANES - 2 * r64), (0, 0)))])
            if j == 0:
                v0 = jnp.zeros((di,), F32)
                v1 = jnp.zeros((d, LANES), BF16)
                v2 = jnp.zeros((LANES, di), BF16)
            else:
                v0 = rw_v0[j - 1]
                v1 = _pad_cols(rw_v1[j - 1], LANES).astype(BF16)
                v2 = _pad_rows(rw_v2[j - 1], LANES).astype(BF16)
            p = {
                "mix": rw_mix[j],
                "proj": rw_proj[j].astype(BF16),
                "w1": jnp.concatenate([rw_w1[j, 0], rw_w1[j, 1]], axis=1).astype(BF16),
                "a1": jnp.concatenate([rw_a1[j, 0], rw_a1[j, 1]], axis=1).astype(BF16),
                "v1": v1,
                "w2": w2.astype(BF16),
                "a2": a2.astype(BF16),
                "v2": v2,
                "vec": jnp.stack([rw_w0[j, 0], rw_w0[j, 1], rw_a0[j, 0], rw_a0[j, 1], v0, rw_kk[j], rw_ka[j],
                                  rw_rk[j]]),
            }
            r, v, kk, z, bv, lw, bb, kd = _rw_features(xs, mod_i, pg, p, vf if j > 0 else None,
                                                       n_lat_tiles=n_lat_tiles, ctx_len=ctx_len)
            if j == 0:
                vf = v
            yf, yb = _rw_scan((r, v, kk, lw, bb, kd), n_lat_chunks=n_lat_chunks, n_ctx_chunks=n_ctx_chunks)
            gspec = _out_group_spec(di)
            extras = [z, bv, rw_lnw[j][None], rw_lnb[j][None], _head_sum_matrix(LANES, RW_HEAD)]
            especs = [gspec, gspec, pl.BlockSpec((1, di), lambda bi, jj: (0, 0)),
                      pl.BlockSpec((1, di), lambda bi, jj: (0, 0)), pl.BlockSpec((LANES, LANES), lambda bi, jj: (0, 0))]
            xs = _mixer_out("rwkv7_out", _rw_out_kernel, (yf, yb), extras, especs, rw_wo[j].astype(BF16), xs, mod_i,
                            post_g[i][None], n_out_tiles)
        else:
            w5 = hg_win[j].reshape(d, 5, di).transpose(1, 0, 2).astype(BF16)
            q, iv, sg, lf, kd = _hg_features(xs, mod_i, pg, w5, lb_all[i][None])
            yf, yb = _hg_scan(q, iv, lf, kd, n_lat_chunks=n_lat_chunks, n_ctx_chunks=n_ctx_chunks)
            extras = [sg, hg_gn[j][None]]
            especs = [_out_group_spec(di), pl.BlockSpec((1, HG_HEAD), lambda bi, jj: (0, 0))]
            xs = _mixer_out("hgrn2_out", _hg_out_kernel, (yf, yb), extras, especs, hg_wo[j].astype(BF16), xs, mod_i,
                            post_g[i][None], n_out_tiles)
    return xs[:, :seq]
```

```python
import functools
import math

import jax
import jax.numpy as jnp
from jax import lax
from jax.experimental import pallas as pl
from jax.experimental.pallas import tpu as pltpu

F32 = jnp.float32
BF16 = jnp.bfloat16
HI = lax.Precision.HIGHEST

GRID_W = 64
RW_HEAD = 64
HG_HEAD = 128
CHUNK = 64
TILE = 512
OTILE = 256
RW_GROUPS_PER_ITER = 16
HG_GROUPS_PER_ITER = 16
NORM_EPS = 1e-6
LN_X_EPS = 64e-5
LANES = 128
VMEM_LIMIT = 56 * 1024 * 1024


def _silu(x):
    return x * jax.nn.sigmoid(x)


def _dot(a, b):
    return jnp.dot(a, b, preferred_element_type=F32)


def _mod_kernel(s_ref, w_ref, b_ref, o_ref):
    s = _silu(s_ref[...])
    o_ref[0] = jnp.dot(s, w_ref[0], precision=HI, preferred_element_type=F32) + b_ref[0]


def _modulation(cond, mod_w, mod_b):
    depth, d, d3 = mod_w.shape
    rows = cond.shape[0]
    nb = d3 // d
    return pl.pallas_call(
        _mod_kernel,
        grid=(depth, nb),
        in_specs=[
            pl.BlockSpec((rows, d), lambda i, n: (0, 0)),
            pl.BlockSpec((1, d, d), lambda i, n: (i, 0, n)),
            pl.BlockSpec((1, 1, d), lambda i, n: (i, 0, n)),
        ],
        out_specs=pl.BlockSpec((1, rows, d), lambda i, n: (i, 0, n)),
        out_shape=jax.ShapeDtypeStruct((depth, rows, d3), F32),
        name="adaln_modulation",
        compiler_params=pltpu.CompilerParams(vmem_limit_bytes=VMEM_LIMIT),
    )(cond, mod_w, mod_b.reshape(depth, 1, d3))


def _adaln(xb, g, scale, shift):
    ms = jnp.mean(xb * xb, axis=-1, keepdims=True)
    return xb * lax.rsqrt(ms + NORM_EPS) * g * (1.0 + scale) + shift


def _store_groups(o_ref, lead, val):
    for gq in range(val.shape[1] // LANES):
        o_ref[lead + (gq,)] = val[:, gq * LANES:(gq + 1) * LANES].astype(o_ref.dtype)


def _load_groups(ref, lead):
    n = ref.shape[len(lead)]
    return jnp.concatenate([ref[lead + (gq,)] for gq in range(n)], axis=1)


def _head_sum_matrix(width, head):
    r = lax.broadcasted_iota(jnp.int32, (width, width), 0) // head
    c = lax.broadcasted_iota(jnp.int32, (width, width), 1) // head
    return (r == c).astype(BF16)


def _rw_feat_kernel(n_lat_tiles, ctx_len, has_vres, *refs):
    (x_ref, xp_ref, xn_ref, mod_ref, g_ref, mix_ref, w1_ref, a1_ref, v1_ref,
     proj_ref, w2_ref, a2_ref, v2_ref, vec_ref, hs_ref) = refs[:15]
    pos = 15
    vf_ref = None
    if has_vres:
        vf_ref = refs[pos]
        pos += 1
    (r_o, v_o, kk_o, z_o, bv_o, lw_o, bb_o, kd_o) = refs[pos:pos + 8]
    hbuf, xm, tw, ta, tv = refs[pos + 8:]

    j = pl.program_id(1)
    n = pl.program_id(2)
    d = x_ref.shape[-1]
    q4 = d // 4
    nblk = TILE // GRID_W

    @pl.when(n == 0)
    def _prologue():
        m = mod_ref[0, 0]
        shift, scale = m[:, :d], m[:, d:2 * d]
        g = g_ref[...]
        is_lat = j < n_lat_tiles
        prev_ok = jnp.logical_and(j >= 1, is_lat)
        next_ok = j < n_lat_tiles - 1

        def norm_rows(rb, c):
            r0 = pl.multiple_of(rb * GRID_W, GRID_W)
            hbuf[pl.ds(r0 + GRID_W, GRID_W), :] = _adaln(x_ref[0, pl.ds(r0, GRID_W), :], g, scale, shift)
            return c

        lax.fori_loop(0, nblk, norm_rows, 0)
        hp = _adaln(xp_ref[0], g, scale, shift)
        hbuf[0:GRID_W, :] = jnp.where(prev_ok, hp, 0.0)
        hn = _adaln(xn_ref[0], g, scale, shift)
        hbuf[TILE + GRID_W:TILE + 2 * GRID_W, :] = jnp.where(next_ok, hn, 0.0)

        row = lax.broadcasted_iota(jnp.int32, (GRID_W, 1), 0)

        def store_mixed(r0, cs, h_c, hs_c):
            dd = hs_c - h_c
            for k in range(6):
                xm[k, pl.ds(r0, GRID_W), cs] = (h_c + dd * mix_ref[k:k + 1, cs]).astype(BF16)

        def mix_lat(rb, c):
            r0 = pl.multiple_of(rb * GRID_W, GRID_W)
            for qi in range(4):
                cs = slice(qi * q4, (qi + 1) * q4)
                h_c = hbuf[pl.ds(r0 + GRID_W, GRID_W), cs]
                if qi == 0:
                    hs_c = jnp.where(row == 0, 0.0, pltpu.roll(h_c, 1, axis=0))
                elif qi == 1:
                    hs_c = jnp.where(row == GRID_W - 1, 0.0, pltpu.roll(h_c, GRID_W - 1, axis=0))
                elif qi == 2:
                    hs_c = hbuf[pl.ds(r0, GRID_W), cs]
                else:
                    hs_c = hbuf[pl.ds(r0 + 2 * GRID_W, GRID_W), cs]
                store_mixed(r0, cs, h_c, hs_c)
            return c

        def mix_ctx(rb, c):
            r0 = pl.multiple_of(rb * GRID_W, GRID_W)
            t = row + rb * GRID_W
            for qi in range(4):
                cs = slice(qi * q4, (qi + 1) * q4)
                h_c = hbuf[pl.ds(r0 + GRID_W, GRID_W), cs]
                if qi < 2:
                    edge = hbuf[pl.ds(r0 + GRID_W - 1, 1), cs]
                    hs_c = jnp.where(row == 0, edge, pltpu.roll(h_c, 1, axis=0))
                else:
                    edge = hbuf[pl.ds(r0 + 2 * GRID_W, 1), cs]
                    hs_c = jnp.where(row == GRID_W - 1, edge, pltpu.roll(h_c, GRID_W - 1, axis=0))
                    hs_c = jnp.where(t == ctx_len - 1, 0.0, hs_c)
                store_mixed(r0, cs, h_c, hs_c)
            return c

        @pl.when(is_lat)
        def _():
            lax.fori_loop(0, nblk, mix_lat, 0)

        @pl.when(jnp.logical_not(is_lat))
        def _():
            lax.fori_loop(0, nblk, mix_ctx, 0)

        tw[...] = jnp.tanh(_dot(xm[1], w1_ref[...])).astype(BF16)
        ta[...] = _dot(xm[4], a1_ref[...]).astype(BF16)
        if has_vres:
            tv[...] = _dot(xm[3], v1_ref[...]).astype(BF16)

    vec = vec_ref[...]
    r = _dot(xm[0], proj_ref[0])
    k = _dot(xm[2], proj_ref[1])
    v = _dot(xm[3], proj_ref[2])
    z = _silu(_dot(xm[5], proj_ref[3]))
    if has_vres:
        gate = jax.nn.sigmoid(vec[4:5] + _dot(tv[...], v2_ref[...]))
        v = v + (_load_groups(vf_ref, (0,)).astype(F32) - v) * gate
    hs = hs_ref[...]
    kk = k * vec[5:6]
    ss = _dot((kk * kk).astype(BF16), hs)
    kk = kk * lax.rsqrt(jnp.maximum(ss, 1e-24))
    decay_scale = -math.exp(-0.5) * math.log2(math.e)
    ksum = None
    for dr in range(2):
        u = vec[dr:dr + 1] + _dot(tw[...], w2_ref[dr])
        _store_groups(lw_o, (dr, 0), decay_scale * jax.nn.sigmoid(u))
        a = jax.nn.sigmoid(vec[2 + dr:3 + dr] + _dot(ta[...], a2_ref[dr]))
        kd = k * (1.0 + (a - 1.0) * vec[6:7])
        _store_groups(bb_o, (dr, 0), kk * a)
        _store_groups(kd_o, (dr, 0), kd)
        ksum = kd if ksum is None else ksum + kd
    bonus = _dot((r * ksum * vec[7:8]).astype(BF16), hs)
    _store_groups(r_o, (0,), r)
    _store_groups(v_o, (0,), v)
    _store_groups(kk_o, (0,), kk)
    _store_groups(z_o, (0,), z)
    _store_groups(bv_o, (0,), bonus * v)


def _group_major_specs(b, tp, di, tn):
    gq = tn // LANES
    one = pl.BlockSpec((1, gq, TILE, LANES), lambda bi, j, n: (bi, n, j, 0))
    two = pl.BlockSpec((2, 1, gq, TILE, LANES), lambda bi, j, n: (0, bi, n, j, 0))
    s1 = lambda dt: jax.ShapeDtypeStruct((b, di // LANES, tp, LANES), dt)
    s2 = lambda dt: jax.ShapeDtypeStruct((2, b, di // LANES, tp, LANES), dt)
    return one, two, s1, s2


def _rw_features(xs, mod, pre_g, p, vf, *, n_lat_tiles, ctx_len, tn=256):
    b, tp, d = xs.shape
    di = p["proj"].shape[-1]
    nt = di // tn
    ntiles = tp // TILE
    nb64 = tp // GRID_W
    per = TILE // GRID_W
    has_vres = vf is not None

    def tok(bi, j, n):
        return (bi, j, 0)

    in_specs = [
        pl.BlockSpec((1, TILE, d), tok),
        pl.BlockSpec((1, GRID_W, d), lambda bi, j, n: (bi, jnp.maximum(j * per - 1, 0), 0)),
        pl.BlockSpec((1, GRID_W, d), lambda bi, j, n: (bi, jnp.minimum((j + 1) * per, nb64 - 1), 0)),
        pl.BlockSpec((1, 1, 1, mod.shape[-1]), lambda bi, j, n: (bi, j, 0, 0)),
        pl.BlockSpec((1, d), lambda bi, j, n: (0, 0)),
        pl.BlockSpec((6, d), lambda bi, j, n: (0, 0)),
        pl.BlockSpec((d, LANES), lambda bi, j, n: (0, 0)),
        pl.BlockSpec((d, LANES), lambda bi, j, n: (0, 0)),
        pl.BlockSpec((d, LANES), lambda bi, j, n: (0, 0)),
        pl.BlockSpec((4, d, tn), lambda bi, j, n: (0, 0, n)),
        pl.BlockSpec((2, LANES, tn), lambda bi, j, n: (0, 0, n)),
        pl.BlockSpec((2, LANES, tn), lambda bi, j, n: (0, 0, n)),
        pl.BlockSpec((LANES, tn), lambda bi, j, n: (0, n)),
        pl.BlockSpec((8, tn), lambda bi, j, n: (0, n)),
        pl.BlockSpec((tn, tn), lambda bi, j, n: (0, 0)),
    ]
    args = [xs, xs, xs, mod, pre_g, p["mix"], p["w1"], p["a1"], p["v1"], p["proj"], p["w2"], p["a2"],
            p["v2"], p["vec"], _head_sum_matrix(tn, RW_HEAD)]
    one, two, s1, s2 = _group_major_specs(b, tp, di, tn)
    if has_vres:
        in_specs.append(one)
        args.append(vf)
    return pl.pallas_call(
        functools.partial(_rw_feat_kernel, n_lat_tiles, ctx_len, has_vres),
        grid=(b, ntiles, nt),
        in_specs=in_specs,
        out_specs=[one, one, one, one, one, two, two, two],
        out_shape=[s1(BF16), s1(BF16), s1(BF16), s1(BF16), s1(BF16), s2(F32), s2(BF16), s2(BF16)],
        scratch_shapes=[
            pltpu.VMEM((TILE + 2 * GRID_W, d), F32),
            pltpu.VMEM((6, TILE, d), BF16),
            pltpu.VMEM((TILE, LANES), BF16),
            pltpu.VMEM((TILE, LANES), BF16),
            pltpu.VMEM((TILE, LANES), BF16),
        ],
        name="rwkv7_features",
        compiler_params=pltpu.CompilerParams(
            dimension_semantics=("parallel", "parallel", "arbitrary"), vmem_limit_bytes=VMEM_LIMIT),
    )(*args)


def _bmm(a, b):
    return lax.dot_general(a, b, (((2,), (1,)), ((0,), (0,))), preferred_element_type=F32)


def _bmm_nt(a, b):
    return lax.dot_general(a, b, (((2,), (2,)), ((0,), (0,))), preferred_element_type=F32)


def _bmm_tn(a, b):
    return lax.dot_general(a, b, (((1,), (1,)), ((0,), (0,))), preferred_element_type=F32)


def _chunk_cumsum(x, rev):
    c = x.shape[1]
    row = lax.broadcasted_iota(jnp.int32, (1, c, 1), 1)
    s = 1
    while s < c:
        if rev:
            x = x + jnp.where(row < c - s, pltpu.roll(x, c - s, axis=1), 0.0)
        else:
            x = x + jnp.where(row >= s, pltpu.roll(x, s, axis=1), 0.0)
        s *= 2
    return x


def _scan_chunk_maps(n_lat_chunks, n_ctx_chunks):
    total = n_lat_chunks + n_ctx_chunks

    def fwd(s):
        return (s + n_lat_chunks) % total

    def bwd(s):
        return total - 1 - s

    return total, fwd, bwd


def _scan_specs(b, ng, tp, n_lat_chunks, n_ctx_chunks, n_one, n_two):
    total, fwd, bwd = _scan_chunk_maps(n_lat_chunks, n_ctx_chunks)
    in_specs = []
    for dr, cm in enumerate((fwd, bwd)):
        one = pl.BlockSpec((1, ng, CHUNK, LANES), lambda bi, s, cm=cm: (bi, 0, cm(s), 0))
        two = pl.BlockSpec((1, 1, ng, CHUNK, LANES), lambda bi, s, cm=cm, dr=dr: (dr, bi, 0, cm(s), 0))
        in_specs += [one] * n_one + [two] * n_two
    out_specs = [pl.BlockSpec((1, ng, CHUNK, LANES), lambda bi, s, cm=cm: (bi, 0, cm(s), 0)) for cm in (fwd, bwd)]
    out_shape = [jax.ShapeDtypeStruct((b, ng, tp, LANES), F32)] * 2
    return total, in_specs, out_specs, out_shape


def _rw_chunk(r, v, kk, bb, kd, lw, st, rev):
    c = CHUNK
    half = RW_HEAD
    npair = r.shape[0]
    ri = lax.broadcasted_iota(jnp.int32, (1, c, c), 1)
    ci = lax.broadcasted_iota(jnp.int32, (1, c, c), 2)
    if rev:
        incl, strict = ci >= ri, ci > ri
    else:
        incl, strict = ci <= ri, ci < ri
    cum = _chunk_cumsum(lw, rev)
    tot = cum[:, 0:1] if rev else cum[:, c - 1:c]
    e_in = jnp.exp2(cum)
    e_prev = jnp.exp2(cum - lw)
    e_inv = jnp.exp2(-cum)
    e_out = jnp.exp2(tot - cum)
    rt = (r * e_in).astype(BF16)
    at = (-kk * e_prev).astype(BF16)
    bi = (bb * e_inv).astype(BF16)
    ki = (kd * e_inv).astype(BF16)
    bh = (bb * e_out).astype(BF16)
    kh = (kd * e_out).astype(BF16)
    vb = v.astype(BF16)
    sb = st.astype(BF16)
    lane = lax.broadcasted_iota(jnp.int32, (1, 1, LANES), 2)
    eye = (ri == ci).astype(F32)

    h0 = _bmm_nt(jnp.concatenate([at, rt], axis=1), sb)
    ah0, o_full = h0[:, :c], h0[:, c:]
    u_full = jnp.zeros((npair, c, LANES), F32)
    zero = jnp.zeros_like(at)
    for hx in range(2):
        mx = (lane // half) == hx
        lhs = jnp.concatenate([jnp.where(mx, at, zero), jnp.where(mx, rt, zero)], axis=1)
        m_b = _bmm_nt(lhs, bi)
        m_k = _bmm_nt(lhs, ki)
        a_ab = jnp.where(strict, m_b[:, :c], 0.0)
        a_ak = jnp.where(strict, m_k[:, :c], 0.0)
        a_rb = jnp.where(incl, m_b[:, c:], 0.0)
        a_rk = jnp.where(incl, m_k[:, c:], 0.0)
        pm = eye + jnp.where(ri // 2 == ci // 2, a_ab, 0.0)
        m = 2
        while m < c:
            off = jnp.logical_and(ri // (2 * m) == ci // (2 * m), ri // m != ci // m)
            a_off = jnp.where(off, a_ab, 0.0).astype(BF16)
            pmb = pm.astype(BF16)
            pm = pm + _bmm(_bmm(pmb, a_off).astype(BF16), pmb)
            m *= 2
        av = _bmm(jnp.concatenate([a_ak, a_rk], axis=1).astype(BF16), vb)
        x_x = ah0 + av[:, :c]
        u_x = _bmm(pm.astype(BF16), x_x.astype(BF16))
        u_full = jnp.where(mx, u_x, u_full)
        o_x = _bmm(a_rb.astype(BF16), u_x.astype(BF16)) + av[:, c:]
        o_full = o_full + jnp.where(mx, o_x, 0.0)
    upd = _bmm_tn(jnp.concatenate([u_full.astype(BF16), vb], axis=1),
                  jnp.concatenate([bh, kh], axis=1))
    blk_r = lax.broadcasted_iota(jnp.int32, (1, LANES, LANES), 1) // half
    blk_c = lax.broadcasted_iota(jnp.int32, (1, LANES, LANES), 2) // half
    st_new = st * jnp.exp2(tot) + jnp.where(blk_r == blk_c, upd, 0.0)
    return o_full, st_new


def _rw_scan_kernel(*refs):
    ins = refs[:12]
    y_os = refs[12:14]
    st = refs[14]

    @pl.when(pl.program_id(1) == 0)
    def _():
        st[...] = jnp.zeros_like(st)

    def body(gi, carry):
        gs = pl.ds(gi * RW_GROUPS_PER_ITER, RW_GROUPS_PER_ITER)
        loaded = []
        for dr in range(2):
            r_r, v_r, kk_r, bb_r, kd_r, lw_r = ins[6 * dr:6 * dr + 6]
            loaded.append((r_r[0, gs].astype(F32), v_r[0, gs].astype(F32), kk_r[0, gs].astype(F32),
                           bb_r[0, 0, gs].astype(F32), kd_r[0, 0, gs].astype(F32), lw_r[0, 0, gs], st[dr, gs]))
        results = [_rw_chunk(*loaded[dr], rev=(dr == 1)) for dr in range(2)]
        for dr in range(2):
            y_os[dr][0, gs] = results[dr][0]
            st[dr, gs] = results[dr][1]
        return carry

    lax.fori_loop(0, st.shape[1] // RW_GROUPS_PER_ITER, body, 0)


def _rw_scan(feats, *, n_lat_chunks, n_ctx_chunks):
    r, v, kk, lw, bb, kd = feats
    b, ng, tp, _ = r.shape
    total, in_specs, out_specs, out_shape = _scan_specs(b, ng, tp, n_lat_chunks, n_ctx_chunks, 3, 3)
    return pl.pallas_call(
        _rw_scan_kernel,
        grid=(b, total),
        in_specs=in_specs,
        out_specs=out_specs,
        out_shape=out_shape,
        scratch_shapes=[pltpu.VMEM((2, ng, LANES, LANES), F32)],
        name="rwkv7_scan",
        compiler_params=pltpu.CompilerParams(
            dimension_semantics=("parallel", "arbitrary"), vmem_limit_bytes=VMEM_LIMIT),
    )(r, v, kk, bb, kd, lw, r, v, kk, bb, kd, lw)


def _residual_tail(yg_bf16, wo_ref, x_ref, mod_ref, pg_ref, o_ref):
    d = x_ref.shape[-1]
    o = _dot(yg_bf16, wo_ref[...])
    ms = jnp.mean(o * o, axis=-1, keepdims=True)
    o = o * lax.rsqrt(ms + NORM_EPS) * pg_ref[...]
    gate = mod_ref[0, 0][:, 2 * d:]
    o_ref[0] = x_ref[0] + gate * o


def _rw_out_kernel(yf_ref, yb_ref, z_ref, bv_ref, lnw_ref, lnb_ref, hs_ref, wo_ref, x_ref, mod_ref, pg_ref,
                   o_ref, yg):
    inv = 1.0 / RW_HEAD
    hs = hs_ref[...]
    for gq in range(yf_ref.shape[1]):
        cs = slice(gq * LANES, (gq + 1) * LANES)
        y = yf_ref[0, gq] + yb_ref[0, gq]
        mu = _dot(y.astype(BF16), hs) * inv
        yc = y - mu
        var = _dot((yc * yc).astype(BF16), hs) * inv
        yn = yc * lax.rsqrt(var + LN_X_EPS) * lnw_ref[:, cs] + lnb_ref[:, cs] + bv_ref[0, gq].astype(F32)
        yg[:, cs] = yn.astype(BF16) * z_ref[0, gq]
    _residual_tail(yg[...], wo_ref, x_ref, mod_ref, pg_ref, o_ref)


def _hg_out_kernel(yf_ref, yb_ref, sg_ref, gn_ref, wo_ref, x_ref, mod_ref, pg_ref, o_ref, yg):
    for hx in range(yf_ref.shape[1]):
        cs = slice(hx * LANES, (hx + 1) * LANES)
        y = yf_ref[0, hx] + yb_ref[0, hx]
        ms = jnp.mean(y * y, axis=-1, keepdims=True)
        yn = y * lax.rsqrt(ms + NORM_EPS) * gn_ref[...]
        yg[:, cs] = yn.astype(BF16) * sg_ref[0, hx]
    _residual_tail(yg[...], wo_ref, x_ref, mod_ref, pg_ref, o_ref)


def _out_group_spec(di):
    return pl.BlockSpec((1, di // LANES, OTILE, LANES), lambda bi, j: (bi, 0, j, 0))


def _mixer_out(name, kernel_fn, ys, extras, extra_specs, wo, xs, mod, post_g, n_tiles):
    b, tp, d = xs.shape
    di = wo.shape[0]
    per = TILE // OTILE
    tokd = pl.BlockSpec((1, OTILE, d), lambda bi, j: (bi, j, 0))
    in_specs = [_out_group_spec(di)] * 2 + extra_specs + [
        pl.BlockSpec((di, d), lambda bi, j: (0, 0)),
        tokd,
        pl.BlockSpec((1, 1, 1, mod.shape[-1]), lambda bi, j: (bi, j // per, 0, 0)),
        pl.BlockSpec((1, d), lambda bi, j: (0, 0)),
    ]
    n_in = len(in_specs)
    return pl.pallas_call(
        kernel_fn,
        grid=(b, n_tiles),
        in_specs=in_specs,
        out_specs=tokd,
        out_shape=jax.ShapeDtypeStruct(xs.shape, F32),
        scratch_shapes=[pltpu.VMEM((OTILE, di), BF16)],
        input_output_aliases={n_in - 3: 0},
        name=name,
        compiler_params=pltpu.CompilerParams(
            dimension_semantics=("parallel", "parallel"), vmem_limit_bytes=VMEM_LIMIT),
    )(*ys, *extras, wo, xs, mod, post_g)


def _hg_feat_kernel(x_ref, mod_ref, g_ref, w_ref, lb_ref, q_o, i_o, sg_o, lf_o, kd_o, hbf):
    n = pl.program_id(2)
    d = x_ref.shape[-1]

    @pl.when(n == 0)
    def _prologue():
        m = mod_ref[0, 0]
        shift, scale = m[:, :d], m[:, d:2 * d]
        g = g_ref[...]

        def norm_rows(rb, c):
            r0 = pl.multiple_of(rb * GRID_W, GRID_W)
            hbf[pl.ds(r0, GRID_W), :] = _adaln(x_ref[0, pl.ds(r0, GRID_W), :], g, scale, shift).astype(BF16)
            return c

        lax.fori_loop(0, TILE // GRID_W, norm_rows, 0)

    h = hbf[...]
    lb = lb_ref[...]
    _store_groups(q_o, (0,), _silu(_dot(h, w_ref[0])))
    for dr in range(2):
        f = lb + (1.0 - lb) * jax.nn.sigmoid(_dot(h, w_ref[1 + dr]))
        _store_groups(kd_o, (dr, 0), 1.0 - f)
        _store_groups(lf_o, (dr, 0), jnp.log2(f))
    _store_groups(i_o, (0,), _dot(h, w_ref[3]))
    _store_groups(sg_o, (0,), _silu(_dot(h, w_ref[4])))


def _hg_features(xs, mod, pre_g, w_in5, lb, *, tn=256):
    b, tp, d = xs.shape
    di = w_in5.shape[-1]
    nt = di // tn
    one, two, s1, s2 = _group_major_specs(b, tp, di, tn)
    return pl.pallas_call(
        _hg_feat_kernel,
        grid=(b, tp // TILE, nt),
        in_specs=[
            pl.BlockSpec((1, TILE, d), lambda bi, j, n: (bi, j, 0)),
            pl.BlockSpec((1, 1, 1, mod.shape[-1]), lambda bi, j, n: (bi, j, 0, 0)),
            pl.BlockSpec((1, d), lambda bi, j, n: (0, 0)),
            pl.BlockSpec((5, d, tn), lambda bi, j, n: (0, 0, n)),
            pl.BlockSpec((1, tn), lambda bi, j, n: (0, n)),
        ],
        out_specs=[one, one, one, two, two],
        out_shape=[s1(BF16), s1(BF16), s1(BF16), s2(F32), s2(BF16)],
        scratch_shapes=[pltpu.VMEM((TILE, d), BF16)],
        name="hgrn2_features",
        compiler_params=pltpu.CompilerParams(
            dimension_semantics=("parallel", "parallel", "arbitrary"), vmem_limit_bytes=VMEM_LIMIT),
    )(xs, mod, pre_g, w_in5, lb)


def _hg_chunk(q, k, v, lf2, st, rev):
    c = CHUNK
    sub = 8
    nh = q.shape[0]
    row = lax.broadcasted_iota(jnp.int32, (1, c, 1), 1)
    ri = lax.broadcasted_iota(jnp.int32, (1, c, c), 1)
    ci = lax.broadcasted_iota(jnp.int32, (1, c, c), 2)
    p = lf2
    att = jnp.zeros((nh, c, c), F32)
    m = 1
    while m < c:
        same = (ri // (2 * m)) == (ci // (2 * m))
        if m >= sub:
            zeros = jnp.zeros((nh, m, LANES), BF16)
            q_parts, k_parts, p_parts = [], [], []
            for blk in range(c // (2 * m)):
                lo = slice(blk * 2 * m, blk * 2 * m + m)
                hi = slice(blk * 2 * m + m, (blk + 1) * 2 * m)
                early, later = (hi, lo) if rev else (lo, hi)
                bidx = blk * 2 * m + (m if rev else m - 1)
                tt = p[:, bidx:bidx + 1, :]
                q_l = (q[:, later] * jnp.exp2(p[:, later])).astype(BF16)
                k_e = (k[:, early] * jnp.exp2(tt - p[:, early])).astype(BF16)
                if rev:
                    q_parts += [q_l, zeros]
                    k_parts += [zeros, k_e]
                    p_parts += [p[:, later] + tt, p[:, early]]
                else:
                    q_parts += [zeros, q_l]
                    k_parts += [k_e, zeros]
                    p_parts += [p[:, early], p[:, later] + tt]
            mm = _bmm_nt(jnp.concatenate(q_parts, axis=1), jnp.concatenate(k_parts, axis=1))
            att = att + (mm if 2 * m == c else jnp.where(same, mm, 0.0))
            p = jnp.concatenate(p_parts, axis=1)
        else:
            in_hi = ((row // m) % 2) == 1
            late = jnp.logical_not(in_hi) if rev else in_hi
            if m == 1:
                tt = jnp.where(late, pltpu.roll(p, c - 1 if rev else 1, axis=1), p)
            else:
                p4 = p.reshape(nh, c // sub, sub, LANES)
                srow = lax.broadcasted_iota(jnp.int32, (1, 1, sub, 1), 2)
                tt4 = None
                for blk in range(sub // (2 * m)):
                    bidx = blk * 2 * m + (m if rev else m - 1)
                    cand = jnp.broadcast_to(p4[:, :, bidx:bidx + 1, :], p4.shape)
                    tt4 = cand if tt4 is None else jnp.where(srow // (2 * m) == blk, cand, tt4)
                tt = tt4.reshape(nh, c, LANES)
            ee = jnp.exp2(jnp.where(late, p, tt - p))
            qt = jnp.where(late, q * ee, 0.0).astype(BF16)
            kt = jnp.where(late, 0.0, k * ee).astype(BF16)
            att = att + jnp.where(same, _bmm_nt(qt, kt), 0.0)
            p = jnp.where(late, p + tt, p)
        m *= 2
    tot = p[:, 0:1] if rev else p[:, c - 1:c]
    vb = v.astype(BF16)
    y = (_bmm(att.astype(BF16), vb) + jnp.sum(q * k, axis=2, keepdims=True) * v
         + _bmm_nt((q * jnp.exp2(p)).astype(BF16), st.astype(BF16)))
    st_new = st * jnp.exp2(tot) + _bmm_tn(vb, (k * jnp.exp2(tot - p)).astype(BF16))
    return y, st_new


def _hg_scan_kernel(*refs):
    ins = refs[:8]
    y_os = refs[8:10]
    st = refs[10]

    @pl.when(pl.program_id(1) == 0)
    def _():
        st[...] = jnp.zeros_like(st)

    def body(gi, carry):
        gs = pl.ds(gi * HG_GROUPS_PER_ITER, HG_GROUPS_PER_ITER)
        loaded = []
        for dr in range(2):
            q_r, v_r, kd_r, lf_r = ins[4 * dr:4 * dr + 4]
            loaded.append((q_r[0, gs].astype(F32), kd_r[0, 0, gs].astype(F32), v_r[0, gs].astype(F32),
                           lf_r[0, 0, gs], st[dr, gs]))
        results = [_hg_chunk(*loaded[dr], rev=(dr == 1)) for dr in range(2)]
        for dr in range(2):
            y_os[dr][0, gs] = results[dr][0]
            st[dr, gs] = results[dr][1]
        return carry

    lax.fori_loop(0, st.shape[1] // HG_GROUPS_PER_ITER, body, 0)


def _hg_scan(q, v, lf, kd, *, n_lat_chunks, n_ctx_chunks):
    b, ng, tp, _ = q.shape
    total, in_specs, out_specs, out_shape = _scan_specs(b, ng, tp, n_lat_chunks, n_ctx_chunks, 2, 2)
    return pl.pallas_call(
        _hg_scan_kernel,
        grid=(b, total),
        in_specs=in_specs,
        out_specs=out_specs,
        out_shape=out_shape,
        scratch_shapes=[pltpu.VMEM((2, ng, LANES, LANES), F32)],
        name="hgrn2_scan",
        compiler_params=pltpu.CompilerParams(
            dimension_semantics=("parallel", "arbitrary"), vmem_limit_bytes=VMEM_LIMIT),
    )(q, v, kd, lf, q, v, kd, lf)


def _pad_rows(w, rows):
    return jnp.pad(w, ((0, rows - w.shape[0]), (0, 0)))


def _pad_cols(w, cols):
    return jnp.pad(w, ((0, 0), (0, cols - w.shape[1])))


def kernel(x, c, ctx, c_ctx, mod_w, mod_b, pre_g, post_g, rw_mix, rw_proj, rw_wo, rw_w0, rw_w1, rw_w2, rw_a0,
           rw_a1, rw_a2, rw_v0, rw_v1, rw_v2, rw_kk, rw_ka, rw_rk, rw_lnw, rw_lnb, hg_win, hg_wo, hg_gn, hg_lb):
    b, seq, d = x.shape
    ctx_len = ctx.shape[1]
    depth = mod_w.shape[0]
    di = rw_proj.shape[-1]
    assert seq % TILE == 0 and ctx_len % CHUNK == 0 and ctx_len <= TILE and seq % GRID_W == 0
    n_lat_tiles = seq // TILE
    tp = seq + TILE
    n_lat_chunks, n_ctx_chunks = seq // CHUNK, ctx_len // CHUNK

    xs = jnp.concatenate([x, ctx, jnp.zeros((b, tp - seq - ctx_len, d), x.dtype)], axis=1)

    rows = ((b + 1 + 7) // 8) * 8
    cond = jnp.concatenate([c, c_ctx[None], jnp.zeros((rows - b - 1, d), c.dtype)], axis=0)
    mod_all = _modulation(cond, mod_w, mod_b)
    mod_lat = jnp.broadcast_to(mod_all[:, :b, None, :], (depth, b, n_lat_tiles, 3 * d))
    mod_ctx = jnp.broadcast_to(mod_all[:, b:b + 1, None, :], (depth, b, 1, 3 * d))
    mod_t = jnp.concatenate([mod_lat, mod_ctx], axis=2)[:, :, :, None, :]

    p_lb = jax.nn.softmax(hg_lb.astype(F32), axis=0)
    lb_all = jnp.cumsum(p_lb, axis=0) - p_lb[0]

    vf = None
    for i in range(depth):
        ctx_out = i < depth - 1
        n_out_tiles = (seq + (ctx_len if ctx_out else 0)) // OTILE
        j = i // 2
        mod_i = mod_t[i]
        pg = pre_g[i][None]
        if i % 2 == 0:
            r64 = rw_w1.shape[-1]
            w2 = jnp.stack([jnp.pad(rw_w2[j, 0], ((0, LANES - r64), (0, 0))),
                            jnp.pad(rw_w2[j, 1], ((r64, LANES - 2 * r64), (0, 0)))])
            a2 = jnp.stack([jnp.pad(rw_a2[j, 0], ((0, LANES - r64), (0, 0))),
                            jnp.pad(rw_a2[j, 1], ((r64, LANES - 2 * r64), (0, 0)))])
            if j == 0:
                v0 = jnp.zeros((di,), F32)
                v1 = jnp.zeros((d, LANES), BF16)
                v2 = jnp.zeros((LANES, di), BF16)
            else:
                v0 = rw_v0[j - 1]
                v1 = _pad_cols(rw_v1[j - 1], LANES).astype(BF16)
                v2 = _pad_rows(rw_v2[j - 1], LANES).astype(BF16)
            p = {
                "mix": rw_mix[j],
                "proj": rw_proj[j].astype(BF16),
                "w1": jnp.concatenate([rw_w1[j, 0], rw_w1[j, 1]], axis=1).astype(BF16),
                "a1": jnp.concatenate([rw_a1[j, 0], rw_a1[j, 1]], axis=1).astype(BF16),
                "v1": v1,
                "w2": w2.astype(BF16),
                "a2": a2.astype(BF16),
                "v2": v2,
                "vec": jnp.stack([rw_w0[j, 0], rw_w0[j, 1], rw_a0[j, 0], rw_a0[j, 1], v0, rw_kk[j], rw_ka[j],
                                  rw_rk[j]]),
            }
            r, v, kk, z, bv, lw, bb, kd = _rw_features(xs, mod_i, pg, p, vf if j > 0 else None,
                                                       n_lat_tiles=n_lat_tiles, ctx_len=ctx_len)
            if j == 0:
                vf = v
            yf, yb = _rw_scan((r, v, kk, lw, bb, kd), n_lat_chunks=n_lat_chunks, n_ctx_chunks=n_ctx_chunks)
            gspec = _out_group_spec(di)
            extras = [z, bv, rw_lnw[j][None], rw_lnb[j][None], _head_sum_matrix(LANES, RW_HEAD)]
            especs = [gspec, gspec, pl.BlockSpec((1, di), lambda bi, jj: (0, 0)),
                      pl.BlockSpec((1, di), lambda bi, jj: (0, 0)), pl.BlockSpec((LANES, LANES), lambda bi, jj: (0, 0))]
            xs = _mixer_out("rwkv7_out", _rw_out_kernel, (yf, yb), extras, especs, rw_wo[j].astype(BF16), xs, mod_i,
                            post_g[i][None], n_out_tiles)
        else:
            w5 = hg_win[j].reshape(d, 5, di).transpose(1, 0, 2).astype(BF16)
            q, iv, sg, lf, kd = _hg_features(xs, mod_i, pg, w5, lb_all[i][None])
            yf, yb = _hg_scan(q, iv, lf, kd, n_lat_chunks=n_lat_chunks, n_ctx_chunks=n_ctx_chunks)
            extras = [sg, hg_gn[j][None]]
            especs = [_out_group_spec(di), pl.BlockSpec((1, HG_HEAD), lambda bi, jj: (0, 0))]
            xs = _mixer_out("hgrn2_out", _hg_out_kernel, (yf, yb), extras, especs, hg_wo[j].astype(BF16), xs, mod_i,
                            post_g[i][None], n_out_tiles)
    return xs[:, :seq]
```

```python
import functools
import math

import jax
import jax.numpy as jnp
from jax import lax
from jax.experimental import pallas as pl
from jax.experimental.pallas import tpu as pltpu

F32 = jnp.float32
BF16 = jnp.bfloat16
HI = lax.Precision.HIGHEST

GRID_W = 64
RW_HEAD = 64
HG_HEAD = 128
CHUNK = 64
TILE = 512
OTILE = 256
SCAN_CHUNKS = 4
NORM_EPS = 1e-6
LN_X_EPS = 64e-5
LANES = 128
VMEM_LIMIT = 56 * 1024 * 1024


def _silu(x):
    return x * jax.nn.sigmoid(x)


def _dot(a, b):
    return jnp.dot(a, b, preferred_element_type=F32)


def _mod_kernel(s_ref, w_ref, b_ref, o_ref):
    s = _silu(s_ref[...])
    o_ref[0] = jnp.dot(s, w_ref[0], precision=HI, preferred_element_type=F32) + b_ref[0]


def _modulation(cond, mod_w, mod_b):
    depth, d, d3 = mod_w.shape
    rows = cond.shape[0]
    nb = d3 // d
    return pl.pallas_call(
        _mod_kernel,
        grid=(depth, nb),
        in_specs=[
            pl.BlockSpec((rows, d), lambda i, n: (0, 0)),
            pl.BlockSpec((1, d, d), lambda i, n: (i, 0, n)),
            pl.BlockSpec((1, 1, d), lambda i, n: (i, 0, n)),
        ],
        out_specs=pl.BlockSpec((1, rows, d), lambda i, n: (i, 0, n)),
        out_shape=jax.ShapeDtypeStruct((depth, rows, d3), F32),
        name="adaln_modulation",
        compiler_params=pltpu.CompilerParams(vmem_limit_bytes=VMEM_LIMIT),
    )(cond, mod_w, mod_b.reshape(depth, 1, d3))


def _adaln(xb, g, scale, shift):
    ms = jnp.mean(xb * xb, axis=-1, keepdims=True)
    return xb * lax.rsqrt(ms + NORM_EPS) * g * (1.0 + scale) + shift


def _store_groups(o_ref, lead, val):
    for gq in range(val.shape[1] // LANES):
        o_ref[lead + (gq,)] = val[:, gq * LANES:(gq + 1) * LANES].astype(o_ref.dtype)


def _load_groups(ref, lead):
    n = ref.shape[len(lead)]
    return jnp.concatenate([ref[lead + (gq,)] for gq in range(n)], axis=1)


def _head_sum_matrix(width, head):
    r = lax.broadcasted_iota(jnp.int32, (width, width), 0) // head
    c = lax.broadcasted_iota(jnp.int32, (width, width), 1) // head
    return (r == c).astype(BF16)


def _rw_feat_kernel(n_lat_tiles, ctx_len, has_vres, *refs):
    (x_ref, xp_ref, xn_ref, mod_ref, g_ref, mix_ref, w1_ref, a1_ref, v1_ref,
     proj_ref, w2_ref, a2_ref, v2_ref, vec_ref, hs_ref) = refs[:15]
    pos = 15
    vf_ref = None
    if has_vres:
        vf_ref = refs[pos]
        pos += 1
    (r_o, v_o, kk_o, z_o, bv_o, lw_o, bb_o, kd_o) = refs[pos:pos + 8]
    hbuf, xm, tw, ta, tv = refs[pos + 8:]

    j = pl.program_id(1)
    n = pl.program_id(2)
    d = x_ref.shape[-1]
    q4 = d // 4
    nblk = TILE // GRID_W

    @pl.when(n == 0)
    def _prologue():
        m = mod_ref[0, 0]
        shift, scale = m[:, :d], m[:, d:2 * d]
        g = g_ref[...]
        is_lat = j < n_lat_tiles
        prev_ok = jnp.logical_and(j >= 1, is_lat)
        next_ok = j < n_lat_tiles - 1

        def norm_rows(rb, c):
            r0 = pl.multiple_of(rb * GRID_W, GRID_W)
            hbuf[pl.ds(r0 + GRID_W, GRID_W), :] = _adaln(x_ref[0, pl.ds(r0, GRID_W), :], g, scale, shift)
            return c

        lax.fori_loop(0, nblk, norm_rows, 0)
        hp = _adaln(xp_ref[0], g, scale, shift)
        hbuf[0:GRID_W, :] = jnp.where(prev_ok, hp, 0.0)
        hn = _adaln(xn_ref[0], g, scale, shift)
        hbuf[TILE + GRID_W:TILE + 2 * GRID_W, :] = jnp.where(next_ok, hn, 0.0)

        row = lax.broadcasted_iota(jnp.int32, (GRID_W, 1), 0)

        def store_mixed(r0, cs, h_c, hs_c):
            dd = hs_c - h_c
            for k in range(6):
                xm[k, pl.ds(r0, GRID_W), cs] = (h_c + dd * mix_ref[k:k + 1, cs]).astype(BF16)

        def mix_lat(rb, c):
            r0 = pl.multiple_of(rb * GRID_W, GRID_W)
            for qi in range(4):
                cs = slice(qi * q4, (qi + 1) * q4)
                h_c = hbuf[pl.ds(r0 + GRID_W, GRID_W), cs]
                if qi == 0:
                    hs_c = jnp.where(row == 0, 0.0, pltpu.roll(h_c, 1, axis=0))
                elif qi == 1:
                    hs_c = jnp.where(row == GRID_W - 1, 0.0, pltpu.roll(h_c, GRID_W - 1, axis=0))
                elif qi == 2:
                    hs_c = hbuf[pl.ds(r0, GRID_W), cs]
                else:
                    hs_c = hbuf[pl.ds(r0 + 2 * GRID_W, GRID_W), cs]
                store_mixed(r0, cs, h_c, hs_c)
            return c

        def mix_ctx(rb, c):
            r0 = pl.multiple_of(rb * GRID_W, GRID_W)
            t = row + rb * GRID_W
            for qi in range(4):
                cs = slice(qi * q4, (qi + 1) * q4)
                h_c = hbuf[pl.ds(r0 + GRID_W, GRID_W), cs]
                if qi < 2:
                    edge = hbuf[pl.ds(r0 + GRID_W - 1, 1), cs]
                    hs_c = jnp.where(row == 0, edge, pltpu.roll(h_c, 1, axis=0))
                else:
                    edge = hbuf[pl.ds(r0 + 2 * GRID_W, 1), cs]
                    hs_c = jnp.where(row == GRID_W - 1, edge, pltpu.roll(h_c, GRID_W - 1, axis=0))
                    hs_c = jnp.where(t == ctx_len - 1, 0.0, hs_c)
                store_mixed(r0, cs, h_c, hs_c)
            return c

        @pl.when(is_lat)
        def _():
            lax.fori_loop(0, nblk, mix_lat, 0)

        @pl.when(jnp.logical_not(is_lat))
        def _():
            lax.fori_loop(0, nblk, mix_ctx, 0)

        tw[...] = jnp.tanh(_dot(xm[1], w1_ref[...])).astype(BF16)
        ta[...] = _dot(xm[4], a1_ref[...]).astype(BF16)
        if has_vres:
            tv[...] = _dot(xm[3], v1_ref[...]).astype(BF16)

    vec = vec_ref[...]
    r = _dot(xm[0], proj_ref[0, 0])
    k = _dot(xm[2], proj_ref[0, 1])
    v = _dot(xm[3], proj_ref[0, 2])
    z = _silu(_dot(xm[5], proj_ref[0, 3]))
    if has_vres:
        gate = jax.nn.sigmoid(vec[4:5] + _dot(tv[...], v2_ref[...]))
        v = v + (_load_groups(vf_ref, (0,)).astype(F32) - v) * gate
    hs = hs_ref[...]
    kk = k * vec[5:6]
    ss = _dot((kk * kk).astype(BF16), hs)
    kk = kk * lax.rsqrt(jnp.maximum(ss, 1e-24))
    decay_scale = -math.exp(-0.5) * math.log2(math.e)
    ksum = None
    for dr in range(2):
        u = vec[dr:dr + 1] + _dot(tw[...], w2_ref[dr])
        _store_groups(lw_o, (dr, 0), decay_scale * jax.nn.sigmoid(u))
        a = jax.nn.sigmoid(vec[2 + dr:3 + dr] + _dot(ta[...], a2_ref[dr]))
        kd = k * (1.0 + (a - 1.0) * vec[6:7])
        _store_groups(bb_o, (dr, 0), kk * a)
        _store_groups(kd_o, (dr, 0), kd)
        ksum = kd if ksum is None else ksum + kd
    bonus = _dot((r * ksum * vec[7:8]).astype(BF16), hs)
    _store_groups(r_o, (0,), r)
    _store_groups(v_o, (0,), v)
    _store_groups(kk_o, (0,), kk)
    _store_groups(z_o, (0,), z)
    _store_groups(bv_o, (0,), bonus * v)


def _group_major_specs(b, tp, di, tn):
    gq = tn // LANES
    one = pl.BlockSpec((1, gq, TILE, LANES), lambda bi, j, n: (bi, n, j, 0))
    two = pl.BlockSpec((2, 1, gq, TILE, LANES), lambda bi, j, n: (0, bi, n, j, 0))
    s1 = lambda dt: jax.ShapeDtypeStruct((b, di // LANES, tp, LANES), dt)
    s2 = lambda dt: jax.ShapeDtypeStruct((2, b, di // LANES, tp, LANES), dt)
    return one, two, s1, s2


def _column_blocks(w, tn):
    k, d, n = w.shape
    return w.reshape(k, d, n // tn, tn).transpose(2, 0, 1, 3)


def _rw_features(xs, mod, pre_g, p, vf, *, n_lat_tiles, ctx_len, tn=256):
    b, tp, d = xs.shape
    di = p["proj"].shape[-1]
    nt = di // tn
    ntiles = tp // TILE
    nb64 = tp // GRID_W
    per = TILE // GRID_W
    has_vres = vf is not None

    def tok(bi, j, n):
        return (bi, j, 0)

    in_specs = [
        pl.BlockSpec((1, TILE, d), tok),
        pl.BlockSpec((1, GRID_W, d), lambda bi, j, n: (bi, jnp.maximum(j * per - 1, 0), 0)),
        pl.BlockSpec((1, GRID_W, d), lambda bi, j, n: (bi, jnp.minimum((j + 1) * per, nb64 - 1), 0)),
        pl.BlockSpec((1, 1, 1, mod.shape[-1]), lambda bi, j, n: (bi, j, 0, 0)),
        pl.BlockSpec((1, d), lambda bi, j, n: (0, 0)),
        pl.BlockSpec((6, d), lambda bi, j, n: (0, 0)),
        pl.BlockSpec((d, LANES), lambda bi, j, n: (0, 0)),
        pl.BlockSpec((d, LANES), lambda bi, j, n: (0, 0)),
        pl.BlockSpec((d, LANES), lambda bi, j, n: (0, 0)),
        pl.BlockSpec((1, 4, d, tn), lambda bi, j, n: (n, 0, 0, 0)),
        pl.BlockSpec((2, LANES, tn), lambda bi, j, n: (0, 0, n)),
        pl.BlockSpec((2, LANES, tn), lambda bi, j, n: (0, 0, n)),
        pl.BlockSpec((LANES, tn), lambda bi, j, n: (0, n)),
        pl.BlockSpec((8, tn), lambda bi, j, n: (0, n)),
        pl.BlockSpec((tn, tn), lambda bi, j, n: (0, 0)),
    ]
    args = [xs, xs, xs, mod, pre_g, p["mix"], p["w1"], p["a1"], p["v1"], _column_blocks(p["proj"], tn), p["w2"], p["a2"],
            p["v2"], p["vec"], _head_sum_matrix(tn, RW_HEAD)]
    one, two, s1, s2 = _group_major_specs(b, tp, di, tn)
    if has_vres:
        in_specs.append(one)
        args.append(vf)
    return pl.pallas_call(
        functools.partial(_rw_feat_kernel, n_lat_tiles, ctx_len, has_vres),
        grid=(b, ntiles, nt),
        in_specs=in_specs,
        out_specs=[one, one, one, one, one, two, two, two],
        out_shape=[s1(BF16), s1(BF16), s1(BF16), s1(BF16), s1(BF16), s2(F32), s2(BF16), s2(BF16)],
        scratch_shapes=[
            pltpu.VMEM((TILE + 2 * GRID_W, d), F32),
            pltpu.VMEM((6, TILE, d), BF16),
            pltpu.VMEM((TILE, LANES), BF16),
            pltpu.VMEM((TILE, LANES), BF16),
            pltpu.VMEM((TILE, LANES), BF16),
        ],
        name="rwkv7_features",
        compiler_params=pltpu.CompilerParams(
            dimension_semantics=("parallel", "parallel", "arbitrary"), vmem_limit_bytes=VMEM_LIMIT),
    )(*args)


def _bmm(a, b):
    return lax.dot_general(a, b, (((2,), (1,)), ((0,), (0,))), preferred_element_type=F32)


def _bmm_nt(a, b):
    return lax.dot_general(a, b, (((2,), (2,)), ((0,), (0,))), preferred_element_type=F32)


def _bmm_tn(a, b):
    return lax.dot_general(a, b, (((1,), (1,)), ((0,), (0,))), preferred_element_type=F32)


def _chunk_cumsum(x, rev):
    c = x.shape[1]
    row = lax.broadcasted_iota(jnp.int32, (1, c, 1), 1)
    s = 1
    while s < c:
        if rev:
            x = x + jnp.where(row < c - s, pltpu.roll(x, c - s, axis=1), 0.0)
        else:
            x = x + jnp.where(row >= s, pltpu.roll(x, s, axis=1), 0.0)
        s *= 2
    return x


def _scan_chunk_maps(n_lat_chunks, n_ctx_chunks):
    total = n_lat_chunks + n_ctx_chunks

    def fwd(s):
        return (s + n_lat_chunks) % total

    def bwd(s):
        return total - 1 - s

    return total, fwd, bwd


def _scan_specs(b, ng, tp, n_lat_chunks, n_ctx_chunks, n_one, n_two):
    assert n_lat_chunks % SCAN_CHUNKS == 0 and n_ctx_chunks % SCAN_CHUNKS == 0
    rows = CHUNK * SCAN_CHUNKS
    total, fwd, bwd = _scan_chunk_maps(n_lat_chunks // SCAN_CHUNKS, n_ctx_chunks // SCAN_CHUNKS)
    in_specs = []
    for dr, cm in enumerate((fwd, bwd)):
        one = pl.BlockSpec((1, ng, rows, LANES), lambda bi, s, cm=cm: (bi, 0, cm(s), 0))
        two = pl.BlockSpec((1, 1, ng, rows, LANES), lambda bi, s, cm=cm, dr=dr: (dr, bi, 0, cm(s), 0))
        in_specs += [one] * n_one + [two] * n_two
    out_specs = [pl.BlockSpec((1, ng, rows, LANES), lambda bi, s, cm=cm: (bi, 0, cm(s), 0)) for cm in (fwd, bwd)]
    out_shape = [jax.ShapeDtypeStruct((b, ng, tp, LANES), F32)] * 2
    return total, in_specs, out_specs, out_shape


def _scan_rows(ci):
    fwd = pl.ds(pl.multiple_of(ci * CHUNK, CHUNK), CHUNK)
    bwd = pl.ds(pl.multiple_of((SCAN_CHUNKS - 1 - ci) * CHUNK, CHUNK), CHUNK)
    return fwd, bwd


def _rw_chunk(r, v, kk, bb, kd, lw, st, rev):
    c = CHUNK
    half = RW_HEAD
    npair = r.shape[0]
    ri = lax.broadcasted_iota(jnp.int32, (1, c, c), 1)
    ci = lax.broadcasted_iota(jnp.int32, (1, c, c), 2)
    if rev:
        incl, strict = ci >= ri, ci > ri
    else:
        incl, strict = ci <= ri, ci < ri
    cum = _chunk_cumsum(lw, rev)
    tot = cum[:, 0:1] if rev else cum[:, c - 1:c]
    e_in = jnp.exp2(cum)
    e_prev = jnp.exp2(cum - lw)
    e_inv = jnp.exp2(-cum)
    e_out = jnp.exp2(tot - cum)
    rt = (r * e_in).astype(BF16)
    at = (-kk * e_prev).astype(BF16)
    bi = (bb * e_inv).astype(BF16)
    ki = (kd * e_inv).astype(BF16)
    bh = (bb * e_out).astype(BF16)
    kh = (kd * e_out).astype(BF16)
    vb = v.astype(BF16)
    sb = st.astype(BF16)
    lane = lax.broadcasted_iota(jnp.int32, (1, 1, LANES), 2)
    eye = (ri == ci).astype(F32)

    h0 = _bmm_nt(jnp.concatenate([at, rt], axis=1), sb)
    ah0, o_full = h0[:, :c], h0[:, c:]
    u_full = jnp.zeros((npair, c, LANES), F32)
    zero = jnp.zeros_like(at)
    for hx in range(2):
        mx = (lane // half) == hx
        lhs = jnp.concatenate([jnp.where(mx, at, zero), jnp.where(mx, rt, zero)], axis=1)
        m_b = _bmm_nt(lhs, bi)
        m_k = _bmm_nt(lhs, ki)
        a_ab = jnp.where(strict, m_b[:, :c], 0.0)
        a_ak = jnp.where(strict, m_k[:, :c], 0.0)
        a_rb = jnp.where(incl, m_b[:, c:], 0.0)
        a_rk = jnp.where(incl, m_k[:, c:], 0.0)
        pm = eye + jnp.where(ri // 2 == ci // 2, a_ab, 0.0)
        m = 2
        while m < c:
            off = jnp.logical_and(ri // (2 * m) == ci // (2 * m), ri // m != ci // m)
            a_off = jnp.where(off, a_ab, 0.0).astype(BF16)
            pmb = pm.astype(BF16)
            pm = pm + _bmm(_bmm(pmb, a_off).astype(BF16), pmb)
            m *= 2
        av = _bmm(jnp.concatenate([a_ak, a_rk], axis=1).astype(BF16), vb)
        x_x = ah0 + av[:, :c]
        u_x = _bmm(pm.astype(BF16), x_x.astype(BF16))
        u_full = jnp.where(mx, u_x, u_full)
        o_x = _bmm(a_rb.astype(BF16), u_x.astype(BF16)) + av[:, c:]
        o_full = o_full + jnp.where(mx, o_x, 0.0)
    upd = _bmm_tn(jnp.concatenate([u_full.astype(BF16), vb], axis=1),
                  jnp.concatenate([bh, kh], axis=1))
    blk_r = lax.broadcasted_iota(jnp.int32, (1, LANES, LANES), 1) // half
    blk_c = lax.broadcasted_iota(jnp.int32, (1, LANES, LANES), 2) // half
    st_new = st * jnp.exp2(tot) + jnp.where(blk_r == blk_c, upd, 0.0)
    return o_full, st_new


def _rw_scan_kernel(*refs):
    ins = refs[:12]
    y_os = refs[12:14]
    st = refs[14]

    @pl.when(pl.program_id(1) == 0)
    def _():
        st[...] = jnp.zeros_like(st)

    def body(ci, carry):
        rows = _scan_rows(ci)
        loaded = []
        for dr in range(2):
            r_r, v_r, kk_r, bb_r, kd_r, lw_r = ins[6 * dr:6 * dr + 6]
            rs = rows[dr]
            loaded.append((r_r[0, :, rs].astype(F32), v_r[0, :, rs].astype(F32), kk_r[0, :, rs].astype(F32),
                           bb_r[0, 0, :, rs].astype(F32), kd_r[0, 0, :, rs].astype(F32), lw_r[0, 0, :, rs], st[dr]))
        results = [_rw_chunk(*loaded[dr], rev=(dr == 1)) for dr in range(2)]
        for dr in range(2):
            y_os[dr][0, :, rows[dr]] = results[dr][0]
            st[dr] = results[dr][1]
        return carry

    lax.fori_loop(0, SCAN_CHUNKS, body, 0)


def _rw_scan(feats, *, n_lat_chunks, n_ctx_chunks):
    r, v, kk, lw, bb, kd = feats
    b, ng, tp, _ = r.shape
    total, in_specs, out_specs, out_shape = _scan_specs(b, ng, tp, n_lat_chunks, n_ctx_chunks, 3, 3)
    return pl.pallas_call(
        _rw_scan_kernel,
        grid=(b, total),
        in_specs=in_specs,
        out_specs=out_specs,
        out_shape=out_shape,
        scratch_shapes=[pltpu.VMEM((2, ng, LANES, LANES), F32)],
        name="rwkv7_scan",
        compiler_params=pltpu.CompilerParams(
            dimension_semantics=("parallel", "arbitrary"), vmem_limit_bytes=VMEM_LIMIT),
    )(r, v, kk, bb, kd, lw, r, v, kk, bb, kd, lw)


def _residual_tail(yg_bf16, wo_ref, x_ref, mod_ref, pg_ref, o_ref):
    d = x_ref.shape[-1]
    o = _dot(yg_bf16, wo_ref[...])
    ms = jnp.mean(o * o, axis=-1, keepdims=True)
    o = o * lax.rsqrt(ms + NORM_EPS) * pg_ref[...]
    gate = mod_ref[0, 0][:, 2 * d:]
    o_ref[0] = x_ref[0] + gate * o


def _rw_out_kernel(yf_ref, yb_ref, z_ref, bv_ref, lnw_ref, lnb_ref, hs_ref, wo_ref, x_ref, mod_ref, pg_ref,
                   o_ref, yg):
    inv = 1.0 / RW_HEAD
    hs = hs_ref[...]
    for gq in range(yf_ref.shape[1]):
        cs = slice(gq * LANES, (gq + 1) * LANES)
        y = yf_ref[0, gq] + yb_ref[0, gq]
        mu = _dot(y.astype(BF16), hs) * inv
        yc = y - mu
        var = _dot((yc * yc).astype(BF16), hs) * inv
        yn = yc * lax.rsqrt(var + LN_X_EPS) * lnw_ref[:, cs] + lnb_ref[:, cs] + bv_ref[0, gq].astype(F32)
        yg[:, cs] = yn.astype(BF16) * z_ref[0, gq]
    _residual_tail(yg[...], wo_ref, x_ref, mod_ref, pg_ref, o_ref)


def _hg_out_kernel(yf_ref, yb_ref, sg_ref, gn_ref, wo_ref, x_ref, mod_ref, pg_ref, o_ref, yg):
    for hx in range(yf_ref.shape[1]):
        cs = slice(hx * LANES, (hx + 1) * LANES)
        y = yf_ref[0, hx] + yb_ref[0, hx]
        ms = jnp.mean(y * y, axis=-1, keepdims=True)
        yn = y * lax.rsqrt(ms + NORM_EPS) * gn_ref[...]
        yg[:, cs] = yn.astype(BF16) * sg_ref[0, hx]
    _residual_tail(yg[...], wo_ref, x_ref, mod_ref, pg_ref, o_ref)


def _out_group_spec(di):
    return pl.BlockSpec((1, di // LANES, OTILE, LANES), lambda bi, j: (bi, 0, j, 0))


def _mixer_out(name, kernel_fn, ys, extras, extra_specs, wo, xs, mod, post_g, n_tiles):
    b, tp, d = xs.shape
    di = wo.shape[0]
    per = TILE // OTILE
    tokd = pl.BlockSpec((1, OTILE, d), lambda bi, j: (bi, j, 0))
    in_specs = [_out_group_spec(di)] * 2 + extra_specs + [
        pl.BlockSpec((di, d), lambda bi, j: (0, 0)),
        tokd,
        pl.BlockSpec((1, 1, 1, mod.shape[-1]), lambda bi, j: (bi, j // per, 0, 0)),
        pl.BlockSpec((1, d), lambda bi, j: (0, 0)),
    ]
    n_in = len(in_specs)
    return pl.pallas_call(
        kernel_fn,
        grid=(b, n_tiles),
        in_specs=in_specs,
        out_specs=tokd,
        out_shape=jax.ShapeDtypeStruct(xs.shape, F32),
        scratch_shapes=[pltpu.VMEM((OTILE, di), BF16)],
        input_output_aliases={n_in - 3: 0},
        name=name,
        compiler_params=pltpu.CompilerParams(
            dimension_semantics=("parallel", "parallel"), vmem_limit_bytes=VMEM_LIMIT),
    )(*ys, *extras, wo, xs, mod, post_g)


def _hg_feat_kernel(x_ref, mod_ref, g_ref, w_ref, lb_ref, q_o, i_o, sg_o, lf_o, kd_o, hbf):
    n = pl.program_id(2)
    d = x_ref.shape[-1]

    @pl.when(n == 0)
    def _prologue():
        m = mod_ref[0, 0]
        shift, scale = m[:, :d], m[:, d:2 * d]
        g = g_ref[...]

        def norm_rows(rb, c):
            r0 = pl.multiple_of(rb * GRID_W, GRID_W)
            hbf[pl.ds(r0, GRID_W), :] = _adaln(x_ref[0, pl.ds(r0, GRID_W), :], g, scale, shift).astype(BF16)
            return c

        lax.fori_loop(0, TILE // GRID_W, norm_rows, 0)

    h = hbf[...]
    lb = lb_ref[...]
    _store_groups(q_o, (0,), _silu(_dot(h, w_ref[0, 0])))
    for dr in range(2):
        f = lb + (1.0 - lb) * jax.nn.sigmoid(_dot(h, w_ref[0, 1 + dr]))
        _store_groups(kd_o, (dr, 0), 1.0 - f)
        _store_groups(lf_o, (dr, 0), jnp.log2(f))
    _store_groups(i_o, (0,), _dot(h, w_ref[0, 3]))
    _store_groups(sg_o, (0,), _silu(_dot(h, w_ref[0, 4])))


def _hg_features(xs, mod, pre_g, w_in5, lb, *, tn=256):
    b, tp, d = xs.shape
    di = w_in5.shape[-1]
    nt = di // tn
    one, two, s1, s2 = _group_major_specs(b, tp, di, tn)
    return pl.pallas_call(
        _hg_feat_kernel,
        grid=(b, tp // TILE, nt),
        in_specs=[
            pl.BlockSpec((1, TILE, d), lambda bi, j, n: (bi, j, 0)),
            pl.BlockSpec((1, 1, 1, mod.shape[-1]), lambda bi, j, n: (bi, j, 0, 0)),
            pl.BlockSpec((1, d), lambda bi, j, n: (0, 0)),
            pl.BlockSpec((1, 5, d, tn), lambda bi, j, n: (n, 0, 0, 0)),
            pl.BlockSpec((1, tn), lambda bi, j, n: (0, n)),
        ],
        out_specs=[one, one, one, two, two],
        out_shape=[s1(BF16), s1(BF16), s1(BF16), s2(F32), s2(BF16)],
        scratch_shapes=[pltpu.VMEM((TILE, d), BF16)],
        name="hgrn2_features",
        compiler_params=pltpu.CompilerParams(
            dimension_semantics=("parallel", "parallel", "arbitrary"), vmem_limit_bytes=VMEM_LIMIT),
    )(xs, mod, pre_g, _column_blocks(w_in5, tn), lb)


def _hg_chunk(q, k, v, lf2, st, rev):
    c = CHUNK
    sub = 8
    nh = q.shape[0]
    row = lax.broadcasted_iota(jnp.int32, (1, c, 1), 1)
    ri = lax.broadcasted_iota(jnp.int32, (1, c, c), 1)
    ci = lax.broadcasted_iota(jnp.int32, (1, c, c), 2)
    p = lf2
    att = jnp.zeros((nh, c, c), F32)
    m = 1
    while m < c:
        same = (ri // (2 * m)) == (ci // (2 * m))
        if m >= sub:
            zeros = jnp.zeros((nh, m, LANES), BF16)
            q_parts, k_parts, p_parts = [], [], []
            for blk in range(c // (2 * m)):
                lo = slice(blk * 2 * m, blk * 2 * m + m)
                hi = slice(blk * 2 * m + m, (blk + 1) * 2 * m)
                early, later = (hi, lo) if rev else (lo, hi)
                bidx = blk * 2 * m + (m if rev else m - 1)
                tt = p[:, bidx:bidx + 1, :]
                q_l = (q[:, later] * jnp.exp2(p[:, later])).astype(BF16)
                k_e = (k[:, early] * jnp.exp2(tt - p[:, early])).astype(BF16)
                if rev:
                    q_parts += [q_l, zeros]
                    k_parts += [zeros, k_e]
                    p_parts += [p[:, later] + tt, p[:, early]]
                else:
                    q_parts += [zeros, q_l]
                    k_parts += [k_e, zeros]
                    p_parts += [p[:, early], p[:, later] + tt]
            mm = _bmm_nt(jnp.concatenate(q_parts, axis=1), jnp.concatenate(k_parts, axis=1))
            att = att + (mm if 2 * m == c else jnp.where(same, mm, 0.0))
            p = jnp.concatenate(p_parts, axis=1)
        else:
            in_hi = ((row // m) % 2) == 1
            late = jnp.logical_not(in_hi) if rev else in_hi
            if m == 1:
                tt = jnp.where(late, pltpu.roll(p, c - 1 if rev else 1, axis=1), p)
            else:
                p4 = p.reshape(nh, c // sub, sub, LANES)
                srow = lax.broadcasted_iota(jnp.int32, (1, 1, sub, 1), 2)
                tt4 = None
                for blk in range(sub // (2 * m)):
                    bidx = blk * 2 * m + (m if rev else m - 1)
                    cand = jnp.broadcast_to(p4[:, :, bidx:bidx + 1, :], p4.shape)
                    tt4 = cand if tt4 is None else jnp.where(srow // (2 * m) == blk, cand, tt4)
                tt = tt4.reshape(nh, c, LANES)
            ee = jnp.exp2(jnp.where(late, p, tt - p))
            qt = jnp.where(late, q * ee, 0.0).astype(BF16)
            kt = jnp.where(late, 0.0, k * ee).astype(BF16)
            att = att + jnp.where(same, _bmm_nt(qt, kt), 0.0)
            p = jnp.where(late, p + tt, p)
        m *= 2
    tot = p[:, 0:1] if rev else p[:, c - 1:c]
    vb = v.astype(BF16)
    y = (_bmm(att.astype(BF16), vb) + jnp.sum(q * k, axis=2, keepdims=True) * v
         + _bmm_nt((q * jnp.exp2(p)).astype(BF16), st.astype(BF16)))
    st_new = st * jnp.exp2(tot) + _bmm_tn(vb, (k * jnp.exp2(tot - p)).astype(BF16))
    return y, st_new


def _hg_scan_kernel(*refs):
    ins = refs[:8]
    y_os = refs[8:10]
    st = refs[10]

    @pl.when(pl.program_id(1) == 0)
    def _():
        st[...] = jnp.zeros_like(st)

    def body(ci, carry):
        rows = _scan_rows(ci)
        loaded = []
        for dr in range(2):
            q_r, v_r, kd_r, lf_r = ins[4 * dr:4 * dr + 4]
            rs = rows[dr]
            loaded.append((q_r[0, :, rs].astype(F32), kd_r[0, 0, :, rs].astype(F32), v_r[0, :, rs].astype(F32),
                           lf_r[0, 0, :, rs], st[dr]))
        results = [_hg_chunk(*loaded[dr], rev=(dr == 1)) for dr in range(2)]
        for dr in range(2):
            y_os[dr][0, :, rows[dr]] = results[dr][0]
            st[dr] = results[dr][1]
        return carry

    lax.fori_loop(0, SCAN_CHUNKS, body, 0)


def _hg_scan(q, v, lf, kd, *, n_lat_chunks, n_ctx_chunks):
    b, ng, tp, _ = q.shape
    total, in_specs, out_specs, out_shape = _scan_specs(b, ng, tp, n_lat_chunks, n_ctx_chunks, 2, 2)
    return pl.pallas_call(
        _hg_scan_kernel,
        grid=(b, total),
        in_specs=in_specs,
        out_specs=out_specs,
        out_shape=out_shape,
        scratch_shapes=[pltpu.VMEM((2, ng, LANES, LANES), F32)],
        name="hgrn2_scan",
        compiler_params=pltpu.CompilerParams(
            dimension_semantics=("parallel", "arbitrary"), vmem_limit_bytes=VMEM_LIMIT),
    )(q, v, kd, lf, q, v, kd, lf)


def _pad_rows(w, rows):
    return jnp.pad(w, ((0, rows - w.shape[0]), (0, 0)))


def _pad_cols(w, cols):
    return jnp.pad(w, ((0, 0), (0, cols - w.shape[1])))


def kernel(x, c, ctx, c_ctx, mod_w, mod_b, pre_g, post_g, rw_mix, rw_proj, rw_wo, rw_w0, rw_w1, rw_w2, rw_a0,
           rw_a1, rw_a2, rw_v0, rw_v1, rw_v2, rw_kk, rw_ka, rw_rk, rw_lnw, rw_lnb, hg_win, hg_wo, hg_gn, hg_lb):
    b, seq, d = x.shape
    ctx_len = ctx.shape[1]
    depth = mod_w.shape[0]
    di = rw_proj.shape[-1]
    assert seq % TILE == 0 and ctx_len % CHUNK == 0 and ctx_len <= TILE and seq % GRID_W == 0
    n_lat_tiles = seq // TILE
    tp = seq + TILE
    n_lat_chunks, n_ctx_chunks = seq // CHUNK, ctx_len // CHUNK

    xs = jnp.concatenate([x, ctx, jnp.zeros((b, tp - seq - ctx_len, d), x.dtype)], axis=1)

    rows = ((b + 1 + 7) // 8) * 8
    cond = jnp.concatenate([c, c_ctx[None], jnp.zeros((rows - b - 1, d), c.dtype)], axis=0)
    mod_all = _modulation(cond, mod_w, mod_b)
    mod_lat = jnp.broadcast_to(mod_all[:, :b, None, :], (depth, b, n_lat_tiles, 3 * d))
    mod_ctx = jnp.broadcast_to(mod_all[:, b:b + 1, None, :], (depth, b, 1, 3 * d))
    mod_t = jnp.concatenate([mod_lat, mod_ctx], axis=2)[:, :, :, None, :]

    p_lb = jax.nn.softmax(hg_lb.astype(F32), axis=0)
    lb_all = jnp.cumsum(p_lb, axis=0) - p_lb[0]

    vf = None
    for i in range(depth):
        ctx_out = i < depth - 1
        n_out_tiles = (seq + (ctx_len if ctx_out else 0)) // OTILE
        j = i // 2
        mod_i = mod_t[i]
        pg = pre_g[i][None]
        if i % 2 == 0:
            r64 = rw_w1.shape[-1]
            w2 = jnp.stack([jnp.pad(rw_w2[j, 0], ((0, LANES - r64), (0, 0))),
                            jnp.pad(rw_w2[j, 1], ((r64, LANES - 2 * r64), (0, 0)))])
            a2 = jnp.stack([jnp.pad(rw_a2[j, 0], ((0, LANES - r64), (0, 0))),
                            jnp.pad(rw_a2[j, 1], ((r64, LANES - 2 * r64), (0, 0)))])
            if j == 0:
                v0 = jnp.zeros((di,), F32)
                v1 = jnp.zeros((d, LANES), BF16)
                v2 = jnp.zeros((LANES, di), BF16)
            else:
                v0 = rw_v0[j - 1]
                v1 = _pad_cols(rw_v1[j - 1], LANES).astype(BF16)
                v2 = _pad_rows(rw_v2[j - 1], LANES).astype(BF16)
            p = {
                "mix": rw_mix[j],
                "proj": rw_proj[j].astype(BF16),
                "w1": jnp.concatenate([rw_w1[j, 0], rw_w1[j, 1]], axis=1).astype(BF16),
                "a1": jnp.concatenate([rw_a1[j, 0], rw_a1[j, 1]], axis=1).astype(BF16),
                "v1": v1,
                "w2": w2.astype(BF16),
                "a2": a2.astype(BF16),
                "v2": v2,
                "vec": jnp.stack([rw_w0[j, 0], rw_w0[j, 1], rw_a0[j, 0], rw_a0[j, 1], v0, rw_kk[j], rw_ka[j],
                                  rw_rk[j]]),
            }
            r, v, kk, z, bv, lw, bb, kd = _rw_features(xs, mod_i, pg, p, vf if j > 0 else None,
                                                       n_lat_tiles=n_lat_tiles, ctx_len=ctx_len)
            if j == 0:
                vf = v
            yf, yb = _rw_scan((r, v, kk, lw, bb, kd), n_lat_chunks=n_lat_chunks, n_ctx_chunks=n_ctx_chunks)
            gspec = _out_group_spec(di)
            extras = [z, bv, rw_lnw[j][None], rw_lnb[j][None], _head_sum_matrix(LANES, RW_HEAD)]
            especs = [gspec, gspec, pl.BlockSpec((1, di), lambda bi, jj: (0, 0)),
                      pl.BlockSpec((1, di), lambda bi, jj: (0, 0)), pl.BlockSpec((LANES, LANES), lambda bi, jj: (0, 0))]
            xs = _mixer_out("rwkv7_out", _rw_out_kernel, (yf, yb), extras, especs, rw_wo[j].astype(BF16), xs, mod_i,
                            post_g[i][None], n_out_tiles)
        else:
            w5 = hg_win[j].reshape(d, 5, di).transpose(1, 0, 2).astype(BF16)
            q, iv, sg, lf, kd = _hg_features(xs, mod_i, pg, w5, lb_all[i][None])
            yf, yb = _hg_scan(q, iv, lf, kd, n_lat_chunks=n_lat_chunks, n_ctx_chunks=n_ctx_chunks)
            extras = [sg, hg_gn[j][None]]
            especs = [_out_group_spec(di), pl.BlockSpec((1, HG_HEAD), lambda bi, jj: (0, 0))]
            xs = _mixer_out("hgrn2_out", _hg_out_kernel, (yf, yb), extras, especs, hg_wo[j].astype(BF16), xs, mod_i,
                            post_g[i][None], n_out_tiles)
    return xs[:, :seq]
```

```python
import functools
import math

import jax
import jax.numpy as jnp
from jax import lax
from jax.experimental import pallas as pl
from jax.experimental.pallas import tpu as pltpu

F32 = jnp.float32
BF16 = jnp.bfloat16
HI = lax.Precision.HIGHEST

GRID_W = 64
RW_HEAD = 64
HG_HEAD = 128
CHUNK = 64
TILE = 512
OTILE = 256
RW_SCAN_CHUNKS = 4
RW_SCAN_BATCH = 1
HG_SCAN_CHUNKS = 4
NORM_EPS = 1e-6
LN_X_EPS = 64e-5
LANES = 128
VMEM_LIMIT = 56 * 1024 * 1024


def _silu(x):
    return x * jax.nn.sigmoid(x)


def _dot(a, b):
    return jnp.dot(a, b, preferred_element_type=F32)


def _mod_kernel(s_ref, w_ref, b_ref, o_ref):
    s = _silu(s_ref[...])
    o_ref[0] = jnp.dot(s, w_ref[0], precision=HI, preferred_element_type=F32) + b_ref[0]


def _modulation(cond, mod_w, mod_b):
    depth, d, d3 = mod_w.shape
    rows = cond.shape[0]
    nb = d3 // d
    return pl.pallas_call(
        _mod_kernel,
        grid=(depth, nb),
        in_specs=[
            pl.BlockSpec((rows, d), lambda i, n: (0, 0)),
            pl.BlockSpec((1, d, d), lambda i, n: (i, 0, n)),
            pl.BlockSpec((1, 1, d), lambda i, n: (i, 0, n)),
        ],
        out_specs=pl.BlockSpec((1, rows, d), lambda i, n: (i, 0, n)),
        out_shape=jax.ShapeDtypeStruct((depth, rows, d3), F32),
        name="adaln_modulation",
        compiler_params=pltpu.CompilerParams(vmem_limit_bytes=VMEM_LIMIT),
    )(cond, mod_w, mod_b.reshape(depth, 1, d3))


def _adaln(xb, g, scale, shift):
    ms = jnp.mean(xb * xb, axis=-1, keepdims=True)
    return xb * lax.rsqrt(ms + NORM_EPS) * g * (1.0 + scale) + shift


def _store_groups(o_ref, lead, val):
    for gq in range(val.shape[1] // LANES):
        o_ref[lead + (gq,)] = val[:, gq * LANES:(gq + 1) * LANES].astype(o_ref.dtype)


def _load_groups(ref, lead):
    n = ref.shape[len(lead)]
    return jnp.concatenate([ref[lead + (gq,)] for gq in range(n)], axis=1)


def _head_sum_matrix(width, head):
    r = lax.broadcasted_iota(jnp.int32, (width, width), 0) // head
    c = lax.broadcasted_iota(jnp.int32, (width, width), 1) // head
    return (r == c).astype(BF16)


def _rw_feat_kernel(n_lat_tiles, ctx_len, has_vres, *refs):
    (x_ref, xp_ref, xn_ref, mod_ref, g_ref, mix_ref, w1_ref, a1_ref, v1_ref,
     proj_ref, w2_ref, a2_ref, v2_ref, vec_ref, hs_ref) = refs[:15]
    pos = 15
    vf_ref = None
    if has_vres:
        vf_ref = refs[pos]
        pos += 1
    (r_o, v_o, kk_o, z_o, bv_o, lw_o, bb_o, kd_o) = refs[pos:pos + 8]
    hbuf, xm, tw, ta, tv = refs[pos + 8:]

    j = pl.program_id(1)
    n = pl.program_id(2)
    d = x_ref.shape[-1]
    q4 = d // 4
    nblk = TILE // GRID_W

    @pl.when(n == 0)
    def _prologue():
        m = mod_ref[0, 0]
        shift, scale = m[:, :d], m[:, d:2 * d]
        g = g_ref[...]
        is_lat = j < n_lat_tiles
        prev_ok = jnp.logical_and(j >= 1, is_lat)
        next_ok = j < n_lat_tiles - 1

        def norm_rows(rb, c):
            r0 = pl.multiple_of(rb * GRID_W, GRID_W)
            hbuf[pl.ds(r0 + GRID_W, GRID_W), :] = _adaln(x_ref[0, pl.ds(r0, GRID_W), :], g, scale, shift)
            return c

        lax.fori_loop(0, nblk, norm_rows, 0)
        hp = _adaln(xp_ref[0], g, scale, shift)
        hbuf[0:GRID_W, :] = jnp.where(prev_ok, hp, 0.0)
        hn = _adaln(xn_ref[0], g, scale, shift)
        hbuf[TILE + GRID_W:TILE + 2 * GRID_W, :] = jnp.where(next_ok, hn, 0.0)

        row = lax.broadcasted_iota(jnp.int32, (GRID_W, 1), 0)

        def store_mixed(r0, cs, h_c, hs_c):
            dd = hs_c - h_c
            for k in range(6):
                xm[k, pl.ds(r0, GRID_W), cs] = (h_c + dd * mix_ref[k:k + 1, cs]).astype(BF16)

        def mix_lat(rb, c):
            r0 = pl.multiple_of(rb * GRID_W, GRID_W)
            for qi in range(4):
                cs = slice(qi * q4, (qi + 1) * q4)
                h_c = hbuf[pl.ds(r0 + GRID_W, GRID_W), cs]
                if qi == 0:
                    hs_c = jnp.where(row == 0, 0.0, pltpu.roll(h_c, 1, axis=0))
                elif qi == 1:
                    hs_c = jnp.where(row == GRID_W - 1, 0.0, pltpu.roll(h_c, GRID_W - 1, axis=0))
                elif qi == 2:
                    hs_c = hbuf[pl.ds(r0, GRID_W), cs]
                else:
                    hs_c = hbuf[pl.ds(r0 + 2 * GRID_W, GRID_W), cs]
                store_mixed(r0, cs, h_c, hs_c)
            return c

        def mix_ctx(rb, c):
            r0 = pl.multiple_of(rb * GRID_W, GRID_W)
            t = row + rb * GRID_W
            for qi in range(4):
                cs = slice(qi * q4, (qi + 1) * q4)
                h_c = hbuf[pl.ds(r0 + GRID_W, GRID_W), cs]
                if qi < 2:
                    edge = hbuf[pl.ds(r0 + GRID_W - 1, 1), cs]
                    hs_c = jnp.where(row == 0, edge, pltpu.roll(h_c, 1, axis=0))
                else:
                    edge = hbuf[pl.ds(r0 + 2 * GRID_W, 1), cs]
                    hs_c = jnp.where(row == GRID_W - 1, edge, pltpu.roll(h_c, GRID_W - 1, axis=0))
                    hs_c = jnp.where(t == ctx_len - 1, 0.0, hs_c)
                store_mixed(r0, cs, h_c, hs_c)
            return c

        @pl.when(is_lat)
        def _():
            lax.fori_loop(0, nblk, mix_lat, 0)

        @pl.when(jnp.logical_not(is_lat))
        def _():
            lax.fori_loop(0, nblk, mix_ctx, 0)

        tw[...] = jnp.tanh(_dot(xm[1], w1_ref[...])).astype(BF16)
        ta[...] = _dot(xm[4], a1_ref[...]).astype(BF16)
        if has_vres:
            tv[...] = _dot(xm[3], v1_ref[...]).astype(BF16)

    vec = vec_ref[...]
    r = _dot(xm[0], proj_ref[0, 0])
    k = _dot(xm[2], proj_ref[0, 1])
    v = _dot(xm[3], proj_ref[0, 2])
    z = _silu(_dot(xm[5], proj_ref[0, 3]))
    if has_vres:
        gate = jax.nn.sigmoid(vec[4:5] + _dot(tv[...], v2_ref[...]))
        v = v + (_load_groups(vf_ref, (0,)).astype(F32) - v) * gate
    hs = hs_ref[...]
    kk = k * vec[5:6]
    ss = _dot((kk * kk).astype(BF16), hs)
    kk = kk * lax.rsqrt(jnp.maximum(ss, 1e-24))
    decay_scale = -math.exp(-0.5) * math.log2(math.e)
    ksum = None
    for dr in range(2):
        u = vec[dr:dr + 1] + _dot(tw[...], w2_ref[dr])
        _store_groups(lw_o, (dr, 0), decay_scale * jax.nn.sigmoid(u))
        a = jax.nn.sigmoid(vec[2 + dr:3 + dr] + _dot(ta[...], a2_ref[dr]))
        kd = k * (1.0 + (a - 1.0) * vec[6:7])
        _store_groups(bb_o, (dr, 0), kk * a)
        _store_groups(kd_o, (dr, 0), kd)
        ksum = kd if ksum is None else ksum + kd
    bonus = _dot((r * ksum * vec[7:8]).astype(BF16), hs)
    _store_groups(r_o, (0,), r)
    _store_groups(v_o, (0,), v)
    _store_groups(kk_o, (0,), kk)
    _store_groups(z_o, (0,), z)
    _store_groups(bv_o, (0,), bonus * v)


def _group_major_specs(b, tp, di, tn):
    gq = tn // LANES
    one = pl.BlockSpec((1, gq, TILE, LANES), lambda bi, j, n: (bi, n, j, 0))
    two = pl.BlockSpec((2, 1, gq, TILE, LANES), lambda bi, j, n: (0, bi, n, j, 0))
    s1 = lambda dt: jax.ShapeDtypeStruct((b, di // LANES, tp, LANES), dt)
    s2 = lambda dt: jax.ShapeDtypeStruct((2, b, di // LANES, tp, LANES), dt)
    return one, two, s1, s2


def _column_blocks(w, tn):
    k, d, n = w.shape
    return w.reshape(k, d, n // tn, tn).transpose(2, 0, 1, 3)


def _rw_features(xs, mod, pre_g, p, vf, *, n_lat_tiles, ctx_len, tn=256):
    b, tp, d = xs.shape
    di = p["proj"].shape[-1]
    nt = di // tn
    ntiles = tp // TILE
    nb64 = tp // GRID_W
    per = TILE // GRID_W
    has_vres = vf is not None

    def tok(bi, j, n):
        return (bi, j, 0)

    in_specs = [
        pl.BlockSpec((1, TILE, d), tok),
        pl.BlockSpec((1, GRID_W, d), lambda bi, j, n: (bi, jnp.maximum(j * per - 1, 0), 0)),
        pl.BlockSpec((1, GRID_W, d), lambda bi, j, n: (bi, jnp.minimum((j + 1) * per, nb64 - 1), 0)),
        pl.BlockSpec((1, 1, 1, mod.shape[-1]), lambda bi, j, n: (bi, j, 0, 0)),
        pl.BlockSpec((1, d), lambda bi, j, n: (0, 0)),
        pl.BlockSpec((6, d), lambda bi, j, n: (0, 0)),
        pl.BlockSpec((d, LANES), lambda bi, j, n: (0, 0)),
        pl.BlockSpec((d, LANES), lambda bi, j, n: (0, 0)),
        pl.BlockSpec((d, LANES), lambda bi, j, n: (0, 0)),
        pl.BlockSpec((1, 4, d, tn), lambda bi, j, n: (n, 0, 0, 0)),
        pl.BlockSpec((2, LANES, tn), lambda bi, j, n: (0, 0, n)),
        pl.BlockSpec((2, LANES, tn), lambda bi, j, n: (0, 0, n)),
        pl.BlockSpec((LANES, tn), lambda bi, j, n: (0, n)),
        pl.BlockSpec((8, tn), lambda bi, j, n: (0, n)),
        pl.BlockSpec((tn, tn), lambda bi, j, n: (0, 0)),
    ]
    args = [xs, xs, xs, mod, pre_g, p["mix"], p["w1"], p["a1"], p["v1"], _column_blocks(p["proj"], tn), p["w2"], p["a2"],
            p["v2"], p["vec"], _head_sum_matrix(tn, RW_HEAD)]
    one, two, s1, s2 = _group_major_specs(b, tp, di, tn)
    if has_vres:
        in_specs.append(one)
        args.append(vf)
    return pl.pallas_call(
        functools.partial(_rw_feat_kernel, n_lat_tiles, ctx_len, has_vres),
        grid=(b, ntiles, nt),
        in_specs=in_specs,
        out_specs=[one, one, one, one, one, two, two, two],
        out_shape=[s1(BF16), s1(BF16), s1(BF16), s1(BF16), s1(BF16), s2(F32), s2(BF16), s2(BF16)],
        scratch_shapes=[
            pltpu.VMEM((TILE + 2 * GRID_W, d), F32),
            pltpu.VMEM((6, TILE, d), BF16),
            pltpu.VMEM((TILE, LANES), BF16),
            pltpu.VMEM((TILE, LANES), BF16),
            pltpu.VMEM((TILE, LANES), BF16),
        ],
        name="rwkv7_features",
        compiler_params=pltpu.CompilerParams(
            dimension_semantics=("parallel", "parallel", "arbitrary"), vmem_limit_bytes=VMEM_LIMIT),
    )(*args)


def _bmm(a, b):
    return lax.dot_general(a, b, (((2,), (1,)), ((0,), (0,))), preferred_element_type=F32)


def _bmm_nt(a, b):
    return lax.dot_general(a, b, (((2,), (2,)), ((0,), (0,))), preferred_element_type=F32)


def _bmm_tn(a, b):
    return lax.dot_general(a, b, (((1,), (1,)), ((0,), (0,))), preferred_element_type=F32)


def _chunk_cumsum(x, rev):
    c = x.shape[1]
    row = lax.broadcasted_iota(jnp.int32, (1, c, 1), 1)
    s = 1
    while s < c:
        if rev:
            x = x + jnp.where(row < c - s, pltpu.roll(x, c - s, axis=1), 0.0)
        else:
            x = x + jnp.where(row >= s, pltpu.roll(x, s, axis=1), 0.0)
        s *= 2
    return x


def _scan_chunk_maps(n_lat_chunks, n_ctx_chunks):
    total = n_lat_chunks + n_ctx_chunks

    def fwd(s):
        return (s + n_lat_chunks) % total

    def bwd(s):
        return total - 1 - s

    return total, fwd, bwd


def _scan_specs(b, ng, tp, n_lat_chunks, n_ctx_chunks, n_one, n_two, nb, nchunks):
    assert n_lat_chunks % nchunks == 0 and n_ctx_chunks % nchunks == 0 and b % nb == 0
    rows = CHUNK * nchunks
    total, fwd, bwd = _scan_chunk_maps(n_lat_chunks // nchunks, n_ctx_chunks // nchunks)
    in_specs = []
    for dr, cm in enumerate((fwd, bwd)):
        one = pl.BlockSpec((nb, ng, rows, LANES), lambda bi, s, cm=cm: (bi, 0, cm(s), 0))
        two = pl.BlockSpec((1, nb, ng, rows, LANES), lambda bi, s, cm=cm, dr=dr: (dr, bi, 0, cm(s), 0))
        in_specs += [one] * n_one + [two] * n_two
    out_specs = [pl.BlockSpec((nb, ng, rows, LANES), lambda bi, s, cm=cm: (bi, 0, cm(s), 0)) for cm in (fwd, bwd)]
    out_shape = [jax.ShapeDtypeStruct((b, ng, tp, LANES), F32)] * 2
    return (b // nb, total), in_specs, out_specs, out_shape


def _scan_rows(ci, nchunks):
    fwd = pl.ds(pl.multiple_of(ci * CHUNK, CHUNK), CHUNK)
    bwd = pl.ds(pl.multiple_of((nchunks - 1 - ci) * CHUNK, CHUNK), CHUNK)
    return fwd, bwd


def _load_chunk(ref, lead, rs):
    x = ref[lead + (slice(None), slice(None), rs)]
    return x.reshape((x.shape[0] * x.shape[1],) + x.shape[2:]).astype(F32)


def _rw_chunk(r, v, kk, bb, kd, lw, st, rev):
    c = CHUNK
    half = RW_HEAD
    ri = lax.broadcasted_iota(jnp.int32, (1, c, LANES), 1)
    ci = lax.broadcasted_iota(jnp.int32, (1, c, LANES), 2) % half
    head_a = lax.broadcasted_iota(jnp.int32, (1, 1, LANES), 2) < half

    def split_rows(x):
        zero = jnp.zeros_like(x)
        return jnp.concatenate([jnp.where(head_a, x, zero), jnp.where(head_a, zero, x)], axis=1)

    if rev:
        incl, strict = ci >= ri, ci > ri
    else:
        incl, strict = ci <= ri, ci < ri
    cum = _chunk_cumsum(lw, rev)
    tot = cum[:, 0:1] if rev else cum[:, c - 1:c]
    e_in = jnp.exp2(cum)
    e_prev = jnp.exp2(cum - lw)
    e_inv = jnp.exp2(-cum)
    e_out = jnp.exp2(tot - cum)
    rt = (r * e_in).astype(BF16)
    at = (-kk * e_prev).astype(BF16)
    bi = (bb * e_inv).astype(BF16)
    ki = (kd * e_inv).astype(BF16)
    bh = (bb * e_out).astype(BF16)
    kh = (kd * e_out).astype(BF16)
    vb = v.astype(BF16)
    sb = st.astype(BF16)
    lhs = jnp.concatenate([at, rt], axis=1)
    h0 = _bmm_nt(lhs, sb)
    m_b = _bmm_nt(lhs, split_rows(bi))
    m_k = _bmm_nt(lhs, split_rows(ki))
    tri = jnp.concatenate([strict, incl], axis=1)
    m_b = m_b.astype(BF16)
    m_b = jnp.where(tri, m_b, jnp.zeros_like(m_b))
    m_k = m_k.astype(BF16)
    m_k = jnp.where(tri, m_k, jnp.zeros_like(m_k))
    ab16, a_rb = m_b[:, :c], m_b[:, c:]
    zero16 = jnp.zeros_like(ab16)
    pmb = jnp.where(ri == ci, jnp.ones_like(ab16), jnp.where(ri // 2 == ci // 2, ab16, zero16))
    m = 2
    while m < c:
        off = jnp.logical_and(ri // (2 * m) == ci // (2 * m), ri // m != ci // m)
        a_off = jnp.where(off, ab16, zero16)
        delta = _bmm(_bmm(pmb, split_rows(a_off)).astype(BF16), split_rows(pmb))
        pmb = jnp.where(off, delta.astype(BF16), pmb)
        m *= 2
    av = _bmm(m_k, split_rows(vb))
    x = h0[:, :c] + av[:, :c]
    u = _bmm(pmb, split_rows(x.astype(BF16)))
    ub = u.astype(BF16)
    o_full = h0[:, c:] + av[:, c:] + _bmm(a_rb, split_rows(ub))
    upd = _bmm_tn(jnp.concatenate([ub, vb], axis=1), jnp.concatenate([bh, kh], axis=1))
    blk_r = lax.broadcasted_iota(jnp.int32, (1, LANES, LANES), 1) // half
    blk_c = lax.broadcasted_iota(jnp.int32, (1, LANES, LANES), 2) // half
    st_new = st * jnp.exp2(tot) + jnp.where(blk_r == blk_c, upd, 0.0)
    return o_full, st_new


def _store_chunk(o_ref, rs, val):
    nb, ng = o_ref.shape[0], o_ref.shape[1]
    o_ref[:, :, rs] = val.reshape((nb, ng) + val.shape[1:])


def _rw_scan_kernel(*refs):
    ins = refs[:12]
    y_os = refs[12:14]
    st = refs[14]

    @pl.when(pl.program_id(1) == 0)
    def _():
        st[...] = jnp.zeros_like(st)

    def body(ci, carry):
        rows = _scan_rows(ci, RW_SCAN_CHUNKS)
        loaded = []
        for dr in range(2):
            r_r, v_r, kk_r, bb_r, kd_r, lw_r = ins[6 * dr:6 * dr + 6]
            rs = rows[dr]
            loaded.append((_load_chunk(r_r, (), rs), _load_chunk(v_r, (), rs), _load_chunk(kk_r, (), rs),
                           _load_chunk(bb_r, (0,), rs), _load_chunk(kd_r, (0,), rs), _load_chunk(lw_r, (0,), rs),
                           st[dr]))
        results = [_rw_chunk(*loaded[dr], rev=(dr == 1)) for dr in range(2)]
        for dr in range(2):
            _store_chunk(y_os[dr], rows[dr], results[dr][0])
            st[dr] = results[dr][1]
        return carry

    lax.fori_loop(0, RW_SCAN_CHUNKS, body, 0)


def _rw_scan(feats, *, n_lat_chunks, n_ctx_chunks):
    r, v, kk, lw, bb, kd = feats
    b, ng, tp, _ = r.shape
    nb = RW_SCAN_BATCH if b % RW_SCAN_BATCH == 0 else 1
    grid, in_specs, out_specs, out_shape = _scan_specs(b, ng, tp, n_lat_chunks, n_ctx_chunks, 3, 3, nb,
                                                       RW_SCAN_CHUNKS)
    return pl.pallas_call(
        _rw_scan_kernel,
        grid=grid,
        in_specs=in_specs,
        out_specs=out_specs,
        out_shape=out_shape,
        scratch_shapes=[pltpu.VMEM((2, nb * ng, LANES, LANES), F32)],
        name="rwkv7_scan",
        compiler_params=pltpu.CompilerParams(
            dimension_semantics=("parallel", "arbitrary"), vmem_limit_bytes=VMEM_LIMIT),
    )(r, v, kk, bb, kd, lw, r, v, kk, bb, kd, lw)


def _residual_tail(yg_bf16, wo_ref, x_ref, mod_ref, pg_ref, o_ref):
    d = x_ref.shape[-1]
    o = _dot(yg_bf16, wo_ref[...])
    ms = jnp.mean(o * o, axis=-1, keepdims=True)
    o = o * lax.rsqrt(ms + NORM_EPS) * pg_ref[...]
    gate = mod_ref[0, 0][:, 2 * d:]
    o_ref[0] = x_ref[0] + gate * o


def _rw_out_kernel(yf_ref, yb_ref, z_ref, bv_ref, lnw_ref, lnb_ref, hs_ref, wo_ref, x_ref, mod_ref, pg_ref,
                   o_ref, yg):
    inv = 1.0 / RW_HEAD
    hs = hs_ref[...]
    for gq in range(yf_ref.shape[1]):
        cs = slice(gq * LANES, (gq + 1) * LANES)
        y = yf_ref[0, gq] + yb_ref[0, gq]
        mu = _dot(y.astype(BF16), hs) * inv
        yc = y - mu
        var = _dot((yc * yc).astype(BF16), hs) * inv
        yn = yc * lax.rsqrt(var + LN_X_EPS) * lnw_ref[:, cs] + lnb_ref[:, cs] + bv_ref[0, gq].astype(F32)
        yg[:, cs] = yn.astype(BF16) * z_ref[0, gq]
    _residual_tail(yg[...], wo_ref, x_ref, mod_ref, pg_ref, o_ref)


def _hg_out_kernel(yf_ref, yb_ref, sg_ref, gn_ref, wo_ref, x_ref, mod_ref, pg_ref, o_ref, yg):
    for hx in range(yf_ref.shape[1]):
        cs = slice(hx * LANES, (hx + 1) * LANES)
        y = yf_ref[0, hx] + yb_ref[0, hx]
        ms = jnp.mean(y * y, axis=-1, keepdims=True)
        yn = y * lax.rsqrt(ms + NORM_EPS) * gn_ref[...]
        yg[:, cs] = yn.astype(BF16) * sg_ref[0, hx]
    _residual_tail(yg[...], wo_ref, x_ref, mod_ref, pg_ref, o_ref)


def _out_group_spec(di):
    return pl.BlockSpec((1, di // LANES, OTILE, LANES), lambda bi, j: (bi, 0, j, 0))


def _mixer_out(name, kernel_fn, ys, extras, extra_specs, wo, xs, mod, post_g, n_tiles):
    b, tp, d = xs.shape
    di = wo.shape[0]
    per = TILE // OTILE
    tokd = pl.BlockSpec((1, OTILE, d), lambda bi, j: (bi, j, 0))
    in_specs = [_out_group_spec(di)] * 2 + extra_specs + [
        pl.BlockSpec((di, d), lambda bi, j: (0, 0)),
        tokd,
        pl.BlockSpec((1, 1, 1, mod.shape[-1]), lambda bi, j: (bi, j // per, 0, 0)),
        pl.BlockSpec((1, d), lambda bi, j: (0, 0)),
    ]
    n_in = len(in_specs)
    return pl.pallas_call(
        kernel_fn,
        grid=(b, n_tiles),
        in_specs=in_specs,
        out_specs=tokd,
        out_shape=jax.ShapeDtypeStruct(xs.shape, F32),
        scratch_shapes=[pltpu.VMEM((OTILE, di), BF16)],
        input_output_aliases={n_in - 3: 0},
        name=name,
        compiler_params=pltpu.CompilerParams(
            dimension_semantics=("parallel", "parallel"), vmem_limit_bytes=VMEM_LIMIT),
    )(*ys, *extras, wo, xs, mod, post_g)


def _hg_feat_kernel(x_ref, mod_ref, g_ref, w_ref, lb_ref, q_o, i_o, sg_o, lf_o, kd_o, hbf):
    n = pl.program_id(2)
    d = x_ref.shape[-1]

    @pl.when(n == 0)
    def _prologue():
        m = mod_ref[0, 0]
        shift, scale = m[:, :d], m[:, d:2 * d]
        g = g_ref[...]

        def norm_rows(rb, c):
            r0 = pl.multiple_of(rb * GRID_W, GRID_W)
            hbf[pl.ds(r0, GRID_W), :] = _adaln(x_ref[0, pl.ds(r0, GRID_W), :], g, scale, shift).astype(BF16)
            return c

        lax.fori_loop(0, TILE // GRID_W, norm_rows, 0)

    h = hbf[...]
    lb = lb_ref[...]
    _store_groups(q_o, (0,), _silu(_dot(h, w_ref[0, 0])))
    for dr in range(2):
        f = lb + (1.0 - lb) * jax.nn.sigmoid(_dot(h, w_ref[0, 1 + dr]))
        _store_groups(kd_o, (dr, 0), 1.0 - f)
        _store_groups(lf_o, (dr, 0), jnp.log2(f))
    _store_groups(i_o, (0,), _dot(h, w_ref[0, 3]))
    _store_groups(sg_o, (0,), _silu(_dot(h, w_ref[0, 4])))


def _hg_features(xs, mod, pre_g, w_in5, lb, *, tn=256):
    b, tp, d = xs.shape
    di = w_in5.shape[-1]
    nt = di // tn
    one, two, s1, s2 = _group_major_specs(b, tp, di, tn)
    return pl.pallas_call(
        _hg_feat_kernel,
        grid=(b, tp // TILE, nt),
        in_specs=[
            pl.BlockSpec((1, TILE, d), lambda bi, j, n: (bi, j, 0)),
            pl.BlockSpec((1, 1, 1, mod.shape[-1]), lambda bi, j, n: (bi, j, 0, 0)),
            pl.BlockSpec((1, d), lambda bi, j, n: (0, 0)),
            pl.BlockSpec((1, 5, d, tn), lambda bi, j, n: (n, 0, 0, 0)),
            pl.BlockSpec((1, tn), lambda bi, j, n: (0, n)),
        ],
        out_specs=[one, one, one, two, two],
        out_shape=[s1(BF16), s1(BF16), s1(BF16), s2(F32), s2(BF16)],
        scratch_shapes=[pltpu.VMEM((TILE, d), BF16)],
        name="hgrn2_features",
        compiler_params=pltpu.CompilerParams(
            dimension_semantics=("parallel", "parallel", "arbitrary"), vmem_limit_bytes=VMEM_LIMIT),
    )(xs, mod, pre_g, _column_blocks(w_in5, tn), lb)


def _hg_chunk(q, k, v, lf2, st, rev):
    c = CHUNK
    sub = 8
    nh = q.shape[0]
    row = lax.broadcasted_iota(jnp.int32, (1, c, 1), 1)
    ri = lax.broadcasted_iota(jnp.int32, (1, c, c), 1)
    ci = lax.broadcasted_iota(jnp.int32, (1, c, c), 2)
    p = lf2
    att = jnp.zeros((nh, c, c), F32)
    m = 1
    while m < c:
        same = (ri // (2 * m)) == (ci // (2 * m))
        if m >= sub:
            zeros = jnp.zeros((nh, m, LANES), BF16)
            q_parts, k_parts, p_parts = [], [], []
            for blk in range(c // (2 * m)):
                lo = slice(blk * 2 * m, blk * 2 * m + m)
                hi = slice(blk * 2 * m + m, (blk + 1) * 2 * m)
                early, later = (hi, lo) if rev else (lo, hi)
                bidx = blk * 2 * m + (m if rev else m - 1)
                tt = p[:, bidx:bidx + 1, :]
                q_l = (q[:, later] * jnp.exp2(p[:, later])).astype(BF16)
                k_e = (k[:, early] * jnp.exp2(tt - p[:, early])).astype(BF16)
                if rev:
                    q_parts += [q_l, zeros]
                    k_parts += [zeros, k_e]
                    p_parts += [p[:, later] + tt, p[:, early]]
                else:
                    q_parts += [zeros, q_l]
                    k_parts += [k_e, zeros]
                    p_parts += [p[:, early], p[:, later] + tt]
            mm = _bmm_nt(jnp.concatenate(q_parts, axis=1), jnp.concatenate(k_parts, axis=1))
            att = att + (mm if 2 * m == c else jnp.where(same, mm, 0.0))
            p = jnp.concatenate(p_parts, axis=1)
        else:
            in_hi = ((row // m) % 2) == 1
            late = jnp.logical_not(in_hi) if rev else in_hi
            if m == 1:
                tt = jnp.where(late, pltpu.roll(p, c - 1 if rev else 1, axis=1), p)
            else:
                p4 = p.reshape(nh, c // sub, sub, LANES)
                srow = lax.broadcasted_iota(jnp.int32, (1, 1, sub, 1), 2)
                tt4 = None
                for blk in range(sub // (2 * m)):
                    bidx = blk * 2 * m + (m if rev else m - 1)
                    cand = jnp.broadcast_to(p4[:, :, bidx:bidx + 1, :], p4.shape)
                    tt4 = cand if tt4 is None else jnp.where(srow // (2 * m) == blk, cand, tt4)
                tt = tt4.reshape(nh, c, LANES)
            ee = jnp.exp2(jnp.where(late, p, tt - p))
            qt = jnp.where(late, q * ee, 0.0).astype(BF16)
            kt = jnp.where(late, 0.0, k * ee).astype(BF16)
            att = att + jnp.where(same, _bmm_nt(qt, kt), 0.0)
            p = jnp.where(late, p + tt, p)
        m *= 2
    tot = p[:, 0:1] if rev else p[:, c - 1:c]
    vb = v.astype(BF16)
    y = (_bmm(att.astype(BF16), vb) + jnp.sum(q * k, axis=2, keepdims=True) * v
         + _bmm_nt((q * jnp.exp2(p)).astype(BF16), st.astype(BF16)))
    st_new = st * jnp.exp2(tot) + _bmm_tn(vb, (k * jnp.exp2(tot - p)).astype(BF16))
    return y, st_new


def _hg_scan_kernel(*refs):
    ins = refs[:8]
    y_os = refs[8:10]
    st = refs[10]

    @pl.when(pl.program_id(1) == 0)
    def _():
        st[...] = jnp.zeros_like(st)

    def body(ci, carry):
        rows = _scan_rows(ci, HG_SCAN_CHUNKS)
        loaded = []
        for dr in range(2):
            q_r, v_r, kd_r, lf_r = ins[4 * dr:4 * dr + 4]
            rs = rows[dr]
            loaded.append((_load_chunk(q_r, (), rs), _load_chunk(kd_r, (0,), rs), _load_chunk(v_r, (), rs),
                           _load_chunk(lf_r, (0,), rs), st[dr]))
        results = [_hg_chunk(*loaded[dr], rev=(dr == 1)) for dr in range(2)]
        for dr in range(2):
            _store_chunk(y_os[dr], rows[dr], results[dr][0])
            st[dr] = results[dr][1]
        return carry

    lax.fori_loop(0, HG_SCAN_CHUNKS, body, 0)


def _hg_scan(q, v, lf, kd, *, n_lat_chunks, n_ctx_chunks):
    b, ng, tp, _ = q.shape
    grid, in_specs, out_specs, out_shape = _scan_specs(b, ng, tp, n_lat_chunks, n_ctx_chunks, 2, 2, 1,
                                                       HG_SCAN_CHUNKS)
    return pl.pallas_call(
        _hg_scan_kernel,
        grid=grid,
        in_specs=in_specs,
        out_specs=out_specs,
        out_shape=out_shape,
        scratch_shapes=[pltpu.VMEM((2, ng, LANES, LANES), F32)],
        name="hgrn2_scan",
        compiler_params=pltpu.CompilerParams(
            dimension_semantics=("parallel", "arbitrary"), vmem_limit_bytes=VMEM_LIMIT),
    )(q, v, kd, lf, q, v, kd, lf)


def _pad_rows(w, rows):
    return jnp.pad(w, ((0, rows - w.shape[0]), (0, 0)))


def _pad_cols(w, cols):
    return jnp.pad(w, ((0, 0), (0, cols - w.shape[1])))


def kernel(x, c, ctx, c_ctx, mod_w, mod_b, pre_g, post_g, rw_mix, rw_proj, rw_wo, rw_w0, rw_w1, rw_w2, rw_a0,
           rw_a1, rw_a2, rw_v0, rw_v1, rw_v2, rw_kk, rw_ka, rw_rk, rw_lnw, rw_lnb, hg_win, hg_wo, hg_gn, hg_lb):
    b, seq, d = x.shape
    ctx_len = ctx.shape[1]
    depth = mod_w.shape[0]
    di = rw_proj.shape[-1]
    assert seq % TILE == 0 and ctx_len % CHUNK == 0 and ctx_len <= TILE and seq % GRID_W == 0
    n_lat_tiles = seq // TILE
    tp = seq + TILE
    n_lat_chunks, n_ctx_chunks = seq // CHUNK, ctx_len // CHUNK

    xs = jnp.concatenate([x, ctx, jnp.zeros((b, tp - seq - ctx_len, d), x.dtype)], axis=1)

    rows = ((b + 1 + 7) // 8) * 8
    cond = jnp.concatenate([c, c_ctx[None], jnp.zeros((rows - b - 1, d), c.dtype)], axis=0)
    mod_all = _modulation(cond, mod_w, mod_b)
    mod_lat = jnp.broadcast_to(mod_all[:, :b, None, :], (depth, b, n_lat_tiles, 3 * d))
    mod_ctx = jnp.broadcast_to(mod_all[:, b:b + 1, None, :], (depth, b, 1, 3 * d))
    mod_t = jnp.concatenate([mod_lat, mod_ctx], axis=2)[:, :, :, None, :]

    p_lb = jax.nn.softmax(hg_lb.astype(F32), axis=0)
    lb_all = jnp.cumsum(p_lb, axis=0) - p_lb[0]

    vf = None
    for i in range(depth):
        ctx_out = i < depth - 1
        n_out_tiles = (seq + (ctx_len if ctx_out else 0)) // OTILE
        j = i // 2
        mod_i = mod_t[i]
        pg = pre_g[i][None]
        if i % 2 == 0:
            r64 = rw_w1.shape[-1]
            w2 = jnp.stack([jnp.pad(rw_w2[j, 0], ((0, LANES - r64), (0, 0))),
                            jnp.pad(rw_w2[j, 1], ((r64, LANES - 2 * r64), (0, 0)))])
            a2 = jnp.stack([jnp.pad(rw_a2[j, 0], ((0, LANES - r64), (0, 0))),
                            jnp.pad(rw_a2[j, 1], ((r64, LANES - 2 * r64), (0, 0)))])
            if j == 0:
                v0 = jnp.zeros((di,), F32)
                v1 = jnp.zeros((d, LANES), BF16)
                v2 = jnp.zeros((LANES, di), BF16)
            else:
                v0 = rw_v0[j - 1]
                v1 = _pad_cols(rw_v1[j - 1], LANES).astype(BF16)
                v2 = _pad_rows(rw_v2[j - 1], LANES).astype(BF16)
            p = {
                "mix": rw_mix[j],
                "proj": rw_proj[j].astype(BF16),
                "w1": jnp.concatenate([rw_w1[j, 0], rw_w1[j, 1]], axis=1).astype(BF16),
                "a1": jnp.concatenate([rw_a1[j, 0], rw_a1[j, 1]], axis=1).astype(BF16),
                "v1": v1,
                "w2": w2.astype(BF16),
                "a2": a2.astype(BF16),
                "v2": v2,
                "vec": jnp.stack([rw_w0[j, 0], rw_w0[j, 1], rw_a0[j, 0], rw_a0[j, 1], v0, rw_kk[j], rw_ka[j],
                                  rw_rk[j]]),
            }
            r, v, kk, z, bv, lw, bb, kd = _rw_features(xs, mod_i, pg, p, vf if j > 0 else None,
                                                       n_lat_tiles=n_lat_tiles, ctx_len=ctx_len)
            if j == 0:
                vf = v
            yf, yb = _rw_scan((r, v, kk, lw, bb, kd), n_lat_chunks=n_lat_chunks, n_ctx_chunks=n_ctx_chunks)
            gspec = _out_group_spec(di)
            extras = [z, bv, rw_lnw[j][None], rw_lnb[j][None], _head_sum_matrix(LANES, RW_HEAD)]
            especs = [gspec, gspec, pl.BlockSpec((1, di), lambda bi, jj: (0, 0)),
                      pl.BlockSpec((1, di), lambda bi, jj: (0, 0)), pl.BlockSpec((LANES, LANES), lambda bi, jj: (0, 0))]
            xs = _mixer_out("rwkv7_out", _rw_out_kernel, (yf, yb), extras, especs, rw_wo[j].astype(BF16), xs, mod_i,
                            post_g[i][None], n_out_tiles)
        else:
            w5 = hg_win[j].reshape(d, 5, di).transpose(1, 0, 2).astype(BF16)
            q, iv, sg, lf, kd = _hg_features(xs, mod_i, pg, w5, lb_all[i][None])
            yf, yb = _hg_scan(q, iv, lf, kd, n_lat_chunks=n_lat_chunks, n_ctx_chunks=n_ctx_chunks)
            extras = [sg, hg_gn[j][None]]
            especs = [_out_group_spec(di), pl.BlockSpec((1, HG_HEAD), lambda bi, jj: (0, 0))]
            xs = _mixer_out("hgrn2_out", _hg_out_kernel, (yf, yb), extras, especs, hg_wo[j].astype(BF16), xs, mod_i,
                            post_g[i][None], n_out_tiles)
    return xs[:, :seq]
```

```python
import functools
import math

import jax
import jax.numpy as jnp
from jax import lax
from jax.experimental import pallas as pl
from jax.experimental.pallas import tpu as pltpu

F32 = jnp.float32
BF16 = jnp.bfloat16
HI = lax.Precision.HIGHEST

GRID_W = 64
RW_HEAD = 64
HG_HEAD = 128
CHUNK = 64
TILE = 512
OTILE = 256
RW_SCAN_CHUNKS = 4
RW_SCAN_BATCH = 1
HG_SCAN_CHUNKS = 4
NORM_EPS = 1e-6
LN_X_EPS = 64e-5
LANES = 128
VMEM_LIMIT = 56 * 1024 * 1024


def _silu(x):
    return x * jax.nn.sigmoid(x)


def _dot(a, b):
    return jnp.dot(a, b, preferred_element_type=F32)


def _mod_kernel(s_ref, w_ref, b_ref, o_ref):
    s = _silu(s_ref[...])
    o_ref[0] = jnp.dot(s, w_ref[0], precision=HI, preferred_element_type=F32) + b_ref[0]


def _modulation(cond, mod_w, mod_b):
    depth, d, d3 = mod_w.shape
    rows = cond.shape[0]
    nb = d3 // d
    return pl.pallas_call(
        _mod_kernel,
        grid=(depth, nb),
        in_specs=[
            pl.BlockSpec((rows, d), lambda i, n: (0, 0)),
            pl.BlockSpec((1, d, d), lambda i, n: (i, 0, n)),
            pl.BlockSpec((1, 1, d), lambda i, n: (i, 0, n)),
        ],
        out_specs=pl.BlockSpec((1, rows, d), lambda i, n: (i, 0, n)),
        out_shape=jax.ShapeDtypeStruct((depth, rows, d3), F32),
        name="adaln_modulation",
        compiler_params=pltpu.CompilerParams(vmem_limit_bytes=VMEM_LIMIT),
    )(cond, mod_w, mod_b.reshape(depth, 1, d3))


def _adaln(xb, g, scale, shift):
    ms = jnp.mean(xb * xb, axis=-1, keepdims=True)
    return xb * lax.rsqrt(ms + NORM_EPS) * g * (1.0 + scale) + shift


def _store_groups(o_ref, lead, val):
    for gq in range(val.shape[1] // LANES):
        o_ref[lead + (gq,)] = val[:, gq * LANES:(gq + 1) * LANES].astype(o_ref.dtype)


def _load_groups(ref, lead):
    n = ref.shape[len(lead)]
    return jnp.concatenate([ref[lead + (gq,)] for gq in range(n)], axis=1)


def _head_sum_matrix(width, head):
    r = lax.broadcasted_iota(jnp.int32, (width, width), 0) // head
    c = lax.broadcasted_iota(jnp.int32, (width, width), 1) // head
    return (r == c).astype(BF16)


def _rw_feat_kernel(n_lat_tiles, ctx_len, has_vres, *refs):
    (x_ref, xp_ref, xn_ref, mod_ref, g_ref, mix_ref, w1_ref, a1_ref, v1_ref,
     proj_ref, w2_ref, a2_ref, v2_ref, vec_ref, hs_ref) = refs[:15]
    pos = 15
    vf_ref = None
    if has_vres:
        vf_ref = refs[pos]
        pos += 1
    (r_o, v_o, kk_o, z_o, bv_o, lw_o, bb_o, kd_o) = refs[pos:pos + 8]
    hbuf, xm, tw, ta, tv = refs[pos + 8:]

    j = pl.program_id(1)
    n = pl.program_id(2)
    d = x_ref.shape[-1]
    q4 = d // 4
    nblk = TILE // GRID_W

    @pl.when(n == 0)
    def _prologue():
        m = mod_ref[0, 0]
        shift, scale = m[:, :d], m[:, d:2 * d]
        g = g_ref[...]
        is_lat = j < n_lat_tiles
        prev_ok = jnp.logical_and(j >= 1, is_lat)
        next_ok = j < n_lat_tiles - 1

        def norm_rows(rb, c):
            r0 = pl.multiple_of(rb * GRID_W, GRID_W)
            hbuf[pl.ds(r0 + GRID_W, GRID_W), :] = _adaln(x_ref[0, pl.ds(r0, GRID_W), :], g, scale, shift)
            return c

        lax.fori_loop(0, nblk, norm_rows, 0)
        hp = _adaln(xp_ref[0], g, scale, shift)
        hbuf[0:GRID_W, :] = jnp.where(prev_ok, hp, 0.0)
        hn = _adaln(xn_ref[0], g, scale, shift)
        hbuf[TILE + GRID_W:TILE + 2 * GRID_W, :] = jnp.where(next_ok, hn, 0.0)

        row = lax.broadcasted_iota(jnp.int32, (GRID_W, 1), 0)

        def store_mixed(r0, cs, h_c, hs_c):
            dd = hs_c - h_c
            for k in range(6):
                xm[k, pl.ds(r0, GRID_W), cs] = (h_c + dd * mix_ref[k:k + 1, cs]).astype(BF16)

        def mix_lat(rb, c):
            r0 = pl.multiple_of(rb * GRID_W, GRID_W)
            for qi in range(4):
                cs = slice(qi * q4, (qi + 1) * q4)
                h_c = hbuf[pl.ds(r0 + GRID_W, GRID_W), cs]
                if qi == 0:
                    hs_c = jnp.where(row == 0, 0.0, pltpu.roll(h_c, 1, axis=0))
                elif qi == 1:
                    hs_c = jnp.where(row == GRID_W - 1, 0.0, pltpu.roll(h_c, GRID_W - 1, axis=0))
                elif qi == 2:
                    hs_c = hbuf[pl.ds(r0, GRID_W), cs]
                else:
                    hs_c = hbuf[pl.ds(r0 + 2 * GRID_W, GRID_W), cs]
                store_mixed(r0, cs, h_c, hs_c)
            return c

        def mix_ctx(rb, c):
            r0 = pl.multiple_of(rb * GRID_W, GRID_W)
            t = row + rb * GRID_W
            for qi in range(4):
                cs = slice(qi * q4, (qi + 1) * q4)
                h_c = hbuf[pl.ds(r0 + GRID_W, GRID_W), cs]
                if qi < 2:
                    edge = hbuf[pl.ds(r0 + GRID_W - 1, 1), cs]
                    hs_c = jnp.where(row == 0, edge, pltpu.roll(h_c, 1, axis=0))
                else:
                    edge = hbuf[pl.ds(r0 + 2 * GRID_W, 1), cs]
                    hs_c = jnp.where(row == GRID_W - 1, edge, pltpu.roll(h_c, GRID_W - 1, axis=0))
                    hs_c = jnp.where(t == ctx_len - 1, 0.0, hs_c)
                store_mixed(r0, cs, h_c, hs_c)
            return c

        @pl.when(is_lat)
        def _():
            lax.fori_loop(0, nblk, mix_lat, 0)

        @pl.when(jnp.logical_not(is_lat))
        def _():
            lax.fori_loop(0, nblk, mix_ctx, 0)

        tw[...] = jnp.tanh(_dot(xm[1], w1_ref[...])).astype(BF16)
        ta[...] = _dot(xm[4], a1_ref[...]).astype(BF16)
        if has_vres:
            tv[...] = _dot(xm[3], v1_ref[...]).astype(BF16)

    vec = vec_ref[...]
    r = _dot(xm[0], proj_ref[0, 0])
    k = _dot(xm[2], proj_ref[0, 1])
    v = _dot(xm[3], proj_ref[0, 2])
    z = _silu(_dot(xm[5], proj_ref[0, 3]))
    if has_vres:
        gate = jax.nn.sigmoid(vec[4:5] + _dot(tv[...], v2_ref[...]))
        v = v + (_load_groups(vf_ref, (0,)).astype(F32) - v) * gate
    hs = hs_ref[...]
    kk = k * vec[5:6]
    ss = _dot((kk * kk).astype(BF16), hs)
    kk = kk * lax.rsqrt(jnp.maximum(ss, 1e-24))
    decay_scale = -math.exp(-0.5) * math.log2(math.e)
    ksum = None
    for dr in range(2):
        u = vec[dr:dr + 1] + _dot(tw[...], w2_ref[dr])
        _store_groups(lw_o, (dr, 0), decay_scale * jax.nn.sigmoid(u))
        a = jax.nn.sigmoid(vec[2 + dr:3 + dr] + _dot(ta[...], a2_ref[dr]))
        kd = k * (1.0 + (a - 1.0) * vec[6:7])
        _store_groups(bb_o, (dr, 0), kk * a)
        _store_groups(kd_o, (dr, 0), kd)
        ksum = kd if ksum is None else ksum + kd
    bonus = _dot((r * ksum * vec[7:8]).astype(BF16), hs)
    _store_groups(r_o, (0,), r)
    _store_groups(v_o, (0,), v)
    _store_groups(kk_o, (0,), kk)
    _store_groups(z_o, (0,), z)
    _store_groups(bv_o, (0,), bonus * v)


def _group_major_specs(b, tp, di, tn):
    gq = tn // LANES
    one = pl.BlockSpec((1, gq, TILE, LANES), lambda bi, j, n: (bi, n, j, 0))
    two = pl.BlockSpec((2, 1, gq, TILE, LANES), lambda bi, j, n: (0, bi, n, j, 0))
    s1 = lambda dt: jax.ShapeDtypeStruct((b, di // LANES, tp, LANES), dt)
    s2 = lambda dt: jax.ShapeDtypeStruct((2, b, di // LANES, tp, LANES), dt)
    return one, two, s1, s2


def _column_blocks(w, tn):
    k, d, n = w.shape
    return w.reshape(k, d, n // tn, tn).transpose(2, 0, 1, 3)


def _rw_features(xs, mod, pre_g, p, vf, *, n_lat_tiles, ctx_len, tn=256):
    b, tp, d = xs.shape
    di = p["proj"].shape[-1]
    nt = di // tn
    ntiles = tp // TILE
    nb64 = tp // GRID_W
    per = TILE // GRID_W
    has_vres = vf is not None

    def tok(bi, j, n):
        return (bi, j, 0)

    in_specs = [
        pl.BlockSpec((1, TILE, d), tok),
        pl.BlockSpec((1, GRID_W, d), lambda bi, j, n: (bi, jnp.maximum(j * per - 1, 0), 0)),
        pl.BlockSpec((1, GRID_W, d), lambda bi, j, n: (bi, jnp.minimum((j + 1) * per, nb64 - 1), 0)),
        pl.BlockSpec((1, 1, 1, mod.shape[-1]), lambda bi, j, n: (bi, j, 0, 0)),
        pl.BlockSpec((1, d), lambda bi, j, n: (0, 0)),
        pl.BlockSpec((6, d), lambda bi, j, n: (0, 0)),
        pl.BlockSpec((d, LANES), lambda bi, j, n: (0, 0)),
        pl.BlockSpec((d, LANES), lambda bi, j, n: (0, 0)),
        pl.BlockSpec((d, LANES), lambda bi, j, n: (0, 0)),
        pl.BlockSpec((1, 4, d, tn), lambda bi, j, n: (n, 0, 0, 0)),
        pl.BlockSpec((2, LANES, tn), lambda bi, j, n: (0, 0, n)),
        pl.BlockSpec((2, LANES, tn), lambda bi, j, n: (0, 0, n)),
        pl.BlockSpec((LANES, tn), lambda bi, j, n: (0, n)),
        pl.BlockSpec((8, tn), lambda bi, j, n: (0, n)),
        pl.BlockSpec((tn, tn), lambda bi, j, n: (0, 0)),
    ]
    args = [xs, xs, xs, mod, pre_g, p["mix"], p["w1"], p["a1"], p["v1"], _column_blocks(p["proj"], tn), p["w2"], p["a2"],
            p["v2"], p["vec"], _head_sum_matrix(tn, RW_HEAD)]
    one, two, s1, s2 = _group_major_specs(b, tp, di, tn)
    if has_vres:
        in_specs.append(one)
        args.append(vf)
    return pl.pallas_call(
        functools.partial(_rw_feat_kernel, n_lat_tiles, ctx_len, has_vres),
        grid=(b, ntiles, nt),
        in_specs=in_specs,
        out_specs=[one, one, one, one, one, two, two, two],
        out_shape=[s1(BF16), s1(BF16), s1(BF16), s1(BF16), s1(BF16), s2(F32), s2(BF16), s2(BF16)],
        scratch_shapes=[
            pltpu.VMEM((TILE + 2 * GRID_W, d), F32),
            pltpu.VMEM((6, TILE, d), BF16),
            pltpu.VMEM((TILE, LANES), BF16),
            pltpu.VMEM((TILE, LANES), BF16),
            pltpu.VMEM((TILE, LANES), BF16),
        ],
        name="rwkv7_features",
        compiler_params=pltpu.CompilerParams(
            dimension_semantics=("parallel", "parallel", "arbitrary"), vmem_limit_bytes=VMEM_LIMIT),
    )(*args)


def _bmm(a, b):
    return lax.dot_general(a, b, (((2,), (1,)), ((0,), (0,))), preferred_element_type=F32)


def _bmm_nt(a, b):
    return lax.dot_general(a, b, (((2,), (2,)), ((0,), (0,))), preferred_element_type=F32)


def _bmm_tn(a, b):
    return lax.dot_general(a, b, (((1,), (1,)), ((0,), (0,))), preferred_element_type=F32)


def _chunk_cumsum(x, rev):
    c = x.shape[1]
    row = lax.broadcasted_iota(jnp.int32, (1, c, 1), 1)
    s = 1
    while s < c:
        if rev:
            x = x + jnp.where(row < c - s, pltpu.roll(x, c - s, axis=1), 0.0)
        else:
            x = x + jnp.where(row >= s, pltpu.roll(x, s, axis=1), 0.0)
        s *= 2
    return x


def _scan_chunk_maps(n_lat_chunks, n_ctx_chunks):
    total = n_lat_chunks + n_ctx_chunks

    def fwd(s):
        return (s + n_lat_chunks) % total

    def bwd(s):
        return total - 1 - s

    return total, fwd, bwd


def _scan_specs(b, ng, tp, n_lat_chunks, n_ctx_chunks, n_one, n_two, nb, nchunks):
    assert n_lat_chunks % nchunks == 0 and n_ctx_chunks % nchunks == 0 and b % nb == 0
    rows = CHUNK * nchunks
    total, fwd, bwd = _scan_chunk_maps(n_lat_chunks // nchunks, n_ctx_chunks // nchunks)
    in_specs = []
    for dr, cm in enumerate((fwd, bwd)):
        one = pl.BlockSpec((nb, ng, rows, LANES), lambda bi, s, cm=cm: (bi, 0, cm(s), 0))
        two = pl.BlockSpec((1, nb, ng, rows, LANES), lambda bi, s, cm=cm, dr=dr: (dr, bi, 0, cm(s), 0))
        in_specs += [one] * n_one + [two] * n_two
    out_specs = [pl.BlockSpec((nb, ng, rows, LANES), lambda bi, s, cm=cm: (bi, 0, cm(s), 0)) for cm in (fwd, bwd)]
    out_shape = [jax.ShapeDtypeStruct((b, ng, tp, LANES), BF16)] * 2
    return (b // nb, total), in_specs, out_specs, out_shape


def _scan_rows(ci, nchunks):
    fwd = pl.ds(pl.multiple_of(ci * CHUNK, CHUNK), CHUNK)
    bwd = pl.ds(pl.multiple_of((nchunks - 1 - ci) * CHUNK, CHUNK), CHUNK)
    return fwd, bwd


def _load_chunk(ref, lead, rs):
    x = ref[lead + (slice(None), slice(None), rs)]
    return x.reshape((x.shape[0] * x.shape[1],) + x.shape[2:]).astype(F32)


def _rw_chunk(r, v, kk, bb, kd, lw, st, rev):
    c = CHUNK
    half = RW_HEAD
    ri = lax.broadcasted_iota(jnp.int32, (1, c, LANES), 1)
    ci = lax.broadcasted_iota(jnp.int32, (1, c, LANES), 2) % half
    head_a = lax.broadcasted_iota(jnp.int32, (1, 1, LANES), 2) < half

    def split_rows(x):
        zero = jnp.zeros_like(x)
        return jnp.concatenate([jnp.where(head_a, x, zero), jnp.where(head_a, zero, x)], axis=1)

    if rev:
        incl, strict = ci >= ri, ci > ri
    else:
        incl, strict = ci <= ri, ci < ri
    cum = _chunk_cumsum(lw, rev)
    tot = cum[:, 0:1] if rev else cum[:, c - 1:c]
    e_in = jnp.exp2(cum)
    e_prev = jnp.exp2(cum - lw)
    e_inv = jnp.exp2(-cum)
    e_out = jnp.exp2(tot - cum)
    rt = (r * e_in).astype(BF16)
    at = (-kk * e_prev).astype(BF16)
    bi = (bb * e_inv).astype(BF16)
    ki = (kd * e_inv).astype(BF16)
    bh = (bb * e_out).astype(BF16)
    kh = (kd * e_out).astype(BF16)
    vb = v.astype(BF16)
    sb = st.astype(BF16)
    lhs = jnp.concatenate([at, rt], axis=1)
    h0 = _bmm_nt(lhs, sb)
    m_b = _bmm_nt(lhs, split_rows(bi))
    m_k = _bmm_nt(lhs, split_rows(ki))
    tri = jnp.concatenate([strict, incl], axis=1)
    m_b = m_b.astype(BF16)
    m_b = jnp.where(tri, m_b, jnp.zeros_like(m_b))
    m_k = m_k.astype(BF16)
    m_k = jnp.where(tri, m_k, jnp.zeros_like(m_k))
    ab16, a_rb = m_b[:, :c], m_b[:, c:]
    zero16 = jnp.zeros_like(ab16)
    pmb = jnp.where(ri == ci, jnp.ones_like(ab16), jnp.where(ri // 2 == ci // 2, ab16, zero16))
    m = 2
    while m < c:
        off = jnp.logical_and(ri // (2 * m) == ci // (2 * m), ri // m != ci // m)
        a_off = jnp.where(off, ab16, zero16)
        delta = _bmm(_bmm(pmb, split_rows(a_off)).astype(BF16), split_rows(pmb))
        pmb = jnp.where(off, delta.astype(BF16), pmb)
        m *= 2
    av = _bmm(m_k, split_rows(vb))
    x = h0[:, :c] + av[:, :c]
    u = _bmm(pmb, split_rows(x.astype(BF16)))
    ub = u.astype(BF16)
    o_full = h0[:, c:] + av[:, c:] + _bmm(a_rb, split_rows(ub))
    upd = _bmm_tn(jnp.concatenate([ub, vb], axis=1), jnp.concatenate([bh, kh], axis=1))
    blk_r = lax.broadcasted_iota(jnp.int32, (1, LANES, LANES), 1) // half
    blk_c = lax.broadcasted_iota(jnp.int32, (1, LANES, LANES), 2) // half
    st_new = st * jnp.exp2(tot) + jnp.where(blk_r == blk_c, upd, 0.0)
    return o_full, st_new


def _store_chunk(o_ref, rs, val):
    nb, ng = o_ref.shape[0], o_ref.shape[1]
    o_ref[:, :, rs] = val.reshape((nb, ng) + val.shape[1:]).astype(o_ref.dtype)


def _rw_scan_kernel(*refs):
    ins = refs[:12]
    y_os = refs[12:14]
    st = refs[14]

    @pl.when(pl.program_id(1) == 0)
    def _():
        st[...] = jnp.zeros_like(st)

    def body(ci, carry):
        rows = _scan_rows(ci, RW_SCAN_CHUNKS)
        loaded = []
        for dr in range(2):
            r_r, v_r, kk_r, bb_r, kd_r, lw_r = ins[6 * dr:6 * dr + 6]
            rs = rows[dr]
            loaded.append((_load_chunk(r_r, (), rs), _load_chunk(v_r, (), rs), _load_chunk(kk_r, (), rs),
                           _load_chunk(bb_r, (0,), rs), _load_chunk(kd_r, (0,), rs), _load_chunk(lw_r, (0,), rs),
                           st[dr]))
        results = [_rw_chunk(*loaded[dr], rev=(dr == 1)) for dr in range(2)]
        for dr in range(2):
            _store_chunk(y_os[dr], rows[dr], results[dr][0])
            st[dr] = results[dr][1]
        return carry

    lax.fori_loop(0, RW_SCAN_CHUNKS, body, 0)


def _rw_scan(feats, *, n_lat_chunks, n_ctx_chunks):
    r, v, kk, lw, bb, kd = feats
    b, ng, tp, _ = r.shape
    nb = RW_SCAN_BATCH if b % RW_SCAN_BATCH == 0 else 1
    grid, in_specs, out_specs, out_shape = _scan_specs(b, ng, tp, n_lat_chunks, n_ctx_chunks, 3, 3, nb,
                                                       RW_SCAN_CHUNKS)
    return pl.pallas_call(
        _rw_scan_kernel,
        grid=grid,
        in_specs=in_specs,
        out_specs=out_specs,
        out_shape=out_shape,
        scratch_shapes=[pltpu.VMEM((2, nb * ng, LANES, LANES), F32)],
        name="rwkv7_scan",
        compiler_params=pltpu.CompilerParams(
            dimension_semantics=("parallel", "arbitrary"), vmem_limit_bytes=VMEM_LIMIT),
    )(r, v, kk, bb, kd, lw, r, v, kk, bb, kd, lw)


def _residual_tail(yg_bf16, wo_ref, x_ref, mod_ref, pg_ref, o_ref):
    d = x_ref.shape[-1]
    o = _dot(yg_bf16, wo_ref[...])
    ms = jnp.mean(o * o, axis=-1, keepdims=True)
    o = o * lax.rsqrt(ms + NORM_EPS) * pg_ref[...]
    gate = mod_ref[0, 0][:, 2 * d:]
    o_ref[0] = x_ref[0] + gate * o


def _rw_out_kernel(yf_ref, yb_ref, z_ref, bv_ref, lnw_ref, lnb_ref, hs_ref, wo_ref, x_ref, mod_ref, pg_ref,
                   o_ref, yg):
    inv = 1.0 / RW_HEAD
    hs = hs_ref[...]
    for gq in range(yf_ref.shape[1]):
        cs = slice(gq * LANES, (gq + 1) * LANES)
        y = yf_ref[0, gq].astype(F32) + yb_ref[0, gq].astype(F32)
        mu = _dot(y.astype(BF16), hs) * inv
        yc = y - mu
        var = _dot((yc * yc).astype(BF16), hs) * inv
        yn = yc * lax.rsqrt(var + LN_X_EPS) * lnw_ref[:, cs] + lnb_ref[:, cs] + bv_ref[0, gq].astype(F32)
        yg[:, cs] = yn.astype(BF16) * z_ref[0, gq]
    _residual_tail(yg[...], wo_ref, x_ref, mod_ref, pg_ref, o_ref)


def _hg_out_kernel(yf_ref, yb_ref, sg_ref, gn_ref, wo_ref, x_ref, mod_ref, pg_ref, o_ref, yg):
    for hx in range(yf_ref.shape[1]):
        cs = slice(hx * LANES, (hx + 1) * LANES)
        y = yf_ref[0, hx].astype(F32) + yb_ref[0, hx].astype(F32)
        ms = jnp.mean(y * y, axis=-1, keepdims=True)
        yn = y * lax.rsqrt(ms + NORM_EPS) * gn_ref[...]
        yg[:, cs] = yn.astype(BF16) * sg_ref[0, hx]
    _residual_tail(yg[...], wo_ref, x_ref, mod_ref, pg_ref, o_ref)


def _out_group_spec(di):
    return pl.BlockSpec((1, di // LANES, OTILE, LANES), lambda bi, j: (bi, 0, j, 0))


def _mixer_out(name, kernel_fn, ys, extras, extra_specs, wo, xs, mod, post_g, n_tiles, out_rows):
    b, tp, d = xs.shape
    in_place = out_rows == tp
    assert in_place or out_rows == n_tiles * OTILE
    di = wo.shape[0]
    per = TILE // OTILE
    tokd = pl.BlockSpec((1, OTILE, d), lambda bi, j: (bi, j, 0))
    in_specs = [_out_group_spec(di)] * 2 + extra_specs + [
        pl.BlockSpec((di, d), lambda bi, j: (0, 0)),
        tokd,
        pl.BlockSpec((1, 1, 1, mod.shape[-1]), lambda bi, j: (bi, j // per, 0, 0)),
        pl.BlockSpec((1, d), lambda bi, j: (0, 0)),
    ]
    n_in = len(in_specs)
    return pl.pallas_call(
        kernel_fn,
        grid=(b, n_tiles),
        in_specs=in_specs,
        out_specs=tokd,
        out_shape=jax.ShapeDtypeStruct((b, out_rows, d), F32),
        scratch_shapes=[pltpu.VMEM((OTILE, di), BF16)],
        input_output_aliases={n_in - 3: 0} if in_place else {},
        name=name,
        compiler_params=pltpu.CompilerParams(
            dimension_semantics=("parallel", "parallel"), vmem_limit_bytes=VMEM_LIMIT),
    )(*ys, *extras, wo, xs, mod, post_g)


def _hg_feat_kernel(x_ref, mod_ref, g_ref, w_ref, lb_ref, q_o, i_o, sg_o, lf_o, kd_o, hbf):
    n = pl.program_id(1)
    j = pl.program_id(2)
    d = x_ref.shape[-1]

    @pl.when(n == 0)
    def _prologue():
        m = mod_ref[0, 0]
        shift, scale = m[:, :d], m[:, d:2 * d]
        g = g_ref[...]

        def norm_rows(rb, c):
            r0 = pl.multiple_of(rb * GRID_W, GRID_W)
            hbf[j, pl.ds(r0, GRID_W), :] = _adaln(x_ref[0, pl.ds(r0, GRID_W), :], g, scale, shift).astype(BF16)
            return c

        lax.fori_loop(0, TILE // GRID_W, norm_rows, 0)

    h = hbf[j]
    lb = lb_ref[...]
    _store_groups(q_o, (0,), _silu(_dot(h, w_ref[0, 0])))
    for dr in range(2):
        f = lb + (1.0 - lb) * jax.nn.sigmoid(_dot(h, w_ref[0, 1 + dr]))
        _store_groups(kd_o, (dr, 0), 1.0 - f)
        _store_groups(lf_o, (dr, 0), jnp.log2(f))
    _store_groups(i_o, (0,), _dot(h, w_ref[0, 3]))
    _store_groups(sg_o, (0,), _silu(_dot(h, w_ref[0, 4])))


def _hg_features(xs, mod, pre_g, w_in5, lb, *, tn=256):
    b, tp, d = xs.shape
    di = w_in5.shape[-1]
    nt = di // tn
    ntiles = tp // TILE
    _, _, s1, s2 = _group_major_specs(b, tp, di, tn)
    gq = tn // LANES
    one = pl.BlockSpec((1, gq, TILE, LANES), lambda bi, n, j: (bi, n, j, 0))
    two = pl.BlockSpec((2, 1, gq, TILE, LANES), lambda bi, n, j: (0, bi, n, j, 0))

    def tok_tile(n, j):
        return jnp.where(n == 0, j, ntiles - 1)

    return pl.pallas_call(
        _hg_feat_kernel,
        grid=(b, nt, ntiles),
        in_specs=[
            pl.BlockSpec((1, TILE, d), lambda bi, n, j: (bi, tok_tile(n, j), 0)),
            pl.BlockSpec((1, 1, 1, mod.shape[-1]), lambda bi, n, j: (bi, tok_tile(n, j), 0, 0)),
            pl.BlockSpec((1, d), lambda bi, n, j: (0, 0)),
            pl.BlockSpec((1, 5, d, tn), lambda bi, n, j: (n, 0, 0, 0)),
            pl.BlockSpec((1, tn), lambda bi, n, j: (0, n)),
        ],
        out_specs=[one, one, one, two, two],
        out_shape=[s1(BF16), s1(BF16), s1(BF16), s2(F32), s2(BF16)],
        scratch_shapes=[pltpu.VMEM((ntiles, TILE, d), BF16)],
        name="hgrn2_features",
        compiler_params=pltpu.CompilerParams(
            dimension_semantics=("parallel", "arbitrary", "arbitrary"), vmem_limit_bytes=VMEM_LIMIT),
    )(xs, mod, pre_g, _column_blocks(w_in5, tn), lb)


def _hg_chunk(q, k, v, lf2, st, rev):
    c = CHUNK
    sub = 8
    nh = q.shape[0]
    row = lax.broadcasted_iota(jnp.int32, (1, c, 1), 1)
    ri = lax.broadcasted_iota(jnp.int32, (1, c, c), 1)
    ci = lax.broadcasted_iota(jnp.int32, (1, c, c), 2)
    p = lf2
    att = jnp.zeros((nh, c, c), F32)
    m = 1
    while m < c:
        same = (ri // (2 * m)) == (ci // (2 * m))
        if m >= sub:
            zeros = jnp.zeros((nh, m, LANES), BF16)
            q_parts, k_parts, p_parts = [], [], []
            for blk in range(c // (2 * m)):
                lo = slice(blk * 2 * m, blk * 2 * m + m)
                hi = slice(blk * 2 * m + m, (blk + 1) * 2 * m)
                early, later = (hi, lo) if rev else (lo, hi)
                bidx = blk * 2 * m + (m if rev else m - 1)
                tt = p[:, bidx:bidx + 1, :]
                q_l = (q[:, later] * jnp.exp2(p[:, later])).astype(BF16)
                k_e = (k[:, early] * jnp.exp2(tt - p[:, early])).astype(BF16)
                if rev:
                    q_parts += [q_l, zeros]
                    k_parts += [zeros, k_e]
                    p_parts += [p[:, later] + tt, p[:, early]]
                else:
                    q_parts += [zeros, q_l]
                    k_parts += [k_e, zeros]
                    p_parts += [p[:, early], p[:, later] + tt]
            mm = _bmm_nt(jnp.concatenate(q_parts, axis=1), jnp.concatenate(k_parts, axis=1))
            att = att + (mm if 2 * m == c else jnp.where(same, mm, 0.0))
            p = jnp.concatenate(p_parts, axis=1)
        else:
            in_hi = ((row // m) % 2) == 1
            late = jnp.logical_not(in_hi) if rev else in_hi
            if m == 1:
                tt = jnp.where(late, pltpu.roll(p, c - 1 if rev else 1, axis=1), p)
            else:
                p4 = p.reshape(nh, c // sub, sub, LANES)
                srow = lax.broadcasted_iota(jnp.int32, (1, 1, sub, 1), 2)
                tt4 = None
                for blk in range(sub // (2 * m)):
                    bidx = blk * 2 * m + (m if rev else m - 1)
                    cand = jnp.broadcast_to(p4[:, :, bidx:bidx + 1, :], p4.shape)
                    tt4 = cand if tt4 is None else jnp.where(srow // (2 * m) == blk, cand, tt4)
                tt = tt4.reshape(nh, c, LANES)
            ee = jnp.exp2(jnp.where(late, p, tt - p))
            qt = jnp.where(late, q * ee, 0.0).astype(BF16)
            kt = jnp.where(late, 0.0, k * ee).astype(BF16)
            att = att + jnp.where(same, _bmm_nt(qt, kt), 0.0)
            p = jnp.where(late, p + tt, p)
        m *= 2
    tot = p[:, 0:1] if rev else p[:, c - 1:c]
    vb = v.astype(BF16)
    y = (_bmm(att.astype(BF16), vb) + jnp.sum(q * k, axis=2, keepdims=True) * v
         + _bmm_nt((q * jnp.exp2(p)).astype(BF16), st.astype(BF16)))
    st_new = st * jnp.exp2(tot) + _bmm_tn(vb, (k * jnp.exp2(tot - p)).astype(BF16))
    return y, st_new


def _hg_scan_kernel(*refs):
    ins = refs[:8]
    y_os = refs[8:10]
    st = refs[10]

    @pl.when(pl.program_id(1) == 0)
    def _():
        st[...] = jnp.zeros_like(st)

    def body(ci, carry):
        rows = _scan_rows(ci, HG_SCAN_CHUNKS)
        loaded = []
        for dr in range(2):
            q_r, v_r, kd_r, lf_r = ins[4 * dr:4 * dr + 4]
            rs = rows[dr]
            loaded.append((_load_chunk(q_r, (), rs), _load_chunk(kd_r, (0,), rs), _load_chunk(v_r, (), rs),
                           _load_chunk(lf_r, (0,), rs), st[dr]))
        results = [_hg_chunk(*loaded[dr], rev=(dr == 1)) for dr in range(2)]
        for dr in range(2):
            _store_chunk(y_os[dr], rows[dr], results[dr][0])
            st[dr] = results[dr][1]
        return carry

    lax.fori_loop(0, HG_SCAN_CHUNKS, body, 0)


def _hg_scan(q, v, lf, kd, *, n_lat_chunks, n_ctx_chunks):
    b, ng, tp, _ = q.shape
    grid, in_specs, out_specs, out_shape = _scan_specs(b, ng, tp, n_lat_chunks, n_ctx_chunks, 2, 2, 1,
                                                       HG_SCAN_CHUNKS)
    return pl.pallas_call(
        _hg_scan_kernel,
        grid=grid,
        in_specs=in_specs,
        out_specs=out_specs,
        out_shape=out_shape,
        scratch_shapes=[pltpu.VMEM((2, ng, LANES, LANES), F32)],
        name="hgrn2_scan",
        compiler_params=pltpu.CompilerParams(
            dimension_semantics=("parallel", "arbitrary"), vmem_limit_bytes=VMEM_LIMIT),
    )(q, v, kd, lf, q, v, kd, lf)


def _pad_rows(w, rows):
    return jnp.pad(w, ((0, rows - w.shape[0]), (0, 0)))


def _pad_cols(w, cols):
    return jnp.pad(w, ((0, 0), (0, cols - w.shape[1])))


def kernel(x, c, ctx, c_ctx, mod_w, mod_b, pre_g, post_g, rw_mix, rw_proj, rw_wo, rw_w0, rw_w1, rw_w2, rw_a0,
           rw_a1, rw_a2, rw_v0, rw_v1, rw_v2, rw_kk, rw_ka, rw_rk, rw_lnw, rw_lnb, hg_win, hg_wo, hg_gn, hg_lb):
    b, seq, d = x.shape
    ctx_len = ctx.shape[1]
    depth = mod_w.shape[0]
    di = rw_proj.shape[-1]
    assert seq % TILE == 0 and ctx_len % CHUNK == 0 and ctx_len <= TILE and seq % GRID_W == 0
    n_lat_tiles = seq // TILE
    tp = seq + TILE
    n_lat_chunks, n_ctx_chunks = seq // CHUNK, ctx_len // CHUNK

    xs = jnp.concatenate([x, ctx, jnp.zeros((b, tp - seq - ctx_len, d), x.dtype)], axis=1)

    rows = ((b + 1 + 7) // 8) * 8
    cond = jnp.concatenate([c, c_ctx[None], jnp.zeros((rows - b - 1, d), c.dtype)], axis=0)
    mod_all = _modulation(cond, mod_w, mod_b)
    mod_lat = jnp.broadcast_to(mod_all[:, :b, None, :], (depth, b, n_lat_tiles, 3 * d))
    mod_ctx = jnp.broadcast_to(mod_all[:, b:b + 1, None, :], (depth, b, 1, 3 * d))
    mod_t = jnp.concatenate([mod_lat, mod_ctx], axis=2)[:, :, :, None, :]

    p_lb = jax.nn.softmax(hg_lb.astype(F32), axis=0)
    lb_all = jnp.cumsum(p_lb, axis=0) - p_lb[0]

    vf = None
    for i in range(depth):
        ctx_out = i < depth - 1
        n_out_tiles = (seq + (ctx_len if ctx_out else 0)) // OTILE
        out_rows = tp if ctx_out else seq
        j = i // 2
        mod_i = mod_t[i]
        pg = pre_g[i][None]
        if i % 2 == 0:
            r64 = rw_w1.shape[-1]
            w2 = jnp.stack([jnp.pad(rw_w2[j, 0], ((0, LANES - r64), (0, 0))),
                            jnp.pad(rw_w2[j, 1], ((r64, LANES - 2 * r64), (0, 0)))])
            a2 = jnp.stack([jnp.pad(rw_a2[j, 0], ((0, LANES - r64), (0, 0))),
                            jnp.pad(rw_a2[j, 1], ((r64, LANES - 2 * r64), (0, 0)))])
            if j == 0:
                v0 = jnp.zeros((di,), F32)
                v1 = jnp.zeros((d, LANES), BF16)
                v2 = jnp.zeros((LANES, di), BF16)
            else:
                v0 = rw_v0[j - 1]
                v1 = _pad_cols(rw_v1[j - 1], LANES).astype(BF16)
                v2 = _pad_rows(rw_v2[j - 1], LANES).astype(BF16)
            p = {
                "mix": rw_mix[j],
                "proj": rw_proj[j].astype(BF16),
                "w1": jnp.concatenate([rw_w1[j, 0], rw_w1[j, 1]], axis=1).astype(BF16),
                "a1": jnp.concatenate([rw_a1[j, 0], rw_a1[j, 1]], axis=1).astype(BF16),
                "v1": v1,
                "w2": w2.astype(BF16),
                "a2": a2.astype(BF16),
                "v2": v2,
                "vec": jnp.stack([rw_w0[j, 0], rw_w0[j, 1], rw_a0[j, 0], rw_a0[j, 1], v0, rw_kk[j], rw_ka[j],
                                  rw_rk[j]]),
            }
            r, v, kk, z, bv, lw, bb, kd = _rw_features(xs, mod_i, pg, p, vf if j > 0 else None,
                                                       n_lat_tiles=n_lat_tiles, ctx_len=ctx_len)
            if j == 0:
                vf = v
            yf, yb = _rw_scan((r, v, kk, lw, bb, kd), n_lat_chunks=n_lat_chunks, n_ctx_chunks=n_ctx_chunks)
            gspec = _out_group_spec(di)
            extras = [z, bv, rw_lnw[j][None], rw_lnb[j][None], _head_sum_matrix(LANES, RW_HEAD)]
            especs = [gspec, gspec, pl.BlockSpec((1, di), lambda bi, jj: (0, 0)),
                      pl.BlockSpec((1, di), lambda bi, jj: (0, 0)), pl.BlockSpec((LANES, LANES), lambda bi, jj: (0, 0))]
            xs = _mixer_out("rwkv7_out", _rw_out_kernel, (yf, yb), extras, especs, rw_wo[j].astype(BF16), xs, mod_i,
                            post_g[i][None], n_out_tiles, out_rows)
        else:
            w5 = hg_win[j].reshape(d, 5, di).transpose(1, 0, 2).astype(BF16)
            q, iv, sg, lf, kd = _hg_features(xs, mod_i, pg, w5, lb_all[i][None])
            yf, yb = _hg_scan(q, iv, lf, kd, n_lat_chunks=n_lat_chunks, n_ctx_chunks=n_ctx_chunks)
            extras = [sg, hg_gn[j][None]]
            especs = [_out_group_spec(di), pl.BlockSpec((1, HG_HEAD), lambda bi, jj: (0, 0))]
            xs = _mixer_out("hgrn2_out", _hg_out_kernel, (yf, yb), extras, especs, hg_wo[j].astype(BF16), xs, mod_i,
                            post_g[i][None], n_out_tiles, out_rows)
    return xs
```

```python
import functools
import math

import jax
import jax.numpy as jnp
from jax import lax
from jax.experimental import pallas as pl
from jax.experimental.pallas import tpu as pltpu

F32 = jnp.float32
BF16 = jnp.bfloat16
HI = lax.Precision.HIGHEST

GRID_W = 64
RW_HEAD = 64
HG_HEAD = 128
CHUNK = 64
TILE = 512
OTILE = 512
HG_TILES_PER_STEP = 3
RW_SCAN_CHUNKS = 4
RW_SCAN_BATCH = 1
HG_SCAN_CHUNKS = 4
NORM_EPS = 1e-6
LN_X_EPS = 64e-5
LANES = 128
VMEM_LIMIT = 56 * 1024 * 1024


def _silu(x):
    return x * jax.nn.sigmoid(x)


def _dot(a, b):
    return jnp.dot(a, b, preferred_element_type=F32)


def _mod_kernel(s_ref, w_ref, b_ref, o_ref):
    s = _silu(s_ref[...])
    o_ref[0] = jnp.dot(s, w_ref[0], precision=HI, preferred_element_type=F32) + b_ref[0]


def _modulation(cond, mod_w, mod_b):
    depth, d, d3 = mod_w.shape
    rows = cond.shape[0]
    nb = d3 // d
    return pl.pallas_call(
        _mod_kernel,
        grid=(depth, nb),
        in_specs=[
            pl.BlockSpec((rows, d), lambda i, n: (0, 0)),
            pl.BlockSpec((1, d, d), lambda i, n: (i, 0, n)),
            pl.BlockSpec((1, 1, d), lambda i, n: (i, 0, n)),
        ],
        out_specs=pl.BlockSpec((1, rows, d), lambda i, n: (i, 0, n)),
        out_shape=jax.ShapeDtypeStruct((depth, rows, d3), F32),
        name="adaln_modulation",
        compiler_params=pltpu.CompilerParams(vmem_limit_bytes=VMEM_LIMIT),
    )(cond, mod_w, mod_b.reshape(depth, 1, d3))


def _adaln(xb, g, scale, shift):
    ms = jnp.mean(xb * xb, axis=-1, keepdims=True)
    return xb * lax.rsqrt(ms + NORM_EPS) * g * (1.0 + scale) + shift


def _store_groups(o_ref, lead, val):
    for gq in range(val.shape[1] // LANES):
        o_ref[lead + (gq,)] = val[:, gq * LANES:(gq + 1) * LANES].astype(o_ref.dtype)


def _load_groups(ref, lead):
    n = ref.shape[len(lead)]
    return jnp.concatenate([ref[lead + (gq,)] for gq in range(n)], axis=1)


def _head_sum_matrix(width, head):
    r = lax.broadcasted_iota(jnp.int32, (width, width), 0) // head
    c = lax.broadcasted_iota(jnp.int32, (width, width), 1) // head
    return (r == c).astype(BF16)


def _rw_feat_kernel(n_lat_tiles, ctx_len, has_vres, *refs):
    (x_ref, xp_ref, xn_ref, mod_ref, g_ref, mix_ref, w1_ref, a1_ref, v1_ref,
     proj_ref, w2_ref, a2_ref, v2_ref, vec_ref, hs_ref) = refs[:15]
    pos = 15
    vf_ref = None
    if has_vres:
        vf_ref = refs[pos]
        pos += 1
    (r_o, v_o, kk_o, z_o, bv_o, lw_o, bb_o, kd_o) = refs[pos:pos + 8]
    hbuf, xm, tw, ta, tv = refs[pos + 8:]

    j = pl.program_id(1)
    n = pl.program_id(2)
    d = x_ref.shape[-1]
    q4 = d // 4
    nblk = TILE // GRID_W

    @pl.when(n == 0)
    def _prologue():
        m = mod_ref[0, 0]
        shift, scale = m[:, :d], m[:, d:2 * d]
        g = g_ref[...]
        is_lat = j < n_lat_tiles
        prev_ok = jnp.logical_and(j >= 1, is_lat)
        next_ok = j < n_lat_tiles - 1

        def norm_rows(rb, c):
            r0 = pl.multiple_of(rb * GRID_W, GRID_W)
            hbuf[pl.ds(r0 + GRID_W, GRID_W), :] = _adaln(x_ref[0, pl.ds(r0, GRID_W), :], g, scale, shift)
            return c

        lax.fori_loop(0, nblk, norm_rows, 0)
        hp = _adaln(xp_ref[0], g, scale, shift)
        hbuf[0:GRID_W, :] = jnp.where(prev_ok, hp, 0.0)
        hn = _adaln(xn_ref[0], g, scale, shift)
        hbuf[TILE + GRID_W:TILE + 2 * GRID_W, :] = jnp.where(next_ok, hn, 0.0)

        row = lax.broadcasted_iota(jnp.int32, (GRID_W, 1), 0)

        def store_mixed(r0, cs, h_c, hs_c):
            dd = hs_c - h_c
            for k in range(6):
                xm[k, pl.ds(r0, GRID_W), cs] = (h_c + dd * mix_ref[k:k + 1, cs]).astype(BF16)

        def mix_lat(rb, c):
            r0 = pl.multiple_of(rb * GRID_W, GRID_W)
            for qi in range(4):
                cs = slice(qi * q4, (qi + 1) * q4)
                h_c = hbuf[pl.ds(r0 + GRID_W, GRID_W), cs]
                if qi == 0:
                    hs_c = jnp.where(row == 0, 0.0, pltpu.roll(h_c, 1, axis=0))
                elif qi == 1:
                    hs_c = jnp.where(row == GRID_W - 1, 0.0, pltpu.roll(h_c, GRID_W - 1, axis=0))
                elif qi == 2:
                    hs_c = hbuf[pl.ds(r0, GRID_W), cs]
                else:
                    hs_c = hbuf[pl.ds(r0 + 2 * GRID_W, GRID_W), cs]
                store_mixed(r0, cs, h_c, hs_c)
            return c

        def mix_ctx(rb, c):
            r0 = pl.multiple_of(rb * GRID_W, GRID_W)
            t = row + rb * GRID_W
            for qi in range(4):
                cs = slice(qi * q4, (qi + 1) * q4)
                h_c = hbuf[pl.ds(r0 + GRID_W, GRID_W), cs]
                if qi < 2:
                    edge = hbuf[pl.ds(r0 + GRID_W - 1, 1), cs]
                    hs_c = jnp.where(row == 0, edge, pltpu.roll(h_c, 1, axis=0))
                else:
                    edge = hbuf[pl.ds(r0 + 2 * GRID_W, 1), cs]
                    hs_c = jnp.where(row == GRID_W - 1, edge, pltpu.roll(h_c, GRID_W - 1, axis=0))
                    hs_c = jnp.where(t == ctx_len - 1, 0.0, hs_c)
                store_mixed(r0, cs, h_c, hs_c)
            return c

        @pl.when(is_lat)
        def _():
            lax.fori_loop(0, nblk, mix_lat, 0)

        @pl.when(jnp.logical_not(is_lat))
        def _():
            lax.fori_loop(0, nblk, mix_ctx, 0)

        tw[...] = jnp.tanh(_dot(xm[1], w1_ref[...])).astype(BF16)
        ta[...] = _dot(xm[4], a1_ref[...]).astype(BF16)
        if has_vres:
            tv[...] = _dot(xm[3], v1_ref[...]).astype(BF16)

    vec = vec_ref[...]
    r = _dot(xm[0], proj_ref[0, 0])
    k = _dot(xm[2], proj_ref[0, 1])
    v = _dot(xm[3], proj_ref[0, 2])
    z = _silu(_dot(xm[5], proj_ref[0, 3]))
    if has_vres:
        gate = jax.nn.sigmoid(vec[4:5] + _dot(tv[...], v2_ref[...]))
        v = v + (_load_groups(vf_ref, (0,)).astype(F32) - v) * gate
    hs = hs_ref[...]
    kk = k * vec[5:6]
    ss = _dot((kk * kk).astype(BF16), hs)
    kk = kk * lax.rsqrt(jnp.maximum(ss, 1e-24))
    decay_scale = -math.exp(-0.5) * math.log2(math.e)
    ksum = None
    for dr in range(2):
        u = vec[dr:dr + 1] + _dot(tw[...], w2_ref[dr])
        _store_groups(lw_o, (dr, 0), decay_scale * jax.nn.sigmoid(u))
        a = jax.nn.sigmoid(vec[2 + dr:3 + dr] + _dot(ta[...], a2_ref[dr]))
        kd = k * (1.0 + (a - 1.0) * vec[6:7])
        _store_groups(bb_o, (dr, 0), kk * a)
        _store_groups(kd_o, (dr, 0), kd)
        ksum = kd if ksum is None else ksum + kd
    bonus = _dot((r * ksum * vec[7:8]).astype(BF16), hs)
    _store_groups(r_o, (0,), r)
    _store_groups(v_o, (0,), v)
    _store_groups(kk_o, (0,), kk)
    _store_groups(z_o, (0,), z)
    _store_groups(bv_o, (0,), bonus * v)


def _group_major_specs(b, tp, di, tn):
    gq = tn // LANES
    one = pl.BlockSpec((1, gq, TILE, LANES), lambda bi, j, n: (bi, n, j, 0))
    two = pl.BlockSpec((2, 1, gq, TILE, LANES), lambda bi, j, n: (0, bi, n, j, 0))
    s1 = lambda dt: jax.ShapeDtypeStruct((b, di // LANES, tp, LANES), dt)
    s2 = lambda dt: jax.ShapeDtypeStruct((2, b, di // LANES, tp, LANES), dt)
    return one, two, s1, s2


def _rw_features(xs, mod, pre_g, p, vf, *, n_lat_tiles, ctx_len, tn=256):
    b, tp, d = xs.shape
    di = p["proj"].shape[-1]
    nt = di // tn
    ntiles = tp // TILE
    nb64 = tp // GRID_W
    per = TILE // GRID_W
    has_vres = vf is not None

    def tok(bi, j, n):
        return (bi, j, 0)

    in_specs = [
        pl.BlockSpec((1, TILE, d), tok),
        pl.BlockSpec((1, GRID_W, d), lambda bi, j, n: (bi, jnp.maximum(j * per - 1, 0), 0)),
        pl.BlockSpec((1, GRID_W, d), lambda bi, j, n: (bi, jnp.minimum((j + 1) * per, nb64 - 1), 0)),
        pl.BlockSpec((1, 1, 1, mod.shape[-1]), lambda bi, j, n: (bi, j, 0, 0)),
        pl.BlockSpec((1, d), lambda bi, j, n: (0, 0)),
        pl.BlockSpec((6, d), lambda bi, j, n: (0, 0)),
        pl.BlockSpec((d, LANES), lambda bi, j, n: (0, 0)),
        pl.BlockSpec((d, LANES), lambda bi, j, n: (0, 0)),
        pl.BlockSpec((d, LANES), lambda bi, j, n: (0, 0)),
        pl.BlockSpec((1, 4, d, tn), lambda bi, j, n: (0, 0, 0, n)),
        pl.BlockSpec((2, LANES, tn), lambda bi, j, n: (0, 0, n)),
        pl.BlockSpec((2, LANES, tn), lambda bi, j, n: (0, 0, n)),
        pl.BlockSpec((LANES, tn), lambda bi, j, n: (0, n)),
        pl.BlockSpec((8, tn), lambda bi, j, n: (0, n)),
        pl.BlockSpec((tn, tn), lambda bi, j, n: (0, 0)),
    ]
    args = [xs, xs, xs, mod, pre_g, p["mix"], p["w1"], p["a1"], p["v1"], p["proj"][None], p["w2"], p["a2"],
            p["v2"], p["vec"], _head_sum_matrix(tn, RW_HEAD)]
    one, two, s1, s2 = _group_major_specs(b, tp, di, tn)
    if has_vres:
        in_specs.append(one)
        args.append(vf)
    return pl.pallas_call(
        functools.partial(_rw_feat_kernel, n_lat_tiles, ctx_len, has_vres),
        grid=(b, ntiles, nt),
        in_specs=in_specs,
        out_specs=[one, one, one, one, one, two, two, two],
        out_shape=[s1(BF16), s1(BF16), s1(BF16), s1(BF16), s1(BF16), s2(F32), s2(BF16), s2(BF16)],
        scratch_shapes=[
            pltpu.VMEM((TILE + 2 * GRID_W, d), F32),
            pltpu.VMEM((6, TILE, d), BF16),
            pltpu.VMEM((TILE, LANES), BF16),
            pltpu.VMEM((TILE, LANES), BF16),
            pltpu.VMEM((TILE, LANES), BF16),
        ],
        name="rwkv7_features",
        compiler_params=pltpu.CompilerParams(
            dimension_semantics=("parallel", "parallel", "arbitrary"), vmem_limit_bytes=VMEM_LIMIT),
    )(*args)


def _bmm(a, b):
    return lax.dot_general(a, b, (((2,), (1,)), ((0,), (0,))), preferred_element_type=F32)


def _bmm_nt(a, b):
    return lax.dot_general(a, b, (((2,), (2,)), ((0,), (0,))), preferred_element_type=F32)


def _bmm_tn(a, b):
    return lax.dot_general(a, b, (((1,), (1,)), ((0,), (0,))), preferred_element_type=F32)


def _chunk_cumsum(x, rev):
    c = x.shape[1]
    row = lax.broadcasted_iota(jnp.int32, (1, c, 1), 1)
    s = 1
    while s < c:
        if rev:
            x = x + jnp.where(row < c - s, pltpu.roll(x, c - s, axis=1), 0.0)
        else:
            x = x + jnp.where(row >= s, pltpu.roll(x, s, axis=1), 0.0)
        s *= 2
    return x


def _scan_chunk_maps(n_lat_chunks, n_ctx_chunks):
    total = n_lat_chunks + n_ctx_chunks

    def fwd(s):
        return jnp.where(s < total, (s + n_lat_chunks) % total, total)

    def bwd(s):
        return jnp.where(s < total, total - 1 - s, total)

    return total, fwd, bwd


def _scan_specs(b, ng, tp, n_lat_chunks, n_ctx_chunks, n_one, n_two, nb, nchunks):
    assert n_lat_chunks % nchunks == 0 and n_ctx_chunks % nchunks == 0 and b % nb == 0
    rows = CHUNK * nchunks
    total, fwd, bwd = _scan_chunk_maps(n_lat_chunks // nchunks, n_ctx_chunks // nchunks)
    assert tp == (total + 1) * rows
    in_specs = []
    for dr, cm in enumerate((fwd, bwd)):
        one = pl.BlockSpec((nb, ng, rows, LANES), lambda bi, s, cm=cm: (bi, 0, cm(s), 0))
        two = pl.BlockSpec((1, nb, ng, rows, LANES), lambda bi, s, cm=cm, dr=dr: (dr, bi, 0, cm(s), 0))
        in_specs += [one] * n_one + [two] * n_two
    out_specs = [pl.BlockSpec((nb, ng, rows, LANES), lambda bi, s, cm=cm: (bi, 0, cm(s), 0)) for cm in (fwd, bwd)]
    out_shape = [jax.ShapeDtypeStruct((b, ng, tp, LANES), BF16)] * 2
    return (b // nb, total + 1), in_specs, out_specs, out_shape


def _scan_rows(ci, nchunks):
    fwd = pl.ds(pl.multiple_of(ci * CHUNK, CHUNK), CHUNK)
    bwd = pl.ds(pl.multiple_of((nchunks - 1 - ci) * CHUNK, CHUNK), CHUNK)
    return fwd, bwd


def _scan_or_pad(body, nchunks, y_os):
    s = pl.program_id(1)
    last = pl.num_programs(1) - 1

    @pl.when(s < last)
    def _():
        lax.fori_loop(0, nchunks, body, 0)

    @pl.when(s == last)
    def _():
        for y_o in y_os:
            y_o[...] = jnp.zeros_like(y_o)


def _load_chunk(ref, lead, rs, dtype=F32):
    x = ref[lead + (slice(None), slice(None), rs)]
    return x.reshape((x.shape[0] * x.shape[1],) + x.shape[2:]).astype(dtype)


def _rw_chunk(r, v, kk, bb, kd, lw, st, rev):
    c = CHUNK
    half = RW_HEAD
    ri = lax.broadcasted_iota(jnp.int32, (1, c, LANES), 1)
    ci = lax.broadcasted_iota(jnp.int32, (1, c, LANES), 2) % half
    head_a = lax.broadcasted_iota(jnp.int32, (1, 1, LANES), 2) < half

    def split_rows(x):
        zero = jnp.zeros_like(x)
        return jnp.concatenate([jnp.where(head_a, x, zero), jnp.where(head_a, zero, x)], axis=1)

    if rev:
        incl, strict = ci >= ri, ci > ri
    else:
        incl, strict = ci <= ri, ci < ri
    cum = _chunk_cumsum(lw, rev)
    tot = cum[:, 0:1] if rev else cum[:, c - 1:c]
    e_in = jnp.exp2(cum)
    e_prev = jnp.exp2(cum - lw)
    e_inv = jnp.exp2(-cum)
    e_out = jnp.exp2(tot - cum)
    rt = (r * e_in).astype(BF16)
    at = (-kk * e_prev).astype(BF16)
    bi = (bb * e_inv).astype(BF16)
    ki = (kd * e_inv).astype(BF16)
    bh = (bb * e_out).astype(BF16)
    kh = (kd * e_out).astype(BF16)
    vb = v.astype(BF16)
    sb = st.astype(BF16)
    lhs = jnp.concatenate([at, rt], axis=1)
    h0 = _bmm_nt(lhs, sb)
    m_b = _bmm_nt(lhs, split_rows(bi))
    m_k = _bmm_nt(lhs, split_rows(ki))
    tri = jnp.concatenate([strict, incl], axis=1)
    m_b = m_b.astype(BF16)
    m_b = jnp.where(tri, m_b, jnp.zeros_like(m_b))
    m_k = m_k.astype(BF16)
    m_k = jnp.where(tri, m_k, jnp.zeros_like(m_k))
    ab16, a_rb = m_b[:, :c], m_b[:, c:]
    zero16 = jnp.zeros_like(ab16)
    pmb = jnp.where(ri == ci, jnp.ones_like(ab16), jnp.where(ri // 2 == ci // 2, ab16, zero16))
    m = 2
    while m < c:
        off = jnp.logical_and(ri // (2 * m) == ci // (2 * m), ri // m != ci // m)
        a_off = jnp.where(off, ab16, zero16)
        delta = _bmm(_bmm(pmb, split_rows(a_off)).astype(BF16), split_rows(pmb))
        pmb = jnp.where(off, delta.astype(BF16), pmb)
        m *= 2
    av = _bmm(m_k, split_rows(vb))
    x = h0[:, :c] + av[:, :c]
    u = _bmm(pmb, split_rows(x.astype(BF16)))
    ub = u.astype(BF16)
    o_full = h0[:, c:] + av[:, c:] + _bmm(a_rb, split_rows(ub))
    upd = _bmm_tn(jnp.concatenate([ub, vb], axis=1), jnp.concatenate([bh, kh], axis=1))
    blk_r = lax.broadcasted_iota(jnp.int32, (1, LANES, LANES), 1) // half
    blk_c = lax.broadcasted_iota(jnp.int32, (1, LANES, LANES), 2) // half
    st_new = st * jnp.exp2(tot) + jnp.where(blk_r == blk_c, upd, 0.0)
    return o_full, st_new


def _store_chunk(o_ref, rs, val):
    nb, ng = o_ref.shape[0], o_ref.shape[1]
    o_ref[:, :, rs] = val.reshape((nb, ng) + val.shape[1:]).astype(o_ref.dtype)


def _rw_scan_kernel(*refs):
    ins = refs[:12]
    y_os = refs[12:14]
    st = refs[14]

    @pl.when(pl.program_id(1) == 0)
    def _():
        st[...] = jnp.zeros_like(st)

    def body(ci, carry):
        rows = _scan_rows(ci, RW_SCAN_CHUNKS)
        loaded = []
        for dr in range(2):
            r_r, v_r, kk_r, bb_r, kd_r, lw_r = ins[6 * dr:6 * dr + 6]
            rs = rows[dr]
            loaded.append((_load_chunk(r_r, (), rs), _load_chunk(v_r, (), rs), _load_chunk(kk_r, (), rs),
                           _load_chunk(bb_r, (0,), rs), _load_chunk(kd_r, (0,), rs), _load_chunk(lw_r, (0,), rs),
                           st[dr]))
        results = [_rw_chunk(*loaded[dr], rev=(dr == 1)) for dr in range(2)]
        for dr in range(2):
            _store_chunk(y_os[dr], rows[dr], results[dr][0])
            st[dr] = results[dr][1]
        return carry

    _scan_or_pad(body, RW_SCAN_CHUNKS, y_os)


def _rw_scan(feats, *, n_lat_chunks, n_ctx_chunks):
    r, v, kk, lw, bb, kd = feats
    b, ng, tp, _ = r.shape
    nb = RW_SCAN_BATCH if b % RW_SCAN_BATCH == 0 else 1
    grid, in_specs, out_specs, out_shape = _scan_specs(b, ng, tp, n_lat_chunks, n_ctx_chunks, 3, 3, nb,
                                                       RW_SCAN_CHUNKS)
    return pl.pallas_call(
        _rw_scan_kernel,
        grid=grid,
        in_specs=in_specs,
        out_specs=out_specs,
        out_shape=out_shape,
        scratch_shapes=[pltpu.VMEM((2, nb * ng, LANES, LANES), F32)],
        name="rwkv7_scan",
        compiler_params=pltpu.CompilerParams(
            dimension_semantics=("parallel", "arbitrary"), vmem_limit_bytes=VMEM_LIMIT),
    )(r, v, kk, bb, kd, lw, r, v, kk, bb, kd, lw)


def _residual_tail(yg_bf16, wo_ref, x_ref, mod_ref, pg_ref, o_ref):
    d = x_ref.shape[-1]
    o = _dot(yg_bf16, wo_ref[...])
    ms = jnp.mean(o * o, axis=-1, keepdims=True)
    o = o * lax.rsqrt(ms + NORM_EPS) * pg_ref[...]
    gate = mod_ref[0, 0][:, 2 * d:]
    o_ref[0] = x_ref[0] + gate * o


def _rw_out_kernel(yf_ref, yb_ref, z_ref, bv_ref, lnw_ref, lnb_ref, hs_ref, wo_ref, x_ref, mod_ref, pg_ref,
                   o_ref, yg):
    inv = 1.0 / RW_HEAD
    hs = hs_ref[...]
    for gq in range(yf_ref.shape[1]):
        cs = slice(gq * LANES, (gq + 1) * LANES)
        y = yf_ref[0, gq].astype(F32) + yb_ref[0, gq].astype(F32)
        mu = _dot(y.astype(BF16), hs) * inv
        yc = y - mu
        var = _dot((yc * yc).astype(BF16), hs) * inv
        yn = yc * lax.rsqrt(var + LN_X_EPS) * lnw_ref[:, cs] + lnb_ref[:, cs] + bv_ref[0, gq].astype(F32)
        yg[:, cs] = yn.astype(BF16) * z_ref[0, gq]
    _residual_tail(yg[...], wo_ref, x_ref, mod_ref, pg_ref, o_ref)


def _hg_out_kernel(yf_ref, yb_ref, sg_ref, gn_ref, wo_ref, x_ref, mod_ref, pg_ref, o_ref, yg):
    for hx in range(yf_ref.shape[1]):
        cs = slice(hx * LANES, (hx + 1) * LANES)
        y = yf_ref[0, hx].astype(F32) + yb_ref[0, hx].astype(F32)
        ms = jnp.mean(y * y, axis=-1, keepdims=True)
        yn = y * lax.rsqrt(ms + NORM_EPS) * gn_ref[...]
        yg[:, cs] = yn.astype(BF16) * sg_ref[0, hx]
    _residual_tail(yg[...], wo_ref, x_ref, mod_ref, pg_ref, o_ref)


def _out_group_spec(di):
    return pl.BlockSpec((1, di // LANES, OTILE, LANES), lambda bi, j: (bi, 0, j, 0))


def _mixer_out(name, kernel_fn, ys, extras, extra_specs, wo, xs, mod, post_g, n_tiles, out_rows):
    b, tp, d = xs.shape
    in_place = out_rows == tp
    assert in_place or out_rows == n_tiles * OTILE
    di = wo.shape[0]
    per = TILE // OTILE
    tokd = pl.BlockSpec((1, OTILE, d), lambda bi, j: (bi, j, 0))
    in_specs = [_out_group_spec(di)] * 2 + extra_specs + [
        pl.BlockSpec((di, d), lambda bi, j: (0, 0)),
        tokd,
        pl.BlockSpec((1, 1, 1, mod.shape[-1]), lambda bi, j: (bi, j // per, 0, 0)),
        pl.BlockSpec((1, d), lambda bi, j: (0, 0)),
    ]
    n_in = len(in_specs)
    return pl.pallas_call(
        kernel_fn,
        grid=(b, n_tiles),
        in_specs=in_specs,
        out_specs=tokd,
        out_shape=jax.ShapeDtypeStruct((b, out_rows, d), F32),
        scratch_shapes=[pltpu.VMEM((OTILE, di), BF16)],
        input_output_aliases={n_in - 3: 0} if in_place else {},
        name=name,
        compiler_params=pltpu.CompilerParams(
            dimension_semantics=("parallel", "parallel"), vmem_limit_bytes=VMEM_LIMIT),
    )(*ys, *extras, wo, xs, mod, post_g)


def _hg_feat_kernel(x_ref, mod_ref, g_ref, wq_ref, wf0_ref, wf1_ref, wi_ref, wg_ref, lb_ref,
                    q_o, i_o, sg_o, lf_o, kd_o, hbf):
    n = pl.program_id(1)
    j = pl.program_id(2)
    d = x_ref.shape[-1]
    rows = x_ref.shape[1]
    base = pl.multiple_of(j * rows, rows)

    @pl.when(n == 0)
    def _prologue():
        g = g_ref[...]
        for t in range(rows // TILE):
            m = mod_ref[0, t]
            shift, scale = m[:, :d], m[:, d:2 * d]

            def norm_rows(rb, c, t=t, shift=shift, scale=scale):
                r0 = pl.multiple_of(t * TILE + rb * GRID_W, GRID_W)
                hbf[pl.ds(base + r0, GRID_W), :] = _adaln(x_ref[0, pl.ds(r0, GRID_W), :], g, scale,
                                                          shift).astype(BF16)
                return c

            lax.fori_loop(0, TILE // GRID_W, norm_rows, 0)

    h = hbf[pl.ds(base, rows), :]
    lb = lb_ref[...]
    _store_groups(q_o, (0,), _silu(_dot(h, wq_ref[...])))
    for dr, wf_ref in enumerate((wf0_ref, wf1_ref)):
        f = lb + (1.0 - lb) * jax.nn.sigmoid(_dot(h, wf_ref[...]))
        _store_groups(kd_o, (dr, 0), 1.0 - f)
        _store_groups(lf_o, (dr, 0), jnp.log2(f))
    _store_groups(i_o, (0,), _dot(h, wi_ref[...]))
    _store_groups(sg_o, (0,), _silu(_dot(h, wg_ref[...])))


def _hg_features(xs, mod, pre_g, w_in, lb, *, tn=256):
    b, tp, d = xs.shape
    di = w_in.shape[-1] // 5
    nt = di // tn
    per = HG_TILES_PER_STEP if (tp // TILE) % HG_TILES_PER_STEP == 0 else 1
    rows = per * TILE
    nsteps = tp // rows
    _, _, s1, s2 = _group_major_specs(b, tp, di, tn)
    gq = tn // LANES
    one = pl.BlockSpec((1, gq, rows, LANES), lambda bi, n, j: (bi, n, j, 0))
    two = pl.BlockSpec((2, 1, gq, rows, LANES), lambda bi, n, j: (0, bi, n, j, 0))

    def tok_tile(n, j):
        return jnp.where(n == 0, j, nsteps - 1)

    return pl.pallas_call(
        _hg_feat_kernel,
        grid=(b, nt, nsteps),
        in_specs=[
            pl.BlockSpec((1, rows, d), lambda bi, n, j: (bi, tok_tile(n, j), 0)),
            pl.BlockSpec((1, per, 1, mod.shape[-1]), lambda bi, n, j: (bi, tok_tile(n, j), 0, 0)),
            pl.BlockSpec((1, d), lambda bi, n, j: (0, 0)),
            *[pl.BlockSpec((d, tn), lambda bi, n, j, part=part: (0, part * nt + n)) for part in range(5)],
            pl.BlockSpec((1, tn), lambda bi, n, j: (0, n)),
        ],
        out_specs=[one, one, one, two, two],
        out_shape=[s1(BF16), s1(BF16), s1(BF16), s2(F32), s2(BF16)],
        scratch_shapes=[pltpu.VMEM((tp, d), BF16)],
        name="hgrn2_features",
        compiler_params=pltpu.CompilerParams(
            dimension_semantics=("parallel", "arbitrary", "arbitrary"), vmem_limit_bytes=VMEM_LIMIT),
    )(xs, mod, pre_g, w_in, w_in, w_in, w_in, w_in, lb)


def _hg_chunk(q, k, v, lf2, st, rev):
    c = CHUNK
    sub = 8
    nh = q.shape[0]
    row = lax.broadcasted_iota(jnp.int32, (1, c, 1), 1)
    ri = lax.broadcasted_iota(jnp.int32, (1, c, c), 1)
    ci = lax.broadcasted_iota(jnp.int32, (1, c, c), 2)
    p = lf2
    diag = jnp.sum(q * k, axis=2, keepdims=True)
    att = jnp.where(ri == ci, diag, jnp.zeros((nh, c, c), F32))
    m = 1
    while m < c:
        pair = jnp.logical_and(ri // (2 * m) == ci // (2 * m), ri // m != ci // m)
        if m >= sub:
            zeros = jnp.zeros((nh, m, LANES), BF16)
            q_parts, k_parts, p_parts = [], [], []
            for blk in range(c // (2 * m)):
                lo = slice(blk * 2 * m, blk * 2 * m + m)
                hi = slice(blk * 2 * m + m, (blk + 1) * 2 * m)
                early, later = (hi, lo) if rev else (lo, hi)
                bidx = blk * 2 * m + (m if rev else m - 1)
                tt = p[:, bidx:bidx + 1, :]
                q_l = (q[:, later] * jnp.exp2(p[:, later])).astype(BF16)
                k_e = (k[:, early] * jnp.exp2(tt - p[:, early])).astype(BF16)
                if rev:
                    q_parts += [q_l, zeros]
                    k_parts += [zeros, k_e]
                    p_parts += [p[:, later] + tt, p[:, early]]
                else:
                    q_parts += [zeros, q_l]
                    k_parts += [k_e, zeros]
                    p_parts += [p[:, early], p[:, later] + tt]
            mm = _bmm_nt(jnp.concatenate(q_parts, axis=1), jnp.concatenate(k_parts, axis=1))
            p = jnp.concatenate(p_parts, axis=1)
        else:
            in_hi = ((row // m) % 2) == 1
            late = jnp.logical_not(in_hi) if rev else in_hi
            if m == 1:
                tt = jnp.where(late, pltpu.roll(p, c - 1 if rev else 1, axis=1), p)
            else:
                p4 = p.reshape(nh, c // sub, sub, LANES)
                srow = lax.broadcasted_iota(jnp.int32, (1, 1, sub, 1), 2)
                tt4 = None
                for blk in range(sub // (2 * m)):
                    bidx = blk * 2 * m + (m if rev else m - 1)
                    cand = jnp.broadcast_to(p4[:, :, bidx:bidx + 1, :], p4.shape)
                    tt4 = cand if tt4 is None else jnp.where(srow // (2 * m) == blk, cand, tt4)
                tt = tt4.reshape(nh, c, LANES)
            ee = jnp.exp2(jnp.where(late, p, tt - p))
            qt = jnp.where(late, q * ee, 0.0).astype(BF16)
            kt = jnp.where(late, 0.0, k * ee).astype(BF16)
            mm = _bmm_nt(qt, kt)
            p = jnp.where(late, p + tt, p)
        att = jnp.where(pair, mm, att)
        m *= 2
    tot = p[:, 0:1] if rev else p[:, c - 1:c]
    vb = v.astype(BF16)
    y = _bmm(att.astype(BF16), vb) + _bmm_nt((q * jnp.exp2(p)).astype(BF16), st.astype(BF16))
    st_new = st * jnp.exp2(tot) + _bmm_tn(vb, (k * jnp.exp2(tot - p)).astype(BF16))
    return y, st_new


def _hg_scan_kernel(*refs):
    ins = refs[:8]
    y_os = refs[8:10]
    st = refs[10]

    @pl.when(pl.program_id(1) == 0)
    def _():
        st[...] = jnp.zeros_like(st)

    def body(ci, carry):
        rows = _scan_rows(ci, HG_SCAN_CHUNKS)
        loaded = []
        for dr in range(2):
            q_r, v_r, kd_r, lf_r = ins[4 * dr:4 * dr + 4]
            rs = rows[dr]
            loaded.append((_load_chunk(q_r, (), rs), _load_chunk(kd_r, (0,), rs), _load_chunk(v_r, (), rs, BF16),
                           _load_chunk(lf_r, (0,), rs), st[dr]))
        results = [_hg_chunk(*loaded[dr], rev=(dr == 1)) for dr in range(2)]
        for dr in range(2):
            _store_chunk(y_os[dr], rows[dr], results[dr][0])
            st[dr] = results[dr][1]
        return carry

    _scan_or_pad(body, HG_SCAN_CHUNKS, y_os)


def _hg_scan(q, v, lf, kd, *, n_lat_chunks, n_ctx_chunks):
    b, ng, tp, _ = q.shape
    grid, in_specs, out_specs, out_shape = _scan_specs(b, ng, tp, n_lat_chunks, n_ctx_chunks, 2, 2, 1,
                                                       HG_SCAN_CHUNKS)
    return pl.pallas_call(
        _hg_scan_kernel,
        grid=grid,
        in_specs=in_specs,
        out_specs=out_specs,
        out_shape=out_shape,
        scratch_shapes=[pltpu.VMEM((2, ng, LANES, LANES), F32)],
        name="hgrn2_scan",
        compiler_params=pltpu.CompilerParams(
            dimension_semantics=("parallel", "arbitrary"), vmem_limit_bytes=VMEM_LIMIT),
    )(q, v, kd, lf, q, v, kd, lf)


def _pad_rows(w, rows):
    return jnp.pad(w, ((0, rows - w.shape[0]), (0, 0)))


def _pad_cols(w, cols):
    return jnp.pad(w, ((0, 0), (0, cols - w.shape[1])))


def kernel(x, c, ctx, c_ctx, mod_w, mod_b, pre_g, post_g, rw_mix, rw_proj, rw_wo, rw_w0, rw_w1, rw_w2, rw_a0,
           rw_a1, rw_a2, rw_v0, rw_v1, rw_v2, rw_kk, rw_ka, rw_rk, rw_lnw, rw_lnb, hg_win, hg_wo, hg_gn, hg_lb):
    b, seq, d = x.shape
    ctx_len = ctx.shape[1]
    depth = mod_w.shape[0]
    di = rw_proj.shape[-1]
    assert seq % TILE == 0 and ctx_len % CHUNK == 0 and ctx_len <= TILE and seq % GRID_W == 0
    n_lat_tiles = seq // TILE
    tp = seq + TILE
    n_lat_chunks, n_ctx_chunks = seq // CHUNK, ctx_len // CHUNK

    xs = jnp.concatenate([x, ctx, jnp.zeros((b, tp - seq - ctx_len, d), x.dtype)], axis=1)

    rows = ((b + 1 + 7) // 8) * 8
    cond = jnp.concatenate([c, c_ctx[None], jnp.zeros((rows - b - 1, d), c.dtype)], axis=0)
    mod_all = _modulation(cond, mod_w, mod_b)
    mod_lat = jnp.broadcast_to(mod_all[:, :b, None, :], (depth, b, n_lat_tiles, 3 * d))
    mod_ctx = jnp.broadcast_to(mod_all[:, b:b + 1, None, :], (depth, b, 1, 3 * d))
    mod_t = jnp.concatenate([mod_lat, mod_ctx], axis=2)[:, :, :, None, :]

    p_lb = jax.nn.softmax(hg_lb.astype(F32), axis=0)
    lb_all = jnp.cumsum(p_lb, axis=0) - p_lb[0]

    vf = None
    for i in range(depth):
        ctx_out = i < depth - 1
        out_rows = tp if ctx_out else seq
        n_out_tiles = out_rows // OTILE
        j = i // 2
        mod_i = mod_t[i]
        pg = pre_g[i][None]
        if i % 2 == 0:
            r64 = rw_w1.shape[-1]
            w2 = jnp.stack([jnp.pad(rw_w2[j, 0], ((0, LANES - r64), (0, 0))),
                            jnp.pad(rw_w2[j, 1], ((r64, LANES - 2 * r64), (0, 0)))])
            a2 = jnp.stack([jnp.pad(rw_a2[j, 0], ((0, LANES - r64), (0, 0))),
                            jnp.pad(rw_a2[j, 1], ((r64, LANES - 2 * r64), (0, 0)))])
            if j == 0:
                v0 = jnp.zeros((di,), F32)
                v1 = jnp.zeros((d, LANES), BF16)
                v2 = jnp.zeros((LANES, di), BF16)
            else:
                v0 = rw_v0[j - 1]
                v1 = _pad_cols(rw_v1[j - 1], LANES).astype(BF16)
                v2 = _pad_rows(rw_v2[j - 1], LANES).astype(BF16)
            p = {
                "mix": rw_mix[j],
                "proj": rw_proj[j].astype(BF16),
                "w1": jnp.concatenate([rw_w1[j, 0], rw_w1[j, 1]], axis=1).astype(BF16),
                "a1": jnp.concatenate([rw_a1[j, 0], rw_a1[j, 1]], axis=1).astype(BF16),
                "v1": v1,
                "w2": w2.astype(BF16),
                "a2": a2.astype(BF16),
                "v2": v2,
                "vec": jnp.stack([rw_w0[j, 0], rw_w0[j, 1], rw_a0[j, 0], rw_a0[j, 1], v0, rw_kk[j], rw_ka[j],
                                  rw_rk[j]]),
            }
            r, v, kk, z, bv, lw, bb, kd = _rw_features(xs, mod_i, pg, p, vf if j > 0 else None,
                                                       n_lat_tiles=n_lat_tiles, ctx_len=ctx_len)
            if j == 0:
                vf = v
            yf, yb = _rw_scan((r, v, kk, lw, bb, kd), n_lat_chunks=n_lat_chunks, n_ctx_chunks=n_ctx_chunks)
            gspec = _out_group_spec(di)
            extras = [z, bv, rw_lnw[j][None], rw_lnb[j][None], _head_sum_matrix(LANES, RW_HEAD)]
            especs = [gspec, gspec, pl.BlockSpec((1, di), lambda bi, jj: (0, 0)),
                      pl.BlockSpec((1, di), lambda bi, jj: (0, 0)), pl.BlockSpec((LANES, LANES), lambda bi, jj: (0, 0))]
            xs = _mixer_out("rwkv7_out", _rw_out_kernel, (yf, yb), extras, especs, rw_wo[j].astype(BF16), xs, mod_i,
                            post_g[i][None], n_out_tiles, out_rows)
        else:
            q, iv, sg, lf, kd = _hg_features(xs, mod_i, pg, hg_win[j].astype(BF16), lb_all[i][None])
            yf, yb = _hg_scan(q, iv, lf, kd, n_lat_chunks=n_lat_chunks, n_ctx_chunks=n_ctx_chunks)
            extras = [sg, hg_gn[j][None]]
            especs = [_out_group_spec(di), pl.BlockSpec((1, HG_HEAD), lambda bi, jj: (0, 0))]
            xs = _mixer_out("hgrn2_out", _hg_out_kernel, (yf, yb), extras, especs, hg_wo[j].astype(BF16), xs, mod_i,
                            post_g[i][None], n_out_tiles, out_rows)
    return xs
```

```python
import functools
import math

import jax
import jax.numpy as jnp
from jax import lax
from jax.experimental import pallas as pl
from jax.experimental.pallas import tpu as pltpu

F32 = jnp.float32
BF16 = jnp.bfloat16
HI = lax.Precision.HIGHEST

GRID_W = 64
RW_HEAD = 64
HG_HEAD = 128
CHUNK = 64
TILE = 512
OTILE = 512
HG_TILES_PER_STEP = 3
RW_SCAN_CHUNKS = 4
RW_SCAN_BATCH = 1
HG_SCAN_CHUNKS = 4
RW_GROUP_SPLITS = 1
HG_HEAD_SPLITS = 4
NORM_EPS = 1e-6
LN_X_EPS = 64e-5
LANES = 128
VMEM_LIMIT = 56 * 1024 * 1024


def _silu(x):
    return x * jax.nn.sigmoid(x)


def _dot(a, b):
    return jnp.dot(a, b, preferred_element_type=F32)


def _mod_kernel(s_ref, w_ref, b_ref, o_ref):
    s = _silu(s_ref[...])
    o_ref[0] = jnp.dot(s, w_ref[0], precision=HI, preferred_element_type=F32) + b_ref[0]


def _modulation(cond, mod_w, mod_b):
    depth, d, d3 = mod_w.shape
    rows = cond.shape[0]
    nb = d3 // d
    return pl.pallas_call(
        _mod_kernel,
        grid=(depth, nb),
        in_specs=[
            pl.BlockSpec((rows, d), lambda i, n: (0, 0)),
            pl.BlockSpec((1, d, d), lambda i, n: (i, 0, n)),
            pl.BlockSpec((1, 1, d), lambda i, n: (i, 0, n)),
        ],
        out_specs=pl.BlockSpec((1, rows, d), lambda i, n: (i, 0, n)),
        out_shape=jax.ShapeDtypeStruct((depth, rows, d3), F32),
        name="adaln_modulation",
        compiler_params=pltpu.CompilerParams(vmem_limit_bytes=VMEM_LIMIT),
    )(cond, mod_w, mod_b.reshape(depth, 1, d3))


def _adaln(xb, g, scale, shift):
    ms = jnp.mean(xb * xb, axis=-1, keepdims=True)
    return xb * lax.rsqrt(ms + NORM_EPS) * g * (1.0 + scale) + shift


def _store_groups(o_ref, lead, val):
    for gq in range(val.shape[1] // LANES):
        o_ref[lead + (gq,)] = val[:, gq * LANES:(gq + 1) * LANES].astype(o_ref.dtype)


def _load_groups(ref, lead):
    n = ref.shape[len(lead)]
    return jnp.concatenate([ref[lead + (gq,)] for gq in range(n)], axis=1)


def _head_sum_matrix(width, head):
    r = lax.broadcasted_iota(jnp.int32, (width, width), 0) // head
    c = lax.broadcasted_iota(jnp.int32, (width, width), 1) // head
    return (r == c).astype(BF16)


def _rw_feat_kernel(n_lat_tiles, ctx_len, has_vres, *refs):
    (x_ref, xp_ref, xn_ref, mod_ref, g_ref, mix_ref, w1_ref, a1_ref, v1_ref,
     proj_ref, w2_ref, a2_ref, v2_ref, vec_ref, hs_ref) = refs[:15]
    pos = 15
    vf_ref = None
    if has_vres:
        vf_ref = refs[pos]
        pos += 1
    (r_o, v_o, kk_o, z_o, bv_o, lw_o, bb_o, kd_o) = refs[pos:pos + 8]
    hbuf, xm, tw, ta, tv = refs[pos + 8:]

    j = pl.program_id(1)
    n = pl.program_id(2)
    d = x_ref.shape[-1]
    q4 = d // 4
    nblk = TILE // GRID_W

    @pl.when(n == 0)
    def _prologue():
        m = mod_ref[0, 0]
        shift, scale = m[:, :d], m[:, d:2 * d]
        g = g_ref[...]
        is_lat = j < n_lat_tiles
        prev_ok = jnp.logical_and(j >= 1, is_lat)
        next_ok = j < n_lat_tiles - 1

        def norm_rows(rb, c):
            r0 = pl.multiple_of(rb * GRID_W, GRID_W)
            hbuf[pl.ds(r0 + GRID_W, GRID_W), :] = _adaln(x_ref[0, pl.ds(r0, GRID_W), :], g, scale, shift)
            return c

        lax.fori_loop(0, nblk, norm_rows, 0)
        hp = _adaln(xp_ref[0], g, scale, shift)
        hbuf[0:GRID_W, :] = jnp.where(prev_ok, hp, 0.0)
        hn = _adaln(xn_ref[0], g, scale, shift)
        hbuf[TILE + GRID_W:TILE + 2 * GRID_W, :] = jnp.where(next_ok, hn, 0.0)

        row = lax.broadcasted_iota(jnp.int32, (GRID_W, 1), 0)

        def store_mixed(r0, cs, h_c, hs_c):
            dd = hs_c - h_c
            for k in range(6):
                xm[k, pl.ds(r0, GRID_W), cs] = (h_c + dd * mix_ref[k:k + 1, cs]).astype(BF16)

        def mix_lat(rb, c):
            r0 = pl.multiple_of(rb * GRID_W, GRID_W)
            for qi in range(4):
                cs = slice(qi * q4, (qi + 1) * q4)
                h_c = hbuf[pl.ds(r0 + GRID_W, GRID_W), cs]
                if qi == 0:
                    hs_c = jnp.where(row == 0, 0.0, pltpu.roll(h_c, 1, axis=0))
                elif qi == 1:
                    hs_c = jnp.where(row == GRID_W - 1, 0.0, pltpu.roll(h_c, GRID_W - 1, axis=0))
                elif qi == 2:
                    hs_c = hbuf[pl.ds(r0, GRID_W), cs]
                else:
                    hs_c = hbuf[pl.ds(r0 + 2 * GRID_W, GRID_W), cs]
                store_mixed(r0, cs, h_c, hs_c)
            return c

        def mix_ctx(rb, c):
            r0 = pl.multiple_of(rb * GRID_W, GRID_W)
            t = row + rb * GRID_W
            for qi in range(4):
                cs = slice(qi * q4, (qi + 1) * q4)
                h_c = hbuf[pl.ds(r0 + GRID_W, GRID_W), cs]
                if qi < 2:
                    edge = hbuf[pl.ds(r0 + GRID_W - 1, 1), cs]
                    hs_c = jnp.where(row == 0, edge, pltpu.roll(h_c, 1, axis=0))
                else:
                    edge = hbuf[pl.ds(r0 + 2 * GRID_W, 1), cs]
                    hs_c = jnp.where(row == GRID_W - 1, edge, pltpu.roll(h_c, GRID_W - 1, axis=0))
                    hs_c = jnp.where(t == ctx_len - 1, 0.0, hs_c)
                store_mixed(r0, cs, h_c, hs_c)
            return c

        @pl.when(is_lat)
        def _():
            lax.fori_loop(0, nblk, mix_lat, 0)

        @pl.when(jnp.logical_not(is_lat))
        def _():
            lax.fori_loop(0, nblk, mix_ctx, 0)

        tw[...] = jnp.tanh(_dot(xm[1], w1_ref[...])).astype(BF16)
        ta[...] = _dot(xm[4], a1_ref[...]).astype(BF16)
        if has_vres:
            tv[...] = _dot(xm[3], v1_ref[...]).astype(BF16)

    vec = vec_ref[...]
    r = _dot(xm[0], proj_ref[0, 0])
    k = _dot(xm[2], proj_ref[0, 1])
    v = _dot(xm[3], proj_ref[0, 2])
    z = _silu(_dot(xm[5], proj_ref[0, 3]))
    if has_vres:
        gate = jax.nn.sigmoid(vec[4:5] + _dot(tv[...], v2_ref[...]))
        v = v + (_load_groups(vf_ref, (0,)).astype(F32) - v) * gate
    hs = hs_ref[...]
    kk = k * vec[5:6]
    ss = _dot((kk * kk).astype(BF16), hs)
    kk = kk * lax.rsqrt(jnp.maximum(ss, 1e-24))
    decay_scale = -math.exp(-0.5) * math.log2(math.e)
    ksum = None
    for dr in range(2):
        u = vec[dr:dr + 1] + _dot(tw[...], w2_ref[dr])
        _store_groups(lw_o, (dr, 0), decay_scale * jax.nn.sigmoid(u))
        a = jax.nn.sigmoid(vec[2 + dr:3 + dr] + _dot(ta[...], a2_ref[dr]))
        kd = k * (1.0 + (a - 1.0) * vec[6:7])
        _store_groups(bb_o, (dr, 0), kk * a)
        _store_groups(kd_o, (dr, 0), kd)
        ksum = kd if ksum is None else ksum + kd
    bonus = _dot((r * ksum * vec[7:8]).astype(BF16), hs)
    _store_groups(r_o, (0,), r)
    _store_groups(v_o, (0,), v)
    _store_groups(kk_o, (0,), kk)
    _store_groups(z_o, (0,), z)
    _store_groups(bv_o, (0,), bonus * v)


def _group_major_specs(b, tp, di, tn):
    gq = tn // LANES
    one = pl.BlockSpec((1, gq, TILE, LANES), lambda bi, j, n: (bi, n, j, 0))
    two = pl.BlockSpec((2, 1, gq, TILE, LANES), lambda bi, j, n: (0, bi, n, j, 0))
    s1 = lambda dt: jax.ShapeDtypeStruct((b, di // LANES, tp, LANES), dt)
    s2 = lambda dt: jax.ShapeDtypeStruct((2, b, di // LANES, tp, LANES), dt)
    return one, two, s1, s2


def _rw_features(xs, mod, pre_g, p, vf, *, n_lat_tiles, ctx_len, tn=256):
    b, tp, d = xs.shape
    di = p["proj"].shape[-1]
    nt = di // tn
    ntiles = tp // TILE
    nb64 = tp // GRID_W
    per = TILE // GRID_W
    has_vres = vf is not None

    def tok(bi, j, n):
        return (bi, j, 0)

    in_specs = [
        pl.BlockSpec((1, TILE, d), tok),
        pl.BlockSpec((1, GRID_W, d), lambda bi, j, n: (bi, jnp.maximum(j * per - 1, 0), 0)),
        pl.BlockSpec((1, GRID_W, d), lambda bi, j, n: (bi, jnp.minimum((j + 1) * per, nb64 - 1), 0)),
        pl.BlockSpec((1, 1, 1, mod.shape[-1]), lambda bi, j, n: (bi, j, 0, 0)),
        pl.BlockSpec((1, d), lambda bi, j, n: (0, 0)),
        pl.BlockSpec((6, d), lambda bi, j, n: (0, 0)),
        pl.BlockSpec((d, LANES), lambda bi, j, n: (0, 0)),
        pl.BlockSpec((d, LANES), lambda bi, j, n: (0, 0)),
        pl.BlockSpec((d, LANES), lambda bi, j, n: (0, 0)),
        pl.BlockSpec((1, 4, d, tn), lambda bi, j, n: (0, 0, 0, n)),
        pl.BlockSpec((2, LANES, tn), lambda bi, j, n: (0, 0, n)),
        pl.BlockSpec((2, LANES, tn), lambda bi, j, n: (0, 0, n)),
        pl.BlockSpec((LANES, tn), lambda bi, j, n: (0, n)),
        pl.BlockSpec((8, tn), lambda bi, j, n: (0, n)),
        pl.BlockSpec((tn, tn), lambda bi, j, n: (0, 0)),
    ]
    args = [xs, xs, xs, mod, pre_g, p["mix"], p["w1"], p["a1"], p["v1"], p["proj"][None], p["w2"], p["a2"],
            p["v2"], p["vec"], _head_sum_matrix(tn, RW_HEAD)]
    one, two, s1, s2 = _group_major_specs(b, tp, di, tn)
    if has_vres:
        in_specs.append(one)
        args.append(vf)
    return pl.pallas_call(
        functools.partial(_rw_feat_kernel, n_lat_tiles, ctx_len, has_vres),
        grid=(b, ntiles, nt),
        in_specs=in_specs,
        out_specs=[one, one, one, one, one, two, two, two],
        out_shape=[s1(BF16), s1(BF16), s1(BF16), s1(BF16), s1(BF16), s2(F32), s2(BF16), s2(BF16)],
        scratch_shapes=[
            pltpu.VMEM((TILE + 2 * GRID_W, d), F32),
            pltpu.VMEM((6, TILE, d), BF16),
            pltpu.VMEM((TILE, LANES), BF16),
            pltpu.VMEM((TILE, LANES), BF16),
            pltpu.VMEM((TILE, LANES), BF16),
        ],
        name="rwkv7_features",
        compiler_params=pltpu.CompilerParams(
            dimension_semantics=("parallel", "parallel", "arbitrary"), vmem_limit_bytes=VMEM_LIMIT),
    )(*args)


def _bmm(a, b):
    return lax.dot_general(a, b, (((2,), (1,)), ((0,), (0,))), preferred_element_type=F32)


def _bmm_nt(a, b):
    return lax.dot_general(a, b, (((2,), (2,)), ((0,), (0,))), preferred_element_type=F32)


def _bmm_tn(a, b):
    return lax.dot_general(a, b, (((1,), (1,)), ((0,), (0,))), preferred_element_type=F32)


def _chunk_cumsum(x, rev):
    c = x.shape[1]
    row = lax.broadcasted_iota(jnp.int32, (1, c, 1), 1)
    s = 1
    while s < c:
        if rev:
            x = x + jnp.where(row < c - s, pltpu.roll(x, c - s, axis=1), 0.0)
        else:
            x = x + jnp.where(row >= s, pltpu.roll(x, s, axis=1), 0.0)
        s *= 2
    return x


def _scan_chunk_maps(n_lat_chunks, n_ctx_chunks):
    total = n_lat_chunks + n_ctx_chunks

    def fwd(s):
        return jnp.where(s < total, (s + n_lat_chunks) % total, total)

    def bwd(s):
        return jnp.where(s < total, total - 1 - s, total)

    return total, fwd, bwd


def _scan_specs(b, ng, tp, n_lat_chunks, n_ctx_chunks, n_one, n_two, nb, nchunks):
    assert n_lat_chunks % nchunks == 0 and n_ctx_chunks % nchunks == 0 and b % nb == 0
    rows = CHUNK * nchunks
    total, fwd, bwd = _scan_chunk_maps(n_lat_chunks // nchunks, n_ctx_chunks // nchunks)
    assert tp == (total + 1) * rows
    in_specs = []
    for dr, cm in enumerate((fwd, bwd)):
        one = pl.BlockSpec((nb, ng, rows, LANES), lambda bi, s, cm=cm: (bi, 0, cm(s), 0))
        two = pl.BlockSpec((1, nb, ng, rows, LANES), lambda bi, s, cm=cm, dr=dr: (dr, bi, 0, cm(s), 0))
        in_specs += [one] * n_one + [two] * n_two
    out_specs = [pl.BlockSpec((nb, ng, rows, LANES), lambda bi, s, cm=cm: (bi, 0, cm(s), 0)) for cm in (fwd, bwd)]
    out_shape = [jax.ShapeDtypeStruct((b, ng, tp, LANES), BF16)] * 2
    return (b // nb, total + 1), in_specs, out_specs, out_shape


def _scan_rows(ci, nchunks):
    fwd = pl.ds(pl.multiple_of(ci * CHUNK, CHUNK), CHUNK)
    bwd = pl.ds(pl.multiple_of((nchunks - 1 - ci) * CHUNK, CHUNK), CHUNK)
    return fwd, bwd


def _scan_or_pad(body, nchunks, y_os):
    s = pl.program_id(1)
    last = pl.num_programs(1) - 1

    @pl.when(s < last)
    def _():
        lax.fori_loop(0, nchunks, body, 0)

    @pl.when(s == last)
    def _():
        for y_o in y_os:
            y_o[...] = jnp.zeros_like(y_o)


def _chunk_by_groups(chunk_fn, loaded, splits):
    n = loaded[0][0].shape[0]
    step = n // splits
    parts = [[], []]
    for part in range(splits):
        gs = slice(part * step, (part + 1) * step)
        for dr in range(2):
            parts[dr].append(chunk_fn(*[a[gs] for a in loaded[dr]], rev=(dr == 1)))
    return [tuple(jnp.concatenate([p[i] for p in parts[dr]], axis=0) for i in range(2)) for dr in range(2)]


def _load_chunk(ref, lead, rs, dtype=F32):
    x = ref[lead + (slice(None), slice(None), rs)]
    return x.reshape((x.shape[0] * x.shape[1],) + x.shape[2:]).astype(dtype)


def _rw_chunk(r, v, kk, bb, kd, lw, st, rev):
    c = CHUNK
    half = RW_HEAD
    ri = lax.broadcasted_iota(jnp.int32, (1, c, LANES), 1)
    ci = lax.broadcasted_iota(jnp.int32, (1, c, LANES), 2) % half
    head_a = lax.broadcasted_iota(jnp.int32, (1, 1, LANES), 2) < half

    def split_rows(x):
        zero = jnp.zeros_like(x)
        return jnp.concatenate([jnp.where(head_a, x, zero), jnp.where(head_a, zero, x)], axis=1)

    if rev:
        incl, strict = ci >= ri, ci > ri
    else:
        incl, strict = ci <= ri, ci < ri
    cum = _chunk_cumsum(lw, rev)
    tot = cum[:, 0:1] if rev else cum[:, c - 1:c]
    e_in = jnp.exp2(cum)
    e_prev = jnp.exp2(cum - lw)
    e_inv = jnp.exp2(-cum)
    e_out = jnp.exp2(tot - cum)
    rt = (r * e_in).astype(BF16)
    at = (-kk * e_prev).astype(BF16)
    bi = (bb * e_inv).astype(BF16)
    ki = (kd * e_inv).astype(BF16)
    bh = (bb * e_out).astype(BF16)
    kh = (kd * e_out).astype(BF16)
    vb = v.astype(BF16)
    sb = st.astype(BF16)
    lhs = jnp.concatenate([at, rt], axis=1)
    h0 = _bmm_nt(lhs, sb)
    m_b = _bmm_nt(lhs, split_rows(bi))
    m_k = _bmm_nt(lhs, split_rows(ki))
    tri = jnp.concatenate([strict, incl], axis=1)
    m_b = m_b.astype(BF16)
    m_b = jnp.where(tri, m_b, jnp.zeros_like(m_b))
    m_k = m_k.astype(BF16)
    m_k = jnp.where(tri, m_k, jnp.zeros_like(m_k))
    ab16, a_rb = m_b[:, :c], m_b[:, c:]
    zero16 = jnp.zeros_like(ab16)
    pmb = jnp.where(ri == ci, jnp.ones_like(ab16), jnp.where(ri // 2 == ci // 2, ab16, zero16))
    m = 2
    while m < c:
        off = jnp.logical_and(ri // (2 * m) == ci // (2 * m), ri // m != ci // m)
        a_off = jnp.where(off, ab16, zero16)
        delta = _bmm(_bmm(pmb, split_rows(a_off)).astype(BF16), split_rows(pmb))
        pmb = jnp.where(off, delta.astype(BF16), pmb)
        m *= 2
    av = _bmm(m_k, split_rows(vb))
    x = h0[:, :c] + av[:, :c]
    u = _bmm(pmb, split_rows(x.astype(BF16)))
    ub = u.astype(BF16)
    o_full = h0[:, c:] + av[:, c:] + _bmm(a_rb, split_rows(ub))
    upd = _bmm_tn(jnp.concatenate([ub, vb], axis=1), jnp.concatenate([bh, kh], axis=1))
    blk_r = lax.broadcasted_iota(jnp.int32, (1, LANES, LANES), 1) // half
    blk_c = lax.broadcasted_iota(jnp.int32, (1, LANES, LANES), 2) // half
    st_new = st * jnp.exp2(tot) + jnp.where(blk_r == blk_c, upd, 0.0)
    return o_full, st_new


def _store_chunk(o_ref, rs, val):
    nb, ng = o_ref.shape[0], o_ref.shape[1]
    o_ref[:, :, rs] = val.reshape((nb, ng) + val.shape[1:]).astype(o_ref.dtype)


def _rw_scan_kernel(*refs):
    ins = refs[:12]
    y_os = refs[12:14]
    st = refs[14]

    @pl.when(pl.program_id(1) == 0)
    def _():
        st[...] = jnp.zeros_like(st)

    def body(ci, carry):
        rows = _scan_rows(ci, RW_SCAN_CHUNKS)
        loaded = []
        for dr in range(2):
            r_r, v_r, kk_r, bb_r, kd_r, lw_r = ins[6 * dr:6 * dr + 6]
            rs = rows[dr]
            loaded.append((_load_chunk(r_r, (), rs), _load_chunk(v_r, (), rs), _load_chunk(kk_r, (), rs),
                           _load_chunk(bb_r, (0,), rs), _load_chunk(kd_r, (0,), rs), _load_chunk(lw_r, (0,), rs),
                           st[dr]))
        results = _chunk_by_groups(_rw_chunk, loaded, RW_GROUP_SPLITS)
        for dr in range(2):
            _store_chunk(y_os[dr], rows[dr], results[dr][0])
            st[dr] = results[dr][1]
        return carry

    _scan_or_pad(body, RW_SCAN_CHUNKS, y_os)


def _rw_scan(feats, *, n_lat_chunks, n_ctx_chunks):
    r, v, kk, lw, bb, kd = feats
    b, ng, tp, _ = r.shape
    nb = RW_SCAN_BATCH if b % RW_SCAN_BATCH == 0 else 1
    grid, in_specs, out_specs, out_shape = _scan_specs(b, ng, tp, n_lat_chunks, n_ctx_chunks, 3, 3, nb,
                                                       RW_SCAN_CHUNKS)
    return pl.pallas_call(
        _rw_scan_kernel,
        grid=grid,
        in_specs=in_specs,
        out_specs=out_specs,
        out_shape=out_shape,
        scratch_shapes=[pltpu.VMEM((2, nb * ng, LANES, LANES), F32)],
        name="rwkv7_scan",
        compiler_params=pltpu.CompilerParams(
            dimension_semantics=("parallel", "arbitrary"), vmem_limit_bytes=VMEM_LIMIT),
    )(r, v, kk, bb, kd, lw, r, v, kk, bb, kd, lw)


def _residual_tail(yg_bf16, wo_ref, x_ref, mod_ref, pg_ref, o_ref):
    d = x_ref.shape[-1]
    o = _dot(yg_bf16, wo_ref[...])
    ms = jnp.mean(o * o, axis=-1, keepdims=True)
    o = o * lax.rsqrt(ms + NORM_EPS) * pg_ref[...]
    gate = mod_ref[0, 0][:, 2 * d:]
    o_ref[0] = x_ref[0] + gate * o


def _rw_out_kernel(yf_ref, yb_ref, z_ref, bv_ref, lnw_ref, lnb_ref, hs_ref, wo_ref, x_ref, mod_ref, pg_ref,
                   o_ref, yg):
    inv = 1.0 / RW_HEAD
    hs = hs_ref[...]
    per = hs.shape[0] // LANES

    def groups(ref, g0):
        return jnp.concatenate([ref[0, g0 + i] for i in range(per)], axis=1)

    for g0 in range(0, yf_ref.shape[1], per):
        cs = slice(g0 * LANES, (g0 + per) * LANES)
        y = groups(yf_ref, g0).astype(F32) + groups(yb_ref, g0).astype(F32)
        mu = _dot(y.astype(BF16), hs) * inv
        yc = y - mu
        var = _dot((yc * yc).astype(BF16), hs) * inv
        yn = yc * lax.rsqrt(var + LN_X_EPS) * lnw_ref[:, cs] + lnb_ref[:, cs] + groups(bv_ref, g0).astype(F32)
        yg[:, cs] = yn.astype(BF16) * groups(z_ref, g0)
    _residual_tail(yg[...], wo_ref, x_ref, mod_ref, pg_ref, o_ref)


def _hg_out_kernel(yf_ref, yb_ref, sg_ref, gn_ref, wo_ref, x_ref, mod_ref, pg_ref, o_ref, yg):
    for hx in range(yf_ref.shape[1]):
        cs = slice(hx * LANES, (hx + 1) * LANES)
        y = yf_ref[0, hx].astype(F32) + yb_ref[0, hx].astype(F32)
        ms = jnp.mean(y * y, axis=-1, keepdims=True)
        yn = y * lax.rsqrt(ms + NORM_EPS) * gn_ref[...]
        yg[:, cs] = yn.astype(BF16) * sg_ref[0, hx]
    _residual_tail(yg[...], wo_ref, x_ref, mod_ref, pg_ref, o_ref)


def _out_group_spec(di):
    return pl.BlockSpec((1, di // LANES, OTILE, LANES), lambda bi, j: (bi, 0, j, 0))


def _mixer_out(name, kernel_fn, ys, extras, extra_specs, wo, xs, mod, post_g, n_tiles, out_rows):
    b, tp, d = xs.shape
    in_place = out_rows == tp
    assert in_place or out_rows == n_tiles * OTILE
    di = wo.shape[0]
    per = TILE // OTILE
    tokd = pl.BlockSpec((1, OTILE, d), lambda bi, j: (bi, j, 0))
    in_specs = [_out_group_spec(di)] * 2 + extra_specs + [
        pl.BlockSpec((di, d), lambda bi, j: (0, 0)),
        tokd,
        pl.BlockSpec((1, 1, 1, mod.shape[-1]), lambda bi, j: (bi, j // per, 0, 0)),
        pl.BlockSpec((1, d), lambda bi, j: (0, 0)),
    ]
    n_in = len(in_specs)
    return pl.pallas_call(
        kernel_fn,
        grid=(b, n_tiles),
        in_specs=in_specs,
        out_specs=tokd,
        out_shape=jax.ShapeDtypeStruct((b, out_rows, d), F32),
        scratch_shapes=[pltpu.VMEM((OTILE, di), BF16)],
        input_output_aliases={n_in - 3: 0} if in_place else {},
        name=name,
        compiler_params=pltpu.CompilerParams(
            dimension_semantics=("parallel", "parallel"), vmem_limit_bytes=VMEM_LIMIT),
    )(*ys, *extras, wo, xs, mod, post_g)


def _hg_feat_kernel(x_ref, mod_ref, g_ref, wq_ref, wf0_ref, wf1_ref, wi_ref, wg_ref, lb_ref,
                    q_o, i_o, sg_o, lf_o, kd_o, hbf):
    n = pl.program_id(1)
    j = pl.program_id(2)
    d = x_ref.shape[-1]
    rows = x_ref.shape[1]
    base = pl.multiple_of(j * rows, rows)

    @pl.when(n == 0)
    def _prologue():
        g = g_ref[...]
        for t in range(rows // TILE):
            m = mod_ref[0, t]
            shift, scale = m[:, :d], m[:, d:2 * d]

            def norm_rows(rb, c, t=t, shift=shift, scale=scale):
                r0 = pl.multiple_of(t * TILE + rb * GRID_W, GRID_W)
                hbf[pl.ds(base + r0, GRID_W), :] = _adaln(x_ref[0, pl.ds(r0, GRID_W), :], g, scale,
                                                          shift).astype(BF16)
                return c

            lax.fori_loop(0, TILE // GRID_W, norm_rows, 0)

    h = hbf[pl.ds(base, rows), :]
    lb = lb_ref[...]
    _store_groups(q_o, (0,), _silu(_dot(h, wq_ref[...])))
    for dr, wf_ref in enumerate((wf0_ref, wf1_ref)):
        f = lb + (1.0 - lb) * jax.nn.sigmoid(_dot(h, wf_ref[...]))
        _store_groups(kd_o, (dr, 0), 1.0 - f)
        _store_groups(lf_o, (dr, 0), jnp.log2(f))
    _store_groups(i_o, (0,), _dot(h, wi_ref[...]))
    _store_groups(sg_o, (0,), _silu(_dot(h, wg_ref[...])))


def _hg_features(xs, mod, pre_g, w_in, lb, *, tn=256):
    b, tp, d = xs.shape
    di = w_in.shape[-1] // 5
    nt = di // tn
    per = HG_TILES_PER_STEP if (tp // TILE) % HG_TILES_PER_STEP == 0 else 1
    rows = per * TILE
    nsteps = tp // rows
    _, _, s1, s2 = _group_major_specs(b, tp, di, tn)
    gq = tn // LANES
    one = pl.BlockSpec((1, gq, rows, LANES), lambda bi, n, j: (bi, n, j, 0))
    two = pl.BlockSpec((2, 1, gq, rows, LANES), lambda bi, n, j: (0, bi, n, j, 0))

    def tok_tile(n, j):
        return jnp.where(n == 0, j, nsteps - 1)

    return pl.pallas_call(
        _hg_feat_kernel,
        grid=(b, nt, nsteps),
        in_specs=[
            pl.BlockSpec((1, rows, d), lambda bi, n, j: (bi, tok_tile(n, j), 0)),
            pl.BlockSpec((1, per, 1, mod.shape[-1]), lambda bi, n, j: (bi, tok_tile(n, j), 0, 0)),
            pl.BlockSpec((1, d), lambda bi, n, j: (0, 0)),
            *[pl.BlockSpec((d, tn), lambda bi, n, j, part=part: (0, part * nt + n)) for part in range(5)],
            pl.BlockSpec((1, tn), lambda bi, n, j: (0, n)),
        ],
        out_specs=[one, one, one, two, two],
        out_shape=[s1(BF16), s1(BF16), s1(BF16), s2(F32), s2(BF16)],
        scratch_shapes=[pltpu.VMEM((tp, d), BF16)],
        name="hgrn2_features",
        compiler_params=pltpu.CompilerParams(
            dimension_semantics=("parallel", "arbitrary", "arbitrary"), vmem_limit_bytes=VMEM_LIMIT),
    )(xs, mod, pre_g, w_in, w_in, w_in, w_in, w_in, lb)


def _hg_chunk(q, k, v, lf2, st, rev):
    c = CHUNK
    sub = 8
    nh = q.shape[0]
    row = lax.broadcasted_iota(jnp.int32, (1, c, 1), 1)
    ri = lax.broadcasted_iota(jnp.int32, (1, c, c), 1)
    ci = lax.broadcasted_iota(jnp.int32, (1, c, c), 2)
    p = lf2
    diag = jnp.sum(q * k, axis=2, keepdims=True)
    att = jnp.where(ri == ci, diag, jnp.zeros((nh, c, c), F32))
    m = 1
    while m < c:
        pair = jnp.logical_and(ri // (2 * m) == ci // (2 * m), ri // m != ci // m)
        if m >= sub:
            zeros = jnp.zeros((nh, m, LANES), BF16)
            q_parts, k_parts, p_parts = [], [], []
            for blk in range(c // (2 * m)):
                lo = slice(blk * 2 * m, blk * 2 * m + m)
                hi = slice(blk * 2 * m + m, (blk + 1) * 2 * m)
                early, later = (hi, lo) if rev else (lo, hi)
                bidx = blk * 2 * m + (m if rev else m - 1)
                tt = p[:, bidx:bidx + 1, :]
                q_l = (q[:, later] * jnp.exp2(p[:, later])).astype(BF16)
                k_e = (k[:, early] * jnp.exp2(tt - p[:, early])).astype(BF16)
                if rev:
                    q_parts += [q_l, zeros]
                    k_parts += [zeros, k_e]
                    p_parts += [p[:, later] + tt, p[:, early]]
                else:
                    q_parts += [zeros, q_l]
                    k_parts += [k_e, zeros]
                    p_parts += [p[:, early], p[:, later] + tt]
            mm = _bmm_nt(jnp.concatenate(q_parts, axis=1), jnp.concatenate(k_parts, axis=1))
            p = jnp.concatenate(p_parts, axis=1)
        else:
            in_hi = ((row // m) % 2) == 1
            late = jnp.logical_not(in_hi) if rev else in_hi
            if m == 1:
                tt = jnp.where(late, pltpu.roll(p, c - 1 if rev else 1, axis=1), p)
            else:
                p4 = p.reshape(nh, c // sub, sub, LANES)
                srow = lax.broadcasted_iota(jnp.int32, (1, 1, sub, 1), 2)
                tt4 = None
                for blk in range(sub // (2 * m)):
                    bidx = blk * 2 * m + (m if rev else m - 1)
                    cand = jnp.broadcast_to(p4[:, :, bidx:bidx + 1, :], p4.shape)
                    tt4 = cand if tt4 is None else jnp.where(srow // (2 * m) == blk, cand, tt4)
                tt = tt4.reshape(nh, c, LANES)
            ee = jnp.exp2(jnp.where(late, p, tt - p))
            qt = jnp.where(late, q * ee, 0.0).astype(BF16)
            kt = jnp.where(late, 0.0, k * ee).astype(BF16)
            mm = _bmm_nt(qt, kt)
            p = jnp.where(late, p + tt, p)
        att = jnp.where(pair, mm, att)
        m *= 2
    tot = p[:, 0:1] if rev else p[:, c - 1:c]
    vb = v.astype(BF16)
    y = _bmm(att.astype(BF16), vb) + _bmm_nt((q * jnp.exp2(p)).astype(BF16), st.astype(BF16))
    st_new = st * jnp.exp2(tot) + _bmm_tn(vb, (k * jnp.exp2(tot - p)).astype(BF16))
    return y, st_new


def _hg_scan_kernel(*refs):
    ins = refs[:8]
    y_os = refs[8:10]
    st = refs[10]

    @pl.when(pl.program_id(1) == 0)
    def _():
        st[...] = jnp.zeros_like(st)

    def body(ci, carry):
        rows = _scan_rows(ci, HG_SCAN_CHUNKS)
        loaded = []
        for dr in range(2):
            q_r, v_r, kd_r, lf_r = ins[4 * dr:4 * dr + 4]
            rs = rows[dr]
            loaded.append((_load_chunk(q_r, (), rs), _load_chunk(kd_r, (0,), rs), _load_chunk(v_r, (), rs, BF16),
                           _load_chunk(lf_r, (0,), rs), st[dr]))
        results = _chunk_by_groups(_hg_chunk, loaded, HG_HEAD_SPLITS)
        for dr in range(2):
            _store_chunk(y_os[dr], rows[dr], results[dr][0])
            st[dr] = results[dr][1]
        return carry

    _scan_or_pad(body, HG_SCAN_CHUNKS, y_os)


def _hg_scan(q, v, lf, kd, *, n_lat_chunks, n_ctx_chunks):
    b, ng, tp, _ = q.shape
    grid, in_specs, out_specs, out_shape = _scan_specs(b, ng, tp, n_lat_chunks, n_ctx_chunks, 2, 2, 1,
                                                       HG_SCAN_CHUNKS)
    return pl.pallas_call(
        _hg_scan_kernel,
        grid=grid,
        in_specs=in_specs,
        out_specs=out_specs,
        out_shape=out_shape,
        scratch_shapes=[pltpu.VMEM((2, ng, LANES, LANES), F32)],
        name="hgrn2_scan",
        compiler_params=pltpu.CompilerParams(
            dimension_semantics=("parallel", "arbitrary"), vmem_limit_bytes=VMEM_LIMIT),
    )(q, v, kd, lf, q, v, kd, lf)


def _pad_rows(w, rows):
    return jnp.pad(w, ((0, rows - w.shape[0]), (0, 0)))


def _pad_cols(w, cols):
    return jnp.pad(w, ((0, 0), (0, cols - w.shape[1])))


def kernel(x, c, ctx, c_ctx, mod_w, mod_b, pre_g, post_g, rw_mix, rw_proj, rw_wo, rw_w0, rw_w1, rw_w2, rw_a0,
           rw_a1, rw_a2, rw_v0, rw_v1, rw_v2, rw_kk, rw_ka, rw_rk, rw_lnw, rw_lnb, hg_win, hg_wo, hg_gn, hg_lb):
    b, seq, d = x.shape
    ctx_len = ctx.shape[1]
    depth = mod_w.shape[0]
    di = rw_proj.shape[-1]
    assert seq % TILE == 0 and ctx_len % CHUNK == 0 and ctx_len <= TILE and seq % GRID_W == 0
    n_lat_tiles = seq // TILE
    tp = seq + TILE
    n_lat_chunks, n_ctx_chunks = seq // CHUNK, ctx_len // CHUNK

    xs = jnp.concatenate([x, ctx, jnp.zeros((b, tp - seq - ctx_len, d), x.dtype)], axis=1)

    rows = ((b + 1 + 7) // 8) * 8
    cond = jnp.concatenate([c, c_ctx[None], jnp.zeros((rows - b - 1, d), c.dtype)], axis=0)
    mod_all = _modulation(cond, mod_w, mod_b)
    mod_lat = jnp.broadcast_to(mod_all[:, :b, None, :], (depth, b, n_lat_tiles, 3 * d))
    mod_ctx = jnp.broadcast_to(mod_all[:, b:b + 1, None, :], (depth, b, 1, 3 * d))
    mod_t = jnp.concatenate([mod_lat, mod_ctx], axis=2)[:, :, :, None, :]

    p_lb = jax.nn.softmax(hg_lb.astype(F32), axis=0)
    lb_all = jnp.cumsum(p_lb, axis=0) - p_lb[0]

    vf = None
    for i in range(depth):
        ctx_out = i < depth - 1
        out_rows = tp if ctx_out else seq
        n_out_tiles = out_rows // OTILE
        j = i // 2
        mod_i = mod_t[i]
        pg = pre_g[i][None]
        if i % 2 == 0:
            r64 = rw_w1.shape[-1]
            w2 = jnp.stack([jnp.pad(rw_w2[j, 0], ((0, LANES - r64), (0, 0))),
                            jnp.pad(rw_w2[j, 1], ((r64, LANES - 2 * r64), (0, 0)))])
            a2 = jnp.stack([jnp.pad(rw_a2[j, 0], ((0, LANES - r64), (0, 0))),
                            jnp.pad(rw_a2[j, 1], ((r64, LANES - 2 * r64), (0, 0)))])
            if j == 0:
                v0 = jnp.zeros((di,), F32)
                v1 = jnp.zeros((d, LANES), BF16)
                v2 = jnp.zeros((LANES, di), BF16)
            else:
                v0 = rw_v0[j - 1]
                v1 = _pad_cols(rw_v1[j - 1], LANES).astype(BF16)
                v2 = _pad_rows(rw_v2[j - 1], LANES).astype(BF16)
            p = {
                "mix": rw_mix[j],
                "proj": rw_proj[j].astype(BF16),
                "w1": jnp.concatenate([rw_w1[j, 0], rw_w1[j, 1]], axis=1).astype(BF16),
                "a1": jnp.concatenate([rw_a1[j, 0], rw_a1[j, 1]], axis=1).astype(BF16),
                "v1": v1,
                "w2": w2.astype(BF16),
                "a2": a2.astype(BF16),
                "v2": v2,
                "vec": jnp.stack([rw_w0[j, 0], rw_w0[j, 1], rw_a0[j, 0], rw_a0[j, 1], v0, rw_kk[j], rw_ka[j],
                                  rw_rk[j]]),
            }
            r, v, kk, z, bv, lw, bb, kd = _rw_features(xs, mod_i, pg, p, vf if j > 0 else None,
                                                       n_lat_tiles=n_lat_tiles, ctx_len=ctx_len)
            if j == 0:
                vf = v
            yf, yb = _rw_scan((r, v, kk, lw, bb, kd), n_lat_chunks=n_lat_chunks, n_ctx_chunks=n_ctx_chunks)
            gspec = _out_group_spec(di)
            extras = [z, bv, rw_lnw[j][None], rw_lnb[j][None], _head_sum_matrix(2 * LANES, RW_HEAD)]
            especs = [gspec, gspec, pl.BlockSpec((1, di), lambda bi, jj: (0, 0)),
                      pl.BlockSpec((1, di), lambda bi, jj: (0, 0)), pl.BlockSpec((2 * LANES, 2 * LANES), lambda bi, jj: (0, 0))]
            xs = _mixer_out("rwkv7_out", _rw_out_kernel, (yf, yb), extras, especs, rw_wo[j].astype(BF16), xs, mod_i,
                            post_g[i][None], n_out_tiles, out_rows)
        else:
            q, iv, sg, lf, kd = _hg_features(xs, mod_i, pg, hg_win[j].astype(BF16), lb_all[i][None])
            yf, yb = _hg_scan(q, iv, lf, kd, n_lat_chunks=n_lat_chunks, n_ctx_chunks=n_ctx_chunks)
            extras = [sg, hg_gn[j][None]]
            especs = [_out_group_spec(di), pl.BlockSpec((1, HG_HEAD), lambda bi, jj: (0, 0))]
            xs = _mixer_out("hgrn2_out", _hg_out_kernel, (yf, yb), extras, especs, hg_wo[j].astype(BF16), xs, mod_i,
                            post_g[i][None], n_out_tiles, out_rows)
    return xs
```

```python
import functools
import math

import jax
import jax.numpy as jnp
from jax import lax
from jax.experimental import pallas as pl
from jax.experimental.pallas import tpu as pltpu

F32 = jnp.float32
BF16 = jnp.bfloat16
HI = lax.Precision.HIGHEST

GRID_W = 64
RW_HEAD = 64
HG_HEAD = 128
CHUNK = 64
TILE = 512
OTILE = 512
HG_TILES_PER_STEP = 3
RW_SCAN_CHUNKS = 4
RW_SCAN_BATCH = 1
HG_SCAN_CHUNKS = 4
RW_GROUP_SPLITS = 1
HG_HEAD_SPLITS = 4
NORM_EPS = 1e-6
LN_X_EPS = 64e-5
LANES = 128
VMEM_LIMIT = 56 * 1024 * 1024


def _silu(x):
    return x * jax.nn.sigmoid(x)


def _dot(a, b):
    return jnp.dot(a, b, preferred_element_type=F32)


def _mod_kernel(s_ref, w_ref, b_ref, o_ref):
    s = _silu(s_ref[...])
    o_ref[0] = jnp.dot(s, w_ref[0], precision=HI, preferred_element_type=F32) + b_ref[0]


def _modulation(cond, mod_w, mod_b):
    depth, d, d3 = mod_w.shape
    rows = cond.shape[0]
    nb = d3 // d
    return pl.pallas_call(
        _mod_kernel,
        grid=(depth, nb),
        in_specs=[
            pl.BlockSpec((rows, d), lambda i, n: (0, 0)),
            pl.BlockSpec((1, d, d), lambda i, n: (i, 0, n)),
            pl.BlockSpec((1, 1, d), lambda i, n: (i, 0, n)),
        ],
        out_specs=pl.BlockSpec((1, rows, d), lambda i, n: (i, 0, n)),
        out_shape=jax.ShapeDtypeStruct((depth, rows, d3), F32),
        name="adaln_modulation",
        compiler_params=pltpu.CompilerParams(vmem_limit_bytes=VMEM_LIMIT),
    )(cond, mod_w, mod_b.reshape(depth, 1, d3))


def _adaln(xb, g, scale, shift):
    ms = jnp.mean(xb * xb, axis=-1, keepdims=True)
    return xb * lax.rsqrt(ms + NORM_EPS) * g * (1.0 + scale) + shift


def _store_groups(o_ref, lead, val):
    for gq in range(val.shape[1] // LANES):
        o_ref[lead + (gq,)] = val[:, gq * LANES:(gq + 1) * LANES].astype(o_ref.dtype)


def _load_groups(ref, lead):
    n = ref.shape[len(lead)]
    return jnp.concatenate([ref[lead + (gq,)] for gq in range(n)], axis=1)


def _head_sum_matrix(width, head):
    r = lax.broadcasted_iota(jnp.int32, (width, width), 0) // head
    c = lax.broadcasted_iota(jnp.int32, (width, width), 1) // head
    return (r == c).astype(BF16)


def _rw_feat_kernel(n_lat_tiles, ctx_len, has_vres, *refs):
    (x_ref, xp_ref, xn_ref, mod_ref, g_ref, mix_ref, w1_ref, a1_ref, v1_ref,
     proj_ref, w2_ref, a2_ref, v2_ref, vec_ref, hs_ref) = refs[:15]
    pos = 15
    vf_ref = None
    if has_vres:
        vf_ref = refs[pos]
        pos += 1
    (r_o, v_o, kk_o, z_o, bv_o, lw_o, bb_o, kd_o) = refs[pos:pos + 8]
    hbuf, xm, tw, ta, tv = refs[pos + 8:]

    j = pl.program_id(1)
    n = pl.program_id(2)
    d = x_ref.shape[-1]
    q4 = d // 4
    nblk = TILE // GRID_W

    @pl.when(n == 0)
    def _prologue():
        m = mod_ref[0, 0]
        shift, scale = m[:, :d], m[:, d:2 * d]
        g = g_ref[...]
        is_lat = j < n_lat_tiles
        prev_ok = jnp.logical_and(j >= 1, is_lat)
        next_ok = j < n_lat_tiles - 1

        def norm_rows(rb, c):
            r0 = pl.multiple_of(rb * GRID_W, GRID_W)
            hbuf[pl.ds(r0 + GRID_W, GRID_W), :] = _adaln(x_ref[0, pl.ds(r0, GRID_W), :], g, scale, shift)
            return c

        lax.fori_loop(0, nblk, norm_rows, 0)
        hp = _adaln(xp_ref[0], g, scale, shift)
        hbuf[0:GRID_W, :] = jnp.where(prev_ok, hp, 0.0)
        hn = _adaln(xn_ref[0], g, scale, shift)
        hbuf[TILE + GRID_W:TILE + 2 * GRID_W, :] = jnp.where(next_ok, hn, 0.0)

        row = lax.broadcasted_iota(jnp.int32, (GRID_W, 1), 0)

        def store_mixed(r0, cs, h_c, hs_c):
            dd = hs_c - h_c
            for k in range(6):
                xm[k, pl.ds(r0, GRID_W), cs] = (h_c + dd * mix_ref[k:k + 1, cs]).astype(BF16)

        def mix_lat(rb, c):
            r0 = pl.multiple_of(rb * GRID_W, GRID_W)
            for qi in range(4):
                cs = slice(qi * q4, (qi + 1) * q4)
                h_c = hbuf[pl.ds(r0 + GRID_W, GRID_W), cs]
                if qi == 0:
                    hs_c = jnp.where(row == 0, 0.0, pltpu.roll(h_c, 1, axis=0))
                elif qi == 1:
                    hs_c = jnp.where(row == GRID_W - 1, 0.0, pltpu.roll(h_c, GRID_W - 1, axis=0))
                elif qi == 2:
                    hs_c = hbuf[pl.ds(r0, GRID_W), cs]
                else:
                    hs_c = hbuf[pl.ds(r0 + 2 * GRID_W, GRID_W), cs]
                store_mixed(r0, cs, h_c, hs_c)
            return c

        def mix_ctx(rb, c):
            r0 = pl.multiple_of(rb * GRID_W, GRID_W)
            t = row + rb * GRID_W
            for qi in range(4):
                cs = slice(qi * q4, (qi + 1) * q4)
                h_c = hbuf[pl.ds(r0 + GRID_W, GRID_W), cs]
                if qi < 2:
                    edge = hbuf[pl.ds(r0 + GRID_W - 1, 1), cs]
                    hs_c = jnp.where(row == 0, edge, pltpu.roll(h_c, 1, axis=0))
                else:
                    edge = hbuf[pl.ds(r0 + 2 * GRID_W, 1), cs]
                    hs_c = jnp.where(row == GRID_W - 1, edge, pltpu.roll(h_c, GRID_W - 1, axis=0))
                    hs_c = jnp.where(t == ctx_len - 1, 0.0, hs_c)
                store_mixed(r0, cs, h_c, hs_c)
            return c

        @pl.when(is_lat)
        def _():
            lax.fori_loop(0, nblk, mix_lat, 0)

        @pl.when(jnp.logical_not(is_lat))
        def _():
            lax.fori_loop(0, nblk, mix_ctx, 0)

        tw[...] = jnp.tanh(_dot(xm[1], w1_ref[...])).astype(BF16)
        ta[...] = _dot(xm[4], a1_ref[...]).astype(BF16)
        if has_vres:
            tv[...] = _dot(xm[3], v1_ref[...]).astype(BF16)

    vec = vec_ref[...]
    r = _dot(xm[0], proj_ref[0, 0])
    k = _dot(xm[2], proj_ref[0, 1])
    v = _dot(xm[3], proj_ref[0, 2])
    z = _silu(_dot(xm[5], proj_ref[0, 3]))
    if has_vres:
        gate = jax.nn.sigmoid(vec[4:5] + _dot(tv[...], v2_ref[...]))
        v = v + (_load_groups(vf_ref, (0,)).astype(F32) - v) * gate
    hs = hs_ref[...]
    kk = k * vec[5:6]
    ss = _dot((kk * kk).astype(BF16), hs)
    kk = kk * lax.rsqrt(jnp.maximum(ss, 1e-24))
    decay_scale = -math.exp(-0.5) * math.log2(math.e)
    ksum = None
    for dr in range(2):
        u = vec[dr:dr + 1] + _dot(tw[...], w2_ref[dr])
        _store_groups(lw_o, (dr, 0), decay_scale * jax.nn.sigmoid(u))
        a = jax.nn.sigmoid(vec[2 + dr:3 + dr] + _dot(ta[...], a2_ref[dr]))
        kd = k * (1.0 + (a - 1.0) * vec[6:7])
        _store_groups(bb_o, (dr, 0), kk * a)
        _store_groups(kd_o, (dr, 0), kd)
        ksum = kd if ksum is None else ksum + kd
    bonus = _dot((r * ksum * vec[7:8]).astype(BF16), hs)
    _store_groups(r_o, (0,), r)
    _store_groups(v_o, (0,), v)
    _store_groups(kk_o, (0,), kk)
    _store_groups(z_o, (0,), z)
    _store_groups(bv_o, (0,), bonus * v)


def _group_major_specs(b, tp, di, tn):
    gq = tn // LANES
    one = pl.BlockSpec((1, gq, TILE, LANES), lambda bi, j, n: (bi, n, j, 0))
    two = pl.BlockSpec((2, 1, gq, TILE, LANES), lambda bi, j, n: (0, bi, n, j, 0))
    s1 = lambda dt: jax.ShapeDtypeStruct((b, di // LANES, tp, LANES), dt)
    s2 = lambda dt: jax.ShapeDtypeStruct((2, b, di // LANES, tp, LANES), dt)
    return one, two, s1, s2


def _rw_features(xs, mod, pre_g, p, vf, *, n_lat_tiles, ctx_len, tn=256):
    b, tp, d = xs.shape
    di = p["proj"].shape[-1]
    nt = di // tn
    ntiles = tp // TILE
    nb64 = tp // GRID_W
    per = TILE // GRID_W
    has_vres = vf is not None

    def tok(bi, j, n):
        return (bi, j, 0)

    in_specs = [
        pl.BlockSpec((1, TILE, d), tok),
        pl.BlockSpec((1, GRID_W, d), lambda bi, j, n: (bi, jnp.maximum(j * per - 1, 0), 0)),
        pl.BlockSpec((1, GRID_W, d), lambda bi, j, n: (bi, jnp.minimum((j + 1) * per, nb64 - 1), 0)),
        pl.BlockSpec((1, 1, 1, mod.shape[-1]), lambda bi, j, n: (bi, j, 0, 0)),
        pl.BlockSpec((1, d), lambda bi, j, n: (0, 0)),
        pl.BlockSpec((6, d), lambda bi, j, n: (0, 0)),
        pl.BlockSpec((d, LANES), lambda bi, j, n: (0, 0)),
        pl.BlockSpec((d, LANES), lambda bi, j, n: (0, 0)),
        pl.BlockSpec((d, LANES), lambda bi, j, n: (0, 0)),
        pl.BlockSpec((1, 4, d, tn), lambda bi, j, n: (0, 0, 0, n)),
        pl.BlockSpec((2, LANES, tn), lambda bi, j, n: (0, 0, n)),
        pl.BlockSpec((2, LANES, tn), lambda bi, j, n: (0, 0, n)),
        pl.BlockSpec((LANES, tn), lambda bi, j, n: (0, n)),
        pl.BlockSpec((8, tn), lambda bi, j, n: (0, n)),
        pl.BlockSpec((tn, tn), lambda bi, j, n: (0, 0)),
    ]
    args = [xs, xs, xs, mod, pre_g, p["mix"], p["w1"], p["a1"], p["v1"], p["proj"][None], p["w2"], p["a2"],
            p["v2"], p["vec"], _head_sum_matrix(tn, RW_HEAD)]
    one, two, s1, s2 = _group_major_specs(b, tp, di, tn)
    if has_vres:
        in_specs.append(one)
        args.append(vf)
    return pl.pallas_call(
        functools.partial(_rw_feat_kernel, n_lat_tiles, ctx_len, has_vres),
        grid=(b, ntiles, nt),
        in_specs=in_specs,
        out_specs=[one, one, one, one, one, two, two, two],
        out_shape=[s1(BF16), s1(BF16), s1(BF16), s1(BF16), s1(BF16), s2(F32), s2(BF16), s2(BF16)],
        scratch_shapes=[
            pltpu.VMEM((TILE + 2 * GRID_W, d), F32),
            pltpu.VMEM((6, TILE, d), BF16),
            pltpu.VMEM((TILE, LANES), BF16),
            pltpu.VMEM((TILE, LANES), BF16),
            pltpu.VMEM((TILE, LANES), BF16),
        ],
        name="rwkv7_features",
        compiler_params=pltpu.CompilerParams(
            dimension_semantics=("parallel", "parallel", "arbitrary"), vmem_limit_bytes=VMEM_LIMIT),
    )(*args)


def _bmm(a, b):
    return lax.dot_general(a, b, (((2,), (1,)), ((0,), (0,))), preferred_element_type=F32)


def _bmm_nt(a, b):
    return lax.dot_general(a, b, (((2,), (2,)), ((0,), (0,))), preferred_element_type=F32)


def _bmm_tn(a, b):
    return lax.dot_general(a, b, (((1,), (1,)), ((0,), (0,))), preferred_element_type=F32)


def _chunk_cumsum(x, rev):
    c = x.shape[1]
    row = lax.broadcasted_iota(jnp.int32, (1, c, 1), 1)
    s = 1
    while s < c:
        if rev:
            x = x + jnp.where(row < c - s, pltpu.roll(x, c - s, axis=1), 0.0)
        else:
            x = x + jnp.where(row >= s, pltpu.roll(x, s, axis=1), 0.0)
        s *= 2
    return x


def _scan_chunk_maps(n_lat_chunks, n_ctx_chunks):
    total = n_lat_chunks + n_ctx_chunks

    def fwd(s):
        return jnp.where(s < total, (s + n_lat_chunks) % total, total)

    def bwd(s):
        return jnp.where(s < total, total - 1 - s, total)

    return total, fwd, bwd


def _scan_specs(b, ng, tp, n_lat_chunks, n_ctx_chunks, n_one, n_two, nb, nchunks):
    assert n_lat_chunks % nchunks == 0 and n_ctx_chunks % nchunks == 0 and b % nb == 0
    rows = CHUNK * nchunks
    total, fwd, bwd = _scan_chunk_maps(n_lat_chunks // nchunks, n_ctx_chunks // nchunks)
    assert tp == (total + 1) * rows
    in_specs = []
    for dr, cm in enumerate((fwd, bwd)):
        one = pl.BlockSpec((nb, ng, rows, LANES), lambda bi, s, cm=cm: (bi, 0, cm(s), 0))
        two = pl.BlockSpec((1, nb, ng, rows, LANES), lambda bi, s, cm=cm, dr=dr: (dr, bi, 0, cm(s), 0))
        in_specs += [one] * n_one + [two] * n_two
    out_specs = [pl.BlockSpec((nb, ng, rows, LANES), lambda bi, s, cm=cm: (bi, 0, cm(s), 0)) for cm in (fwd, bwd)]
    out_shape = [jax.ShapeDtypeStruct((b, ng, tp, LANES), BF16)] * 2
    return (b // nb, total + 1), in_specs, out_specs, out_shape


def _scan_rows(ci, nchunks):
    fwd = pl.ds(pl.multiple_of(ci * CHUNK, CHUNK), CHUNK)
    bwd = pl.ds(pl.multiple_of((nchunks - 1 - ci) * CHUNK, CHUNK), CHUNK)
    return fwd, bwd


def _scan_or_pad(body, nchunks, y_os):
    s = pl.program_id(1)
    last = pl.num_programs(1) - 1

    @pl.when(s < last)
    def _():
        lax.fori_loop(0, nchunks, body, 0)

    @pl.when(s == last)
    def _():
        for y_o in y_os:
            y_o[...] = jnp.zeros_like(y_o)


def _chunk_by_groups(chunk_fn, loaded, splits):
    n = loaded[0][0].shape[0]
    step = n // splits
    parts = [[], []]
    for part in range(splits):
        gs = slice(part * step, (part + 1) * step)
        for dr in range(2):
            parts[dr].append(chunk_fn(*[a[gs] for a in loaded[dr]], rev=(dr == 1)))
    return [tuple(jnp.concatenate([p[i] for p in parts[dr]], axis=0) for i in range(2)) for dr in range(2)]


def _load_chunk(ref, lead, rs, dtype=F32):
    x = ref[lead + (slice(None), slice(None), rs)]
    return x.reshape((x.shape[0] * x.shape[1],) + x.shape[2:]).astype(dtype)


def _rw_chunk(r, v, kk, bb, kd, lw, st, rev):
    c = CHUNK
    half = RW_HEAD
    npair = r.shape[0]
    ri = lax.broadcasted_iota(jnp.int32, (1, c, LANES), 1)
    ci = lax.broadcasted_iota(jnp.int32, (1, c, LANES), 2) % half
    head_a = lax.broadcasted_iota(jnp.int32, (1, 1, LANES), 2) < half

    def split_rows(x):
        zero = jnp.zeros_like(x)
        return jnp.concatenate([jnp.where(head_a, x, zero), jnp.where(head_a, zero, x)], axis=1)

    if rev:
        incl, strict = ci >= ri, ci > ri
    else:
        incl, strict = ci <= ri, ci < ri
    cum = _chunk_cumsum(lw, rev)
    tot = cum[:, 0:1] if rev else cum[:, c - 1:c]
    e_in = jnp.exp2(cum)
    e_prev = jnp.exp2(cum - lw)
    e_inv = jnp.exp2(-cum)
    e_out = jnp.exp2(tot - cum)
    rt = (r * e_in).astype(BF16)
    at = (-kk * e_prev).astype(BF16)
    bi = (bb * e_inv).astype(BF16)
    ki = (kd * e_inv).astype(BF16)
    bh = (bb * e_out).astype(BF16)
    kh = (kd * e_out).astype(BF16)
    vb = v.astype(BF16)
    sb = st.astype(BF16)
    lhs = jnp.concatenate([at, rt], axis=1)
    h0 = _bmm_nt(lhs, sb)
    m_b = _bmm_nt(lhs, split_rows(bi))
    m_k = _bmm_nt(lhs, split_rows(ki))
    tri = jnp.concatenate([strict, incl], axis=1)
    m_b = m_b.astype(BF16)
    m_b = jnp.where(tri, m_b, jnp.zeros_like(m_b))
    m_k = m_k.astype(BF16)
    m_k = jnp.where(tri, m_k, jnp.zeros_like(m_k))
    ab16, a_rb = m_b[:, :c], m_b[:, c:]
    zero16 = jnp.zeros_like(ab16)
    a1 = jnp.where(ri // 2 == ci // 2, ab16, zero16)
    pmb = jnp.where(ri == ci, jnp.ones_like(ab16), a1)
    off = jnp.logical_and(ri // 4 == ci // 4, ri // 2 != ci // 2)
    x2 = jnp.where(off, ab16, zero16).astype(F32)
    a1f = a1.astype(F32)
    a1_row = jnp.where(head_a, jnp.sum(jnp.where(head_a, a1f, 0.0), axis=2, keepdims=True),
                       jnp.sum(jnp.where(head_a, 0.0, a1f), axis=2, keepdims=True))
    a1_col = jnp.sum(a1f, axis=1, keepdims=True)
    y2 = x2 + a1_row * pltpu.roll(x2, c - 1 if rev else 1, axis=1)
    d4 = y2 + pltpu.roll(y2, 1 if rev else LANES - 1, axis=2) * a1_col
    pmb = jnp.where(off, d4.astype(BF16), pmb)
    m = 4
    while m < c:
        off = jnp.logical_and(ri // (2 * m) == ci // (2 * m), ri // m != ci // m)
        a_off = jnp.where(off, ab16, zero16)
        delta = _bmm(_bmm(pmb, split_rows(a_off)).astype(BF16), split_rows(pmb))
        pmb = jnp.where(off, delta.astype(BF16), pmb)
        m *= 2
    av = _bmm(m_k, split_rows(vb))
    x = h0[:, :c] + av[:, :c]
    u = _bmm(pmb, split_rows(x.astype(BF16)))
    ub = u.astype(BF16)
    o_full = h0[:, c:] + av[:, c:] + _bmm(a_rb, split_rows(ub))
    upd = _bmm_tn(jnp.concatenate([ub, vb], axis=1), jnp.concatenate([bh, kh], axis=1))
    blk_r = lax.broadcasted_iota(jnp.int32, (1, LANES, LANES), 1) // half
    blk_c = lax.broadcasted_iota(jnp.int32, (1, LANES, LANES), 2) // half
    st_new = st * jnp.exp2(tot) + jnp.where(blk_r == blk_c, upd, 0.0)
    return o_full, st_new


def _store_chunk(o_ref, rs, val):
    nb, ng = o_ref.shape[0], o_ref.shape[1]
    o_ref[:, :, rs] = val.reshape((nb, ng) + val.shape[1:]).astype(o_ref.dtype)


def _rw_scan_kernel(*refs):
    ins = refs[:12]
    y_os = refs[12:14]
    st = refs[14]

    @pl.when(pl.program_id(1) == 0)
    def _():
        st[...] = jnp.zeros_like(st)

    def body(ci, carry):
        rows = _scan_rows(ci, RW_SCAN_CHUNKS)
        loaded = []
        for dr in range(2):
            r_r, v_r, kk_r, bb_r, kd_r, lw_r = ins[6 * dr:6 * dr + 6]
            rs = rows[dr]
            loaded.append((_load_chunk(r_r, (), rs), _load_chunk(v_r, (), rs), _load_chunk(kk_r, (), rs),
                           _load_chunk(bb_r, (0,), rs), _load_chunk(kd_r, (0,), rs), _load_chunk(lw_r, (0,), rs),
                           st[dr]))
        results = _chunk_by_groups(_rw_chunk, loaded, RW_GROUP_SPLITS)
        for dr in range(2):
            _store_chunk(y_os[dr], rows[dr], results[dr][0])
            st[dr] = results[dr][1]
        return carry

    _scan_or_pad(body, RW_SCAN_CHUNKS, y_os)


def _rw_scan(feats, *, n_lat_chunks, n_ctx_chunks):
    r, v, kk, lw, bb, kd = feats
    b, ng, tp, _ = r.shape
    nb = RW_SCAN_BATCH if b % RW_SCAN_BATCH == 0 else 1
    grid, in_specs, out_specs, out_shape = _scan_specs(b, ng, tp, n_lat_chunks, n_ctx_chunks, 3, 3, nb,
                                                       RW_SCAN_CHUNKS)
    return pl.pallas_call(
        _rw_scan_kernel,
        grid=grid,
        in_specs=in_specs,
        out_specs=out_specs,
        out_shape=out_shape,
        scratch_shapes=[pltpu.VMEM((2, nb * ng, LANES, LANES), F32)],
        name="rwkv7_scan",
        compiler_params=pltpu.CompilerParams(
            dimension_semantics=("parallel", "arbitrary"), vmem_limit_bytes=VMEM_LIMIT),
    )(r, v, kk, bb, kd, lw, r, v, kk, bb, kd, lw)


def _residual_tail(yg_bf16, wo_ref, x_ref, mod_ref, pg_ref, o_ref):
    d = x_ref.shape[-1]
    o = _dot(yg_bf16, wo_ref[...])
    ms = jnp.mean(o * o, axis=-1, keepdims=True)
    o = o * lax.rsqrt(ms + NORM_EPS) * pg_ref[...]
    gate = mod_ref[0, 0][:, 2 * d:]
    o_ref[0] = x_ref[0] + gate * o


def _rw_out_kernel(yf_ref, yb_ref, z_ref, bv_ref, lnw_ref, lnb_ref, hs_ref, wo_ref, x_ref, mod_ref, pg_ref,
                   o_ref, yg):
    inv = 1.0 / RW_HEAD
    hs = hs_ref[...]
    per = hs.shape[0] // LANES

    def groups(ref, g0):
        return jnp.concatenate([ref[0, g0 + i] for i in range(per)], axis=1)

    for g0 in range(0, yf_ref.shape[1], per):
        cs = slice(g0 * LANES, (g0 + per) * LANES)
        y = groups(yf_ref, g0).astype(F32) + groups(yb_ref, g0).astype(F32)
        mu = _dot(y.astype(BF16), hs) * inv
        yc = y - mu
        var = _dot((yc * yc).astype(BF16), hs) * inv
        yn = yc * lax.rsqrt(var + LN_X_EPS) * lnw_ref[:, cs] + lnb_ref[:, cs] + groups(bv_ref, g0).astype(F32)
        yg[:, cs] = yn.astype(BF16) * groups(z_ref, g0)
    _residual_tail(yg[...], wo_ref, x_ref, mod_ref, pg_ref, o_ref)


def _hg_out_kernel(yf_ref, yb_ref, sg_ref, gn_ref, wo_ref, x_ref, mod_ref, pg_ref, o_ref, yg):
    for hx in range(yf_ref.shape[1]):
        cs = slice(hx * LANES, (hx + 1) * LANES)
        y = yf_ref[0, hx].astype(F32) + yb_ref[0, hx].astype(F32)
        ms = jnp.mean(y * y, axis=-1, keepdims=True)
        yn = y * lax.rsqrt(ms + NORM_EPS) * gn_ref[...]
        yg[:, cs] = yn.astype(BF16) * sg_ref[0, hx]
    _residual_tail(yg[...], wo_ref, x_ref, mod_ref, pg_ref, o_ref)


def _out_group_spec(di):
    return pl.BlockSpec((1, di // LANES, OTILE, LANES), lambda bi, j: (bi, 0, j, 0))


def _mixer_out(name, kernel_fn, ys, extras, extra_specs, wo, xs, mod, post_g, n_tiles, out_rows):
    b, tp, d = xs.shape
    in_place = out_rows == tp
    assert in_place or out_rows == n_tiles * OTILE
    di = wo.shape[0]
    per = TILE // OTILE
    tokd = pl.BlockSpec((1, OTILE, d), lambda bi, j: (bi, j, 0))
    in_specs = [_out_group_spec(di)] * 2 + extra_specs + [
        pl.BlockSpec((di, d), lambda bi, j: (0, 0)),
        tokd,
        pl.BlockSpec((1, 1, 1, mod.shape[-1]), lambda bi, j: (bi, j // per, 0, 0)),
        pl.BlockSpec((1, d), lambda bi, j: (0, 0)),
    ]
    n_in = len(in_specs)
    return pl.pallas_call(
        kernel_fn,
        grid=(b, n_tiles),
        in_specs=in_specs,
        out_specs=tokd,
        out_shape=jax.ShapeDtypeStruct((b, out_rows, d), F32),
        scratch_shapes=[pltpu.VMEM((OTILE, di), BF16)],
        input_output_aliases={n_in - 3: 0} if in_place else {},
        name=name,
        compiler_params=pltpu.CompilerParams(
            dimension_semantics=("parallel", "parallel"), vmem_limit_bytes=VMEM_LIMIT),
    )(*ys, *extras, wo, xs, mod, post_g)


def _hg_feat_kernel(x_ref, mod_ref, g_ref, wq_ref, wf0_ref, wf1_ref, wi_ref, wg_ref, lb_ref,
                    q_o, i_o, sg_o, lf_o, kd_o, hbf):
    n = pl.program_id(1)
    j = pl.program_id(2)
    d = x_ref.shape[-1]
    rows = x_ref.shape[1]
    base = pl.multiple_of(j * rows, rows)

    @pl.when(n == 0)
    def _prologue():
        g = g_ref[...]
        for t in range(rows // TILE):
            m = mod_ref[0, t]
            shift, scale = m[:, :d], m[:, d:2 * d]

            def norm_rows(rb, c, t=t, shift=shift, scale=scale):
                r0 = pl.multiple_of(t * TILE + rb * GRID_W, GRID_W)
                hbf[pl.ds(base + r0, GRID_W), :] = _adaln(x_ref[0, pl.ds(r0, GRID_W), :], g, scale,
                                                          shift).astype(BF16)
                return c

            lax.fori_loop(0, TILE // GRID_W, norm_rows, 0)

    h = hbf[pl.ds(base, rows), :]
    lb = lb_ref[...]
    _store_groups(q_o, (0,), _silu(_dot(h, wq_ref[...])))
    for dr, wf_ref in enumerate((wf0_ref, wf1_ref)):
        f = lb + (1.0 - lb) * jax.nn.sigmoid(_dot(h, wf_ref[...]))
        _store_groups(kd_o, (dr, 0), 1.0 - f)
        _store_groups(lf_o, (dr, 0), jnp.log2(f))
    _store_groups(i_o, (0,), _dot(h, wi_ref[...]))
    _store_groups(sg_o, (0,), _silu(_dot(h, wg_ref[...])))


def _hg_features(xs, mod, pre_g, w_in, lb, *, tn=256):
    b, tp, d = xs.shape
    di = w_in.shape[-1] // 5
    nt = di // tn
    per = HG_TILES_PER_STEP if (tp // TILE) % HG_TILES_PER_STEP == 0 else 1
    rows = per * TILE
    nsteps = tp // rows
    _, _, s1, s2 = _group_major_specs(b, tp, di, tn)
    gq = tn // LANES
    one = pl.BlockSpec((1, gq, rows, LANES), lambda bi, n, j: (bi, n, j, 0))
    two = pl.BlockSpec((2, 1, gq, rows, LANES), lambda bi, n, j: (0, bi, n, j, 0))

    def tok_tile(n, j):
        return jnp.where(n == 0, j, nsteps - 1)

    return pl.pallas_call(
        _hg_feat_kernel,
        grid=(b, nt, nsteps),
        in_specs=[
            pl.BlockSpec((1, rows, d), lambda bi, n, j: (bi, tok_tile(n, j), 0)),
            pl.BlockSpec((1, per, 1, mod.shape[-1]), lambda bi, n, j: (bi, tok_tile(n, j), 0, 0)),
            pl.BlockSpec((1, d), lambda bi, n, j: (0, 0)),
            *[pl.BlockSpec((d, tn), lambda bi, n, j, part=part: (0, part * nt + n)) for part in range(5)],
            pl.BlockSpec((1, tn), lambda bi, n, j: (0, n)),
        ],
        out_specs=[one, one, one, two, two],
        out_shape=[s1(BF16), s1(BF16), s1(BF16), s2(F32), s2(BF16)],
        scratch_shapes=[pltpu.VMEM((tp, d), BF16)],
        name="hgrn2_features",
        compiler_params=pltpu.CompilerParams(
            dimension_semantics=("parallel", "arbitrary", "arbitrary"), vmem_limit_bytes=VMEM_LIMIT),
    )(xs, mod, pre_g, w_in, w_in, w_in, w_in, w_in, lb)


def _hg_chunk(q, k, v, lf2, st, rev):
    c = CHUNK
    sub = 8
    nh = q.shape[0]
    row = lax.broadcasted_iota(jnp.int32, (1, c, 1), 1)
    ri = lax.broadcasted_iota(jnp.int32, (1, c, c), 1)
    ci = lax.broadcasted_iota(jnp.int32, (1, c, c), 2)
    p = lf2
    diag = jnp.sum(q * k, axis=2, keepdims=True)
    att = jnp.where(ri == ci, diag, jnp.zeros((nh, c, c), F32))
    m = 1
    while m < c:
        pair = jnp.logical_and(ri // (2 * m) == ci // (2 * m), ri // m != ci // m)
        if m >= sub:
            zeros = jnp.zeros((nh, m, LANES), BF16)
            q_parts, k_parts, p_parts = [], [], []
            for blk in range(c // (2 * m)):
                lo = slice(blk * 2 * m, blk * 2 * m + m)
                hi = slice(blk * 2 * m + m, (blk + 1) * 2 * m)
                early, later = (hi, lo) if rev else (lo, hi)
                bidx = blk * 2 * m + (m if rev else m - 1)
                tt = p[:, bidx:bidx + 1, :]
                q_l = (q[:, later] * jnp.exp2(p[:, later])).astype(BF16)
                k_e = (k[:, early] * jnp.exp2(tt - p[:, early])).astype(BF16)
                if rev:
                    q_parts += [q_l, zeros]
                    k_parts += [zeros, k_e]
                    p_parts += [p[:, later] + tt, p[:, early]]
                else:
                    q_parts += [zeros, q_l]
                    k_parts += [k_e, zeros]
                    p_parts += [p[:, early], p[:, later] + tt]
            mm = _bmm_nt(jnp.concatenate(q_parts, axis=1), jnp.concatenate(k_parts, axis=1))
            p = jnp.concatenate(p_parts, axis=1)
        else:
            in_hi = ((row // m) % 2) == 1
            late = jnp.logical_not(in_hi) if rev else in_hi
            if m == 1:
                tt = jnp.where(late, pltpu.roll(p, c - 1 if rev else 1, axis=1), p)
            else:
                p4 = p.reshape(nh, c // sub, sub, LANES)
                srow = lax.broadcasted_iota(jnp.int32, (1, 1, sub, 1), 2)
                tt4 = None
                for blk in range(sub // (2 * m)):
                    bidx = blk * 2 * m + (m if rev else m - 1)
                    cand = jnp.broadcast_to(p4[:, :, bidx:bidx + 1, :], p4.shape)
                    tt4 = cand if tt4 is None else jnp.where(srow // (2 * m) == blk, cand, tt4)
                tt = tt4.reshape(nh, c, LANES)
            ee = jnp.exp2(jnp.where(late, p, tt - p))
            qt = jnp.where(late, q * ee, 0.0).astype(BF16)
            kt = jnp.where(late, 0.0, k * ee).astype(BF16)
            mm = _bmm_nt(qt, kt)
            p = jnp.where(late, p + tt, p)
        att = jnp.where(pair, mm, att)
        m *= 2
    tot = p[:, 0:1] if rev else p[:, c - 1:c]
    vb = v.astype(BF16)
    y = _bmm(att.astype(BF16), vb) + _bmm_nt((q * jnp.exp2(p)).astype(BF16), st.astype(BF16))
    st_new = st * jnp.exp2(tot) + _bmm_tn(vb, (k * jnp.exp2(tot - p)).astype(BF16))
    return y, st_new


def _hg_scan_kernel(*refs):
    ins = refs[:8]
    y_os = refs[8:10]
    st = refs[10]

    @pl.when(pl.program_id(1) == 0)
    def _():
        st[...] = jnp.zeros_like(st)

    def body(ci, carry):
        rows = _scan_rows(ci, HG_SCAN_CHUNKS)
        loaded = []
        for dr in range(2):
            q_r, v_r, kd_r, lf_r = ins[4 * dr:4 * dr + 4]
            rs = rows[dr]
            loaded.append((_load_chunk(q_r, (), rs), _load_chunk(kd_r, (0,), rs), _load_chunk(v_r, (), rs, BF16),
                           _load_chunk(lf_r, (0,), rs), st[dr]))
        results = _chunk_by_groups(_hg_chunk, loaded, HG_HEAD_SPLITS)
        for dr in range(2):
            _store_chunk(y_os[dr], rows[dr], results[dr][0])
            st[dr] = results[dr][1]
        return carry

    _scan_or_pad(body, HG_SCAN_CHUNKS, y_os)


def _hg_scan(q, v, lf, kd, *, n_lat_chunks, n_ctx_chunks):
    b, ng, tp, _ = q.shape
    grid, in_specs, out_specs, out_shape = _scan_specs(b, ng, tp, n_lat_chunks, n_ctx_chunks, 2, 2, 1,
                                                       HG_SCAN_CHUNKS)
    return pl.pallas_call(
        _hg_scan_kernel,
        grid=grid,
        in_specs=in_specs,
        out_specs=out_specs,
        out_shape=out_shape,
        scratch_shapes=[pltpu.VMEM((2, ng, LANES, LANES), F32)],
        name="hgrn2_scan",
        compiler_params=pltpu.CompilerParams(
            dimension_semantics=("parallel", "arbitrary"), vmem_limit_bytes=VMEM_LIMIT),
    )(q, v, kd, lf, q, v, kd, lf)


def _pad_rows(w, rows):
    return jnp.pad(w, ((0, rows - w.shape[0]), (0, 0)))


def _pad_cols(w, cols):
    return jnp.pad(w, ((0, 0), (0, cols - w.shape[1])))


def kernel(x, c, ctx, c_ctx, mod_w, mod_b, pre_g, post_g, rw_mix, rw_proj, rw_wo, rw_w0, rw_w1, rw_w2, rw_a0,
           rw_a1, rw_a2, rw_v0, rw_v1, rw_v2, rw_kk, rw_ka, rw_rk, rw_lnw, rw_lnb, hg_win, hg_wo, hg_gn, hg_lb):
    b, seq, d = x.shape
    ctx_len = ctx.shape[1]
    depth = mod_w.shape[0]
    di = rw_proj.shape[-1]
    assert seq % TILE == 0 and ctx_len % CHUNK == 0 and ctx_len <= TILE and seq % GRID_W == 0
    n_lat_tiles = seq // TILE
    tp = seq + TILE
    n_lat_chunks, n_ctx_chunks = seq // CHUNK, ctx_len // CHUNK

    xs = jnp.concatenate([x, ctx, jnp.zeros((b, tp - seq - ctx_len, d), x.dtype)], axis=1)

    rows = ((b + 1 + 7) // 8) * 8
    cond = jnp.concatenate([c, c_ctx[None], jnp.zeros((rows - b - 1, d), c.dtype)], axis=0)
    mod_all = _modulation(cond, mod_w, mod_b)
    mod_lat = jnp.broadcast_to(mod_all[:, :b, None, :], (depth, b, n_lat_tiles, 3 * d))
    mod_ctx = jnp.broadcast_to(mod_all[:, b:b + 1, None, :], (depth, b, 1, 3 * d))
    mod_t = jnp.concatenate([mod_lat, mod_ctx], axis=2)[:, :, :, None, :]

    p_lb = jax.nn.softmax(hg_lb.astype(F32), axis=0)
    lb_all = jnp.cumsum(p_lb, axis=0) - p_lb[0]

    vf = None
    for i in range(depth):
        ctx_out = i < depth - 1
        out_rows = tp if ctx_out else seq
        n_out_tiles = out_rows // OTILE
        j = i // 2
        mod_i = mod_t[i]
        pg = pre_g[i][None]
        if i % 2 == 0:
            r64 = rw_w1.shape[-1]
            w2 = jnp.stack([jnp.pad(rw_w2[j, 0], ((0, LANES - r64), (0, 0))),
                            jnp.pad(rw_w2[j, 1], ((r64, LANES - 2 * r64), (0, 0)))])
            a2 = jnp.stack([jnp.pad(rw_a2[j, 0], ((0, LANES - r64), (0, 0))),
                            jnp.pad(rw_a2[j, 1], ((r64, LANES - 2 * r64), (0, 0)))])
            if j == 0:
                v0 = jnp.zeros((di,), F32)
                v1 = jnp.zeros((d, LANES), BF16)
                v2 = jnp.zeros((LANES, di), BF16)
            else:
                v0 = rw_v0[j - 1]
                v1 = _pad_cols(rw_v1[j - 1], LANES).astype(BF16)
                v2 = _pad_rows(rw_v2[j - 1], LANES).astype(BF16)
            p = {
                "mix": rw_mix[j],
                "proj": rw_proj[j].astype(BF16),
                "w1": jnp.concatenate([rw_w1[j, 0], rw_w1[j, 1]], axis=1).astype(BF16),
                "a1": jnp.concatenate([rw_a1[j, 0], rw_a1[j, 1]], axis=1).astype(BF16),
                "v1": v1,
                "w2": w2.astype(BF16),
                "a2": a2.astype(BF16),
                "v2": v2,
                "vec": jnp.stack([rw_w0[j, 0], rw_w0[j, 1], rw_a0[j, 0], rw_a0[j, 1], v0, rw_kk[j], rw_ka[j],
                                  rw_rk[j]]),
            }
            r, v, kk, z, bv, lw, bb, kd = _rw_features(xs, mod_i, pg, p, vf if j > 0 else None,
                                                       n_lat_tiles=n_lat_tiles, ctx_len=ctx_len)
            if j == 0:
                vf = v
            yf, yb = _rw_scan((r, v, kk, lw, bb, kd), n_lat_chunks=n_lat_chunks, n_ctx_chunks=n_ctx_chunks)
            gspec = _out_group_spec(di)
            extras = [z, bv, rw_lnw[j][None], rw_lnb[j][None], _head_sum_matrix(2 * LANES, RW_HEAD)]
            especs = [gspec, gspec, pl.BlockSpec((1, di), lambda bi, jj: (0, 0)),
                      pl.BlockSpec((1, di), lambda bi, jj: (0, 0)), pl.BlockSpec((2 * LANES, 2 * LANES), lambda bi, jj: (0, 0))]
            xs = _mixer_out("rwkv7_out", _rw_out_kernel, (yf, yb), extras, especs, rw_wo[j].astype(BF16), xs, mod_i,
                            post_g[i][None], n_out_tiles, out_rows)
        else:
            q, iv, sg, lf, kd = _hg_features(xs, mod_i, pg, hg_win[j].astype(BF16), lb_all[i][None])
            yf, yb = _hg_scan(q, iv, lf, kd, n_lat_chunks=n_lat_chunks, n_ctx_chunks=n_ctx_chunks)
            extras = [sg, hg_gn[j][None]]
            especs = [_out_group_spec(di), pl.BlockSpec((1, HG_HEAD), lambda bi, jj: (0, 0))]
            xs = _mixer_out("hgrn2_out", _hg_out_kernel, (yf, yb), extras, especs, hg_wo[j].astype(BF16), xs, mod_i,
                            post_g[i][None], n_out_tiles, out_rows)
    return xs
```

```python
import functools
import math

import jax
import jax.numpy as jnp
from jax import lax
from jax.experimental import pallas as pl
from jax.experimental.pallas import tpu as pltpu

F32 = jnp.float32
BF16 = jnp.bfloat16
HI = lax.Precision.HIGHEST

GRID_W = 64
RW_HEAD = 64
HG_HEAD = 128
CHUNK = 64
TILE = 512
OTILE = 512
HG_TILES_PER_STEP = 3
RW_SCAN_CHUNKS = 4
HG_SCAN_CHUNKS = 4
HG_HEAD_SPLITS = 4
NORM_EPS = 1e-6
LN_X_EPS = 64e-5
LANES = 128
SUBLANES = 8
V7X_VMEM_BYTES = 64 * 1024 * 1024
VMEM_LIMIT = V7X_VMEM_BYTES * 3 // 4


def _silu(x):
    return x * jax.nn.sigmoid(x)


def _dot(a, b):
    return jnp.dot(a, b, preferred_element_type=F32)


def _mod_kernel(s_ref, w_ref, b_ref, o_ref):
    s = _silu(s_ref[...])
    o_ref[0] = jnp.dot(s, w_ref[0], precision=HI, preferred_element_type=F32) + b_ref[0]


def _modulation(cond, mod_w, mod_b):
    depth, d, d3 = mod_w.shape
    rows = cond.shape[0]
    nb = d3 // d
    return pl.pallas_call(
        _mod_kernel,
        grid=(depth, nb),
        in_specs=[
            pl.BlockSpec((rows, d), lambda i, n: (0, 0)),
            pl.BlockSpec((1, d, d), lambda i, n: (i, 0, n)),
            pl.BlockSpec((1, 1, d), lambda i, n: (i, 0, n)),
        ],
        out_specs=pl.BlockSpec((1, rows, d), lambda i, n: (i, 0, n)),
        out_shape=jax.ShapeDtypeStruct((depth, rows, d3), F32),
        name="adaln_modulation",
        compiler_params=pltpu.CompilerParams(vmem_limit_bytes=VMEM_LIMIT),
    )(cond, mod_w, mod_b.reshape(depth, 1, d3))


def _adaln(xb, g, scale, shift):
    ms = jnp.mean(xb * xb, axis=-1, keepdims=True)
    return xb * lax.rsqrt(ms + NORM_EPS) * g * (1.0 + scale) + shift


def _store_groups(o_ref, lead, val):
    for gq in range(val.shape[1] // LANES):
        o_ref[lead + (gq,)] = val[:, gq * LANES:(gq + 1) * LANES].astype(o_ref.dtype)


def _load_groups(ref, lead):
    n = ref.shape[len(lead)]
    return jnp.concatenate([ref[lead + (gq,)] for gq in range(n)], axis=1)


def _head_sum_matrix(width, head):
    r = lax.broadcasted_iota(jnp.int32, (width, width), 0) // head
    c = lax.broadcasted_iota(jnp.int32, (width, width), 1) // head
    return (r == c).astype(BF16)


def _rw_feat_kernel(n_lat_tiles, ctx_len, has_vres, *refs):
    (x_ref, xp_ref, xn_ref, mod_ref, g_ref, mix_ref, w1_ref, a1_ref, v1_ref,
     proj_ref, w2_ref, a2_ref, v2_ref, vec_ref, hs_ref) = refs[:15]
    pos = 15
    vf_ref = None
    if has_vres:
        vf_ref = refs[pos]
        pos += 1
    (r_o, v_o, kk_o, z_o, bv_o, lw_o, bb_o, kd_o) = refs[pos:pos + 8]
    hbuf, xm, tw, ta, tv = refs[pos + 8:]

    j = pl.program_id(1)
    n = pl.program_id(2)
    d = x_ref.shape[-1]
    q4 = d // 4
    nblk = TILE // GRID_W

    @pl.when(n == 0)
    def _prologue():
        m = mod_ref[0, 0]
        shift, scale = m[:, :d], m[:, d:2 * d]
        g = g_ref[...]
        is_lat = j < n_lat_tiles
        prev_ok = jnp.logical_and(j >= 1, is_lat)
        next_ok = j < n_lat_tiles - 1

        def norm_rows(rb, c):
            r0 = pl.multiple_of(rb * GRID_W, GRID_W)
            hbuf[pl.ds(r0 + GRID_W, GRID_W), :] = _adaln(x_ref[0, pl.ds(r0, GRID_W), :], g, scale, shift)
            return c

        lax.fori_loop(0, nblk, norm_rows, 0)
        hp = _adaln(xp_ref[0], g, scale, shift)
        hbuf[0:GRID_W, :] = jnp.where(prev_ok, hp, 0.0)
        hn = _adaln(xn_ref[0], g, scale, shift)
        hbuf[TILE + GRID_W:TILE + 2 * GRID_W, :] = jnp.where(next_ok, hn, 0.0)

        row = lax.broadcasted_iota(jnp.int32, (GRID_W, 1), 0)

        def store_mixed(r0, cs, h_c, hs_c):
            dd = hs_c - h_c
            for k in range(6):
                xm[k, pl.ds(r0, GRID_W), cs] = (h_c + dd * mix_ref[k:k + 1, cs]).astype(BF16)

        def mix_lat(rb, c):
            r0 = pl.multiple_of(rb * GRID_W, GRID_W)
            for qi in range(4):
                cs = slice(qi * q4, (qi + 1) * q4)
                h_c = hbuf[pl.ds(r0 + GRID_W, GRID_W), cs]
                if qi == 0:
                    hs_c = jnp.where(row == 0, 0.0, pltpu.roll(h_c, 1, axis=0))
                elif qi == 1:
                    hs_c = jnp.where(row == GRID_W - 1, 0.0, pltpu.roll(h_c, GRID_W - 1, axis=0))
                elif qi == 2:
                    hs_c = hbuf[pl.ds(r0, GRID_W), cs]
                else:
                    hs_c = hbuf[pl.ds(r0 + 2 * GRID_W, GRID_W), cs]
                store_mixed(r0, cs, h_c, hs_c)
            return c

        def mix_ctx(rb, c):
            r0 = pl.multiple_of(rb * GRID_W, GRID_W)
            t = row + rb * GRID_W
            for qi in range(4):
                cs = slice(qi * q4, (qi + 1) * q4)
                h_c = hbuf[pl.ds(r0 + GRID_W, GRID_W), cs]
                if qi < 2:
                    edge = hbuf[pl.ds(r0 + GRID_W - 1, 1), cs]
                    hs_c = jnp.where(row == 0, edge, pltpu.roll(h_c, 1, axis=0))
                else:
                    edge = hbuf[pl.ds(r0 + 2 * GRID_W, 1), cs]
                    hs_c = jnp.where(row == GRID_W - 1, edge, pltpu.roll(h_c, GRID_W - 1, axis=0))
                    hs_c = jnp.where(t == ctx_len - 1, 0.0, hs_c)
                store_mixed(r0, cs, h_c, hs_c)
            return c

        @pl.when(is_lat)
        def _():
            lax.fori_loop(0, nblk, mix_lat, 0)

        @pl.when(jnp.logical_not(is_lat))
        def _():
            lax.fori_loop(0, nblk, mix_ctx, 0)

        tw[...] = jnp.tanh(_dot(xm[1], w1_ref[...])).astype(BF16)
        ta[...] = _dot(xm[4], a1_ref[...]).astype(BF16)
        if has_vres:
            tv[...] = _dot(xm[3], v1_ref[...]).astype(BF16)

    vec = vec_ref[...]
    r = _dot(xm[0], proj_ref[0, 0])
    k = _dot(xm[2], proj_ref[0, 1])
    v = _dot(xm[3], proj_ref[0, 2])
    z = _silu(_dot(xm[5], proj_ref[0, 3]))
    if has_vres:
        gate = jax.nn.sigmoid(vec[4:5] + _dot(tv[...], v2_ref[...]))
        v = v + (_load_groups(vf_ref, (0,)).astype(F32) - v) * gate
    hs = hs_ref[...]
    kk = k * vec[5:6]
    ss = _dot((kk * kk).astype(BF16), hs)
    kk = kk * lax.rsqrt(jnp.maximum(ss, 1e-24))
    decay_scale = -math.exp(-0.5) * math.log2(math.e)
    ksum = None
    for dr in range(2):
        u = vec[dr:dr + 1] + _dot(tw[...], w2_ref[dr])
        _store_groups(lw_o, (dr, 0), decay_scale * jax.nn.sigmoid(u))
        a = jax.nn.sigmoid(vec[2 + dr:3 + dr] + _dot(ta[...], a2_ref[dr]))
        kd = k * (1.0 + (a - 1.0) * vec[6:7])
        _store_groups(bb_o, (dr, 0), kk * a)
        _store_groups(kd_o, (dr, 0), kd)
        ksum = kd if ksum is None else ksum + kd
    bonus = _dot((r * ksum * vec[7:8]).astype(BF16), hs)
    _store_groups(r_o, (0,), r)
    _store_groups(v_o, (0,), v)
    _store_groups(kk_o, (0,), kk)
    _store_groups(z_o, (0,), z)
    _store_groups(bv_o, (0,), bonus * v)


def _group_major_specs(b, tp, di, tn):
    gq = tn // LANES
    one = pl.BlockSpec((1, gq, TILE, LANES), lambda bi, j, n: (bi, n, j, 0))
    two = pl.BlockSpec((2, 1, gq, TILE, LANES), lambda bi, j, n: (0, bi, n, j, 0))
    s1 = lambda dt: jax.ShapeDtypeStruct((b, di // LANES, tp, LANES), dt)
    s2 = lambda dt: jax.ShapeDtypeStruct((2, b, di // LANES, tp, LANES), dt)
    return one, two, s1, s2


def _rw_features(xs, mod, pre_g, p, vf, *, n_lat_tiles, ctx_len, tn=256):
    b, tp, d = xs.shape
    di = p["proj"].shape[-1]
    nt = di // tn
    ntiles = tp // TILE
    nb64 = tp // GRID_W
    per = TILE // GRID_W
    has_vres = vf is not None

    def tok(bi, j, n):
        return (bi, j, 0)

    in_specs = [
        pl.BlockSpec((1, TILE, d), tok),
        pl.BlockSpec((1, GRID_W, d), lambda bi, j, n: (bi, jnp.maximum(j * per - 1, 0), 0)),
        pl.BlockSpec((1, GRID_W, d), lambda bi, j, n: (bi, jnp.minimum((j + 1) * per, nb64 - 1), 0)),
        pl.BlockSpec((1, 1, 1, mod.shape[-1]), lambda bi, j, n: (bi, j, 0, 0)),
        pl.BlockSpec((1, d), lambda bi, j, n: (0, 0)),
        pl.BlockSpec((6, d), lambda bi, j, n: (0, 0)),
        pl.BlockSpec((d, LANES), lambda bi, j, n: (0, 0)),
        pl.BlockSpec((d, LANES), lambda bi, j, n: (0, 0)),
        pl.BlockSpec((d, LANES), lambda bi, j, n: (0, 0)),
        pl.BlockSpec((1, 4, d, tn), lambda bi, j, n: (0, 0, 0, n)),
        pl.BlockSpec((2, LANES, tn), lambda bi, j, n: (0, 0, n)),
        pl.BlockSpec((2, LANES, tn), lambda bi, j, n: (0, 0, n)),
        pl.BlockSpec((LANES, tn), lambda bi, j, n: (0, n)),
        pl.BlockSpec((8, tn), lambda bi, j, n: (0, n)),
        pl.BlockSpec((tn, tn), lambda bi, j, n: (0, 0)),
    ]
    args = [xs, xs, xs, mod, pre_g, p["mix"], p["w1"], p["a1"], p["v1"], p["proj"][None], p["w2"], p["a2"],
            p["v2"], p["vec"], _head_sum_matrix(tn, RW_HEAD)]
    one, two, s1, s2 = _group_major_specs(b, tp, di, tn)
    if has_vres:
        in_specs.append(one)
        args.append(vf)
    return pl.pallas_call(
        functools.partial(_rw_feat_kernel, n_lat_tiles, ctx_len, has_vres),
        grid=(b, ntiles, nt),
        in_specs=in_specs,
        out_specs=[one, one, one, one, one, two, two, two],
        out_shape=[s1(BF16), s1(BF16), s1(BF16), s1(BF16), s1(BF16), s2(F32), s2(BF16), s2(BF16)],
        scratch_shapes=[
            pltpu.VMEM((TILE + 2 * GRID_W, d), F32),
            pltpu.VMEM((6, TILE, d), BF16),
            pltpu.VMEM((TILE, LANES), BF16),
            pltpu.VMEM((TILE, LANES), BF16),
            pltpu.VMEM((TILE, LANES), BF16),
        ],
        name="rwkv7_features",
        compiler_params=pltpu.CompilerParams(
            dimension_semantics=("parallel", "parallel", "arbitrary"), vmem_limit_bytes=VMEM_LIMIT),
    )(*args)


def _bmm(a, b):
    return lax.dot_general(a, b, (((2,), (1,)), ((0,), (0,))), preferred_element_type=F32)


def _bmm_nt(a, b):
    return lax.dot_general(a, b, (((2,), (2,)), ((0,), (0,))), preferred_element_type=F32)


def _bmm_tn(a, b):
    return lax.dot_general(a, b, (((1,), (1,)), ((0,), (0,))), preferred_element_type=F32)


def _chunk_cumsum(x, rev):
    c = x.shape[1]
    row = lax.broadcasted_iota(jnp.int32, (1, c, 1), 1)
    s = 1
    while s < c:
        if rev:
            x = x + jnp.where(row < c - s, pltpu.roll(x, c - s, axis=1), 0.0)
        else:
            x = x + jnp.where(row >= s, pltpu.roll(x, s, axis=1), 0.0)
        s *= 2
    return x


def _scan_chunk_maps(n_lat_chunks, n_ctx_chunks):
    total = n_lat_chunks + n_ctx_chunks

    def fwd(s):
        return jnp.where(s < total, (s + n_lat_chunks) % total, total)

    def bwd(s):
        return jnp.where(s < total, total - 1 - s, total)

    return total, fwd, bwd


def _scan_specs(b, ng, tp, n_lat_chunks, n_ctx_chunks, n_one, n_two, nchunks):
    assert n_lat_chunks % nchunks == 0 and n_ctx_chunks % nchunks == 0
    rows = CHUNK * nchunks
    total, fwd, bwd = _scan_chunk_maps(n_lat_chunks // nchunks, n_ctx_chunks // nchunks)
    assert tp == (total + 1) * rows
    in_specs = []
    for dr, cm in enumerate((fwd, bwd)):
        one = pl.BlockSpec((1, ng, rows, LANES), lambda bi, s, cm=cm: (bi, 0, cm(s), 0))
        two = pl.BlockSpec((1, 1, ng, rows, LANES), lambda bi, s, cm=cm, dr=dr: (dr, bi, 0, cm(s), 0))
        in_specs += [one] * n_one + [two] * n_two
    out_specs = [pl.BlockSpec((1, ng, rows, LANES), lambda bi, s, cm=cm: (bi, 0, cm(s), 0)) for cm in (fwd, bwd)]
    out_shape = [jax.ShapeDtypeStruct((b, ng, tp, LANES), BF16)] * 2
    return (b, total + 1), in_specs, out_specs, out_shape


def _scan_rows(ci, nchunks):
    fwd = pl.ds(pl.multiple_of(ci * CHUNK, CHUNK), CHUNK)
    bwd = pl.ds(pl.multiple_of((nchunks - 1 - ci) * CHUNK, CHUNK), CHUNK)
    return fwd, bwd


def _scan_or_pad(body, nchunks, y_os):
    s = pl.program_id(1)
    last = pl.num_programs(1) - 1

    @pl.when(s < last)
    def _():
        lax.fori_loop(0, nchunks, body, 0)

    @pl.when(s == last)
    def _():
        for y_o in y_os:
            y_o[...] = jnp.zeros_like(y_o)


def _chunk_by_groups(chunk_fn, loaded, splits):
    n = loaded[0][0].shape[0]
    step = n // splits
    parts = [[], []]
    for part in range(splits):
        gs = slice(part * step, (part + 1) * step)
        for dr in range(2):
            parts[dr].append(chunk_fn(*[a[gs] for a in loaded[dr]], rev=(dr == 1)))
    return [tuple(jnp.concatenate([p[i] for p in parts[dr]], axis=0) for i in range(2)) for dr in range(2)]


def _load_chunk(ref, lead, rs, dtype=F32):
    return ref[lead + (0, slice(None), rs)].astype(dtype)


def _rw_chunk(r, v, kk, bb, kd, lw, st, rev):
    c = CHUNK
    half = RW_HEAD
    npair = r.shape[0]
    ri = lax.broadcasted_iota(jnp.int32, (1, c, LANES), 1)
    ci = lax.broadcasted_iota(jnp.int32, (1, c, LANES), 2) % half
    head_a = lax.broadcasted_iota(jnp.int32, (1, 1, LANES), 2) < half

    def split_rows(x):
        zero = jnp.zeros_like(x)
        return jnp.concatenate([jnp.where(head_a, x, zero), jnp.where(head_a, zero, x)], axis=1)

    if rev:
        incl, strict = ci >= ri, ci > ri
    else:
        incl, strict = ci <= ri, ci < ri
    cum = _chunk_cumsum(lw, rev)
    tot = cum[:, 0:1] if rev else cum[:, c - 1:c]
    e_in = jnp.exp2(cum)
    e_prev = jnp.exp2(cum - lw)
    e_inv = jnp.exp2(-cum)
    e_out = jnp.exp2(tot - cum)
    rt = (r * e_in).astype(BF16)
    at = (-kk * e_prev).astype(BF16)
    bi = (bb * e_inv).astype(BF16)
    ki = (kd * e_inv).astype(BF16)
    bh = (bb * e_out).astype(BF16)
    kh = (kd * e_out).astype(BF16)
    vb = v.astype(BF16)
    sb = st.astype(BF16)
    lhs = jnp.concatenate([at, rt], axis=1)
    h0 = _bmm_nt(lhs, sb)
    m_b = _bmm_nt(lhs, split_rows(bi))
    m_k = _bmm_nt(lhs, split_rows(ki))
    tri = jnp.concatenate([strict, incl], axis=1)
    m_b = m_b.astype(BF16)
    m_b = jnp.where(tri, m_b, jnp.zeros_like(m_b))
    m_k = m_k.astype(BF16)
    m_k = jnp.where(tri, m_k, jnp.zeros_like(m_k))
    ab16, a_rb = m_b[:, :c], m_b[:, c:]
    zero16 = jnp.zeros_like(ab16)
    a1 = jnp.where(ri // 2 == ci // 2, ab16, zero16)
    pmb = jnp.where(ri == ci, jnp.ones_like(ab16), a1)
    off = jnp.logical_and(ri // 4 == ci // 4, ri // 2 != ci // 2)
    x2 = jnp.where(off, ab16, zero16).astype(F32)
    a1f = a1.astype(F32)
    a1_row = jnp.where(head_a, jnp.sum(jnp.where(head_a, a1f, 0.0), axis=2, keepdims=True),
                       jnp.sum(jnp.where(head_a, 0.0, a1f), axis=2, keepdims=True))
    a1_col = jnp.sum(a1f, axis=1, keepdims=True)
    y2 = x2 + a1_row * pltpu.roll(x2, c - 1 if rev else 1, axis=1)
    d4 = y2 + pltpu.roll(y2, 1 if rev else LANES - 1, axis=2) * a1_col
    pmb = jnp.where(off, d4.astype(BF16), pmb)
    m = 4
    while m < c:
        off = jnp.logical_and(ri // (2 * m) == ci // (2 * m), ri // m != ci // m)
        a_off = jnp.where(off, ab16, zero16)
        delta = _bmm(_bmm(pmb, split_rows(a_off)).astype(BF16), split_rows(pmb))
        pmb = jnp.where(off, delta.astype(BF16), pmb)
        m *= 2
    av = _bmm(m_k, split_rows(vb))
    x = h0[:, :c] + av[:, :c]
    u = _bmm(pmb, split_rows(x.astype(BF16)))
    ub = u.astype(BF16)
    o_full = h0[:, c:] + av[:, c:] + _bmm(a_rb, split_rows(ub))
    upd = _bmm_tn(jnp.concatenate([ub, vb], axis=1), jnp.concatenate([bh, kh], axis=1))
    blk_r = lax.broadcasted_iota(jnp.int32, (1, LANES, LANES), 1) // half
    blk_c = lax.broadcasted_iota(jnp.int32, (1, LANES, LANES), 2) // half
    st_new = st * jnp.exp2(tot) + jnp.where(blk_r == blk_c, upd, 0.0)
    return o_full, st_new


def _rw_scan_kernel(*refs):
    ins = refs[:12]
    y_os = refs[12:14]
    st = refs[14]

    @pl.when(pl.program_id(1) == 0)
    def _():
        st[...] = jnp.zeros_like(st)

    def body(ci, carry):
        rows = _scan_rows(ci, RW_SCAN_CHUNKS)
        loaded = []
        for dr in range(2):
            r_r, v_r, kk_r, bb_r, kd_r, lw_r = ins[6 * dr:6 * dr + 6]
            rs = rows[dr]
            loaded.append((_load_chunk(r_r, (), rs), _load_chunk(v_r, (), rs), _load_chunk(kk_r, (), rs),
                           _load_chunk(bb_r, (0,), rs), _load_chunk(kd_r, (0,), rs), _load_chunk(lw_r, (0,), rs),
                           st[dr]))
        results = _chunk_by_groups(_rw_chunk, loaded, 1)
        for dr in range(2):
            y_os[dr][0, :, rows[dr]] = results[dr][0].astype(y_os[dr].dtype)
            st[dr] = results[dr][1]
        return carry

    _scan_or_pad(body, RW_SCAN_CHUNKS, y_os)


def _rw_scan(feats, *, n_lat_chunks, n_ctx_chunks):
    r, v, kk, lw, bb, kd = feats
    b, ng, tp, _ = r.shape
    grid, in_specs, out_specs, out_shape = _scan_specs(b, ng, tp, n_lat_chunks, n_ctx_chunks, 3, 3, RW_SCAN_CHUNKS)
    return pl.pallas_call(
        _rw_scan_kernel,
        grid=grid,
        in_specs=in_specs,
        out_specs=out_specs,
        out_shape=out_shape,
        scratch_shapes=[pltpu.VMEM((2, ng, LANES, LANES), F32)],
        name="rwkv7_scan",
        compiler_params=pltpu.CompilerParams(
            dimension_semantics=("parallel", "arbitrary"), vmem_limit_bytes=VMEM_LIMIT),
    )(r, v, kk, bb, kd, lw, r, v, kk, bb, kd, lw)


def _residual_tail(yg_bf16, wo_ref, x_ref, mod_ref, pg_ref, o_ref):
    d = x_ref.shape[-1]
    o = _dot(yg_bf16, wo_ref[...])
    ms = jnp.mean(o * o, axis=-1, keepdims=True)
    o = o * lax.rsqrt(ms + NORM_EPS) * pg_ref[...]
    gate = mod_ref[0, 0][:, 2 * d:]
    o_ref[0] = x_ref[0] + gate * o


def _rw_out_kernel(yf_ref, yb_ref, z_ref, bv_ref, lnw_ref, lnb_ref, hs_ref, wo_ref, x_ref, mod_ref, pg_ref,
                   o_ref, yg):
    inv = 1.0 / RW_HEAD
    hs = hs_ref[...]
    per = hs.shape[0] // LANES

    def groups(ref, g0):
        return jnp.concatenate([ref[0, g0 + i] for i in range(per)], axis=1)

    for g0 in range(0, yf_ref.shape[1], per):
        cs = slice(g0 * LANES, (g0 + per) * LANES)
        y = groups(yf_ref, g0).astype(F32) + groups(yb_ref, g0).astype(F32)
        mu = _dot(y.astype(BF16), hs) * inv
        yc = y - mu
        var = _dot((yc * yc).astype(BF16), hs) * inv
        yn = yc * lax.rsqrt(var + LN_X_EPS) * lnw_ref[:, cs] + lnb_ref[:, cs] + groups(bv_ref, g0).astype(F32)
        yg[:, cs] = yn.astype(BF16) * groups(z_ref, g0)
    _residual_tail(yg[...], wo_ref, x_ref, mod_ref, pg_ref, o_ref)


def _hg_out_kernel(yf_ref, yb_ref, sg_ref, gn_ref, wo_ref, x_ref, mod_ref, pg_ref, o_ref, yg):
    for hx in range(yf_ref.shape[1]):
        cs = slice(hx * LANES, (hx + 1) * LANES)
        y = yf_ref[0, hx].astype(F32) + yb_ref[0, hx].astype(F32)
        ms = jnp.mean(y * y, axis=-1, keepdims=True)
        yn = y * lax.rsqrt(ms + NORM_EPS) * gn_ref[...]
        yg[:, cs] = yn.astype(BF16) * sg_ref[0, hx]
    _residual_tail(yg[...], wo_ref, x_ref, mod_ref, pg_ref, o_ref)


def _out_group_spec(di):
    return pl.BlockSpec((1, di // LANES, OTILE, LANES), lambda bi, j: (bi, 0, j, 0))


def _mixer_out(name, kernel_fn, ys, extras, extra_specs, wo, xs, mod, post_g, n_tiles, out_rows):
    b, tp, d = xs.shape
    in_place = out_rows == tp
    assert in_place or out_rows == n_tiles * OTILE
    di = wo.shape[0]
    per = TILE // OTILE
    tokd = pl.BlockSpec((1, OTILE, d), lambda bi, j: (bi, j, 0))
    in_specs = [_out_group_spec(di)] * 2 + extra_specs + [
        pl.BlockSpec((di, d), lambda bi, j: (0, 0)),
        tokd,
        pl.BlockSpec((1, 1, 1, mod.shape[-1]), lambda bi, j: (bi, j // per, 0, 0)),
        pl.BlockSpec((1, d), lambda bi, j: (0, 0)),
    ]
    n_in = len(in_specs)
    return pl.pallas_call(
        kernel_fn,
        grid=(b, n_tiles),
        in_specs=in_specs,
        out_specs=tokd,
        out_shape=jax.ShapeDtypeStruct((b, out_rows, d), F32),
        scratch_shapes=[pltpu.VMEM((OTILE, di), BF16)],
        input_output_aliases={n_in - 3: 0} if in_place else {},
        name=name,
        compiler_params=pltpu.CompilerParams(
            dimension_semantics=("parallel", "parallel"), vmem_limit_bytes=VMEM_LIMIT),
    )(*ys, *extras, wo, xs, mod, post_g)


def _hg_feat_kernel(x_ref, mod_ref, g_ref, wq_ref, wf0_ref, wf1_ref, wi_ref, wg_ref, lb_ref,
                    q_o, i_o, sg_o, lf_o, kd_o, hbf):
    n = pl.program_id(1)
    j = pl.program_id(2)
    d = x_ref.shape[-1]
    rows = x_ref.shape[1]
    base = pl.multiple_of(j * rows, rows)

    @pl.when(n == 0)
    def _prologue():
        g = g_ref[...]
        for t in range(rows // TILE):
            m = mod_ref[0, t]
            shift, scale = m[:, :d], m[:, d:2 * d]

            def norm_rows(rb, c, t=t, shift=shift, scale=scale):
                r0 = pl.multiple_of(t * TILE + rb * GRID_W, GRID_W)
                hbf[pl.ds(base + r0, GRID_W), :] = _adaln(x_ref[0, pl.ds(r0, GRID_W), :], g, scale,
                                                          shift).astype(BF16)
                return c

            lax.fori_loop(0, TILE // GRID_W, norm_rows, 0)

    h = hbf[pl.ds(base, rows), :]
    lb = lb_ref[...]
    _store_groups(q_o, (0,), _silu(_dot(h, wq_ref[...])))
    for dr, wf_ref in enumerate((wf0_ref, wf1_ref)):
        f = lb + (1.0 - lb) * jax.nn.sigmoid(_dot(h, wf_ref[...]))
        _store_groups(kd_o, (dr, 0), 1.0 - f)
        _store_groups(lf_o, (dr, 0), jnp.log2(f))
    _store_groups(i_o, (0,), _dot(h, wi_ref[...]))
    _store_groups(sg_o, (0,), _silu(_dot(h, wg_ref[...])))


def _hg_features(xs, mod, pre_g, w_in, lb, *, tn=256):
    b, tp, d = xs.shape
    di = w_in.shape[-1] // 5
    nt = di // tn
    per = HG_TILES_PER_STEP if (tp // TILE) % HG_TILES_PER_STEP == 0 else 1
    rows = per * TILE
    nsteps = tp // rows
    _, _, s1, s2 = _group_major_specs(b, tp, di, tn)
    gq = tn // LANES
    one = pl.BlockSpec((1, gq, rows, LANES), lambda bi, n, j: (bi, n, j, 0))
    two = pl.BlockSpec((2, 1, gq, rows, LANES), lambda bi, n, j: (0, bi, n, j, 0))

    def tok_tile(n, j):
        return jnp.where(n == 0, j, nsteps - 1)

    return pl.pallas_call(
        _hg_feat_kernel,
        grid=(b, nt, nsteps),
        in_specs=[
            pl.BlockSpec((1, rows, d), lambda bi, n, j: (bi, tok_tile(n, j), 0)),
            pl.BlockSpec((1, per, 1, mod.shape[-1]), lambda bi, n, j: (bi, tok_tile(n, j), 0, 0)),
            pl.BlockSpec((1, d), lambda bi, n, j: (0, 0)),
            *[pl.BlockSpec((d, tn), lambda bi, n, j, part=part: (0, part * nt + n)) for part in range(5)],
            pl.BlockSpec((1, tn), lambda bi, n, j: (0, n)),
        ],
        out_specs=[one, one, one, two, two],
        out_shape=[s1(BF16), s1(BF16), s1(BF16), s2(F32), s2(BF16)],
        scratch_shapes=[pltpu.VMEM((tp, d), BF16)],
        name="hgrn2_features",
        compiler_params=pltpu.CompilerParams(
            dimension_semantics=("parallel", "arbitrary", "arbitrary"), vmem_limit_bytes=VMEM_LIMIT),
    )(xs, mod, pre_g, w_in, w_in, w_in, w_in, w_in, lb)


def _hg_chunk(q, k, v, lf2, st, rev):
    c = CHUNK
    sub = SUBLANES
    nh = q.shape[0]
    row = lax.broadcasted_iota(jnp.int32, (1, c, 1), 1)
    ri = lax.broadcasted_iota(jnp.int32, (1, c, c), 1)
    ci = lax.broadcasted_iota(jnp.int32, (1, c, c), 2)
    p = lf2
    diag = jnp.sum(q * k, axis=2, keepdims=True)
    att = jnp.where(ri == ci, diag, jnp.zeros((nh, c, c), F32))
    m = 1
    while m < c:
        pair = jnp.logical_and(ri // (2 * m) == ci // (2 * m), ri // m != ci // m)
        if m >= sub:
            zeros = jnp.zeros((nh, m, LANES), BF16)
            q_parts, k_parts, p_parts = [], [], []
            for blk in range(c // (2 * m)):
                lo = slice(blk * 2 * m, blk * 2 * m + m)
                hi = slice(blk * 2 * m + m, (blk + 1) * 2 * m)
                early, later = (hi, lo) if rev else (lo, hi)
                bidx = blk * 2 * m + (m if rev else m - 1)
                tt = p[:, bidx:bidx + 1, :]
                q_l = (q[:, later] * jnp.exp2(p[:, later])).astype(BF16)
                k_e = (k[:, early] * jnp.exp2(tt - p[:, early])).astype(BF16)
                if rev:
                    q_parts += [q_l, zeros]
                    k_parts += [zeros, k_e]
                    p_parts += [p[:, later] + tt, p[:, early]]
                else:
                    q_parts += [zeros, q_l]
                    k_parts += [k_e, zeros]
                    p_parts += [p[:, early], p[:, later] + tt]
            mm = _bmm_nt(jnp.concatenate(q_parts, axis=1), jnp.concatenate(k_parts, axis=1))
            p = jnp.concatenate(p_parts, axis=1)
        else:
            in_hi = ((row // m) % 2) == 1
            late = jnp.logical_not(in_hi) if rev else in_hi
            if m == 1:
                tt = jnp.where(late, pltpu.roll(p, c - 1 if rev else 1, axis=1), p)
            else:
                p4 = p.reshape(nh, c // sub, sub, LANES)
                srow = lax.broadcasted_iota(jnp.int32, (1, 1, sub, 1), 2)
                tt4 = None
                for blk in range(sub // (2 * m)):
                    bidx = blk * 2 * m + (m if rev else m - 1)
                    cand = jnp.broadcast_to(p4[:, :, bidx:bidx + 1, :], p4.shape)
                    tt4 = cand if tt4 is None else jnp.where(srow // (2 * m) == blk, cand, tt4)
                tt = tt4.reshape(nh, c, LANES)
            ee = jnp.exp2(jnp.where(late, p, tt - p))
            qt = jnp.where(late, q * ee, 0.0).astype(BF16)
            kt = jnp.where(late, 0.0, k * ee).astype(BF16)
            mm = _bmm_nt(qt, kt)
            p = jnp.where(late, p + tt, p)
        att = jnp.where(pair, mm, att)
        m *= 2
    tot = p[:, 0:1] if rev else p[:, c - 1:c]
    vb = v.astype(BF16)
    y = _bmm(att.astype(BF16), vb) + _bmm_nt((q * jnp.exp2(p)).astype(BF16), st.astype(BF16))
    st_new = st * jnp.exp2(tot) + _bmm_tn(vb, (k * jnp.exp2(tot - p)).astype(BF16))
    return y, st_new


def _hg_scan_kernel(*refs):
    ins = refs[:8]
    y_os = refs[8:10]
    st = refs[10]

    @pl.when(pl.program_id(1) == 0)
    def _():
        st[...] = jnp.zeros_like(st)

    def body(ci, carry):
        rows = _scan_rows(ci, HG_SCAN_CHUNKS)
        loaded = []
        for dr in range(2):
            q_r, v_r, kd_r, lf_r = ins[4 * dr:4 * dr + 4]
            rs = rows[dr]
            loaded.append((_load_chunk(q_r, (), rs), _load_chunk(kd_r, (0,), rs), _load_chunk(v_r, (), rs, BF16),
                           _load_chunk(lf_r, (0,), rs), st[dr]))
        results = _chunk_by_groups(_hg_chunk, loaded, HG_HEAD_SPLITS)
        for dr in range(2):
            y_os[dr][0, :, rows[dr]] = results[dr][0].astype(y_os[dr].dtype)
            st[dr] = results[dr][1]
        return carry

    _scan_or_pad(body, HG_SCAN_CHUNKS, y_os)


def _hg_scan(q, v, lf, kd, *, n_lat_chunks, n_ctx_chunks):
    b, ng, tp, _ = q.shape
    grid, in_specs, out_specs, out_shape = _scan_specs(b, ng, tp, n_lat_chunks, n_ctx_chunks, 2, 2, HG_SCAN_CHUNKS)
    return pl.pallas_call(
        _hg_scan_kernel,
        grid=grid,
        in_specs=in_specs,
        out_specs=out_specs,
        out_shape=out_shape,
        scratch_shapes=[pltpu.VMEM((2, ng, LANES, LANES), F32)],
        name="hgrn2_scan",
        compiler_params=pltpu.CompilerParams(
            dimension_semantics=("parallel", "arbitrary"), vmem_limit_bytes=VMEM_LIMIT),
    )(q, v, kd, lf, q, v, kd, lf)


def _pad_rows(w, rows):
    return jnp.pad(w, ((0, rows - w.shape[0]), (0, 0)))


def _pad_cols(w, cols):
    return jnp.pad(w, ((0, 0), (0, cols - w.shape[1])))


def kernel(x, c, ctx, c_ctx, mod_w, mod_b, pre_g, post_g, rw_mix, rw_proj, rw_wo, rw_w0, rw_w1, rw_w2, rw_a0,
           rw_a1, rw_a2, rw_v0, rw_v1, rw_v2, rw_kk, rw_ka, rw_rk, rw_lnw, rw_lnb, hg_win, hg_wo, hg_gn, hg_lb):
    b, seq, d = x.shape
    ctx_len = ctx.shape[1]
    depth = mod_w.shape[0]
    di = rw_proj.shape[-1]
    assert seq % TILE == 0 and ctx_len % CHUNK == 0 and ctx_len <= TILE and seq % GRID_W == 0
    n_lat_tiles = seq // TILE
    tp = seq + TILE
    n_lat_chunks, n_ctx_chunks = seq // CHUNK, ctx_len // CHUNK

    xs = jnp.concatenate([x, ctx, jnp.zeros((b, tp - seq - ctx_len, d), x.dtype)], axis=1)

    rows = ((b + 1 + 7) // 8) * 8
    cond = jnp.concatenate([c, c_ctx[None], jnp.zeros((rows - b - 1, d), c.dtype)], axis=0)
    mod_all = _modulation(cond, mod_w, mod_b)
    mod_lat = jnp.broadcast_to(mod_all[:, :b, None, :], (depth, b, n_lat_tiles, 3 * d))
    mod_ctx = jnp.broadcast_to(mod_all[:, b:b + 1, None, :], (depth, b, 1, 3 * d))
    mod_t = jnp.concatenate([mod_lat, mod_ctx], axis=2)[:, :, :, None, :]

    p_lb = jax.nn.softmax(hg_lb.astype(F32), axis=0)
    lb_all = jnp.cumsum(p_lb, axis=0) - p_lb[0]

    vf = None
    for i in range(depth):
        ctx_out = i < depth - 1
        out_rows = tp if ctx_out else seq
        n_out_tiles = out_rows // OTILE
        j = i // 2
        mod_i = mod_t[i]
        pg = pre_g[i][None]
        if i % 2 == 0:
            r64 = rw_w1.shape[-1]
            w2 = jnp.stack([jnp.pad(rw_w2[j, 0], ((0, LANES - r64), (0, 0))),
                            jnp.pad(rw_w2[j, 1], ((r64, LANES - 2 * r64), (0, 0)))])
            a2 = jnp.stack([jnp.pad(rw_a2[j, 0], ((0, LANES - r64), (0, 0))),
                            jnp.pad(rw_a2[j, 1], ((r64, LANES - 2 * r64), (0, 0)))])
            if j == 0:
                v0 = jnp.zeros((di,), F32)
                v1 = jnp.zeros((d, LANES), BF16)
                v2 = jnp.zeros((LANES, di), BF16)
            else:
                v0 = rw_v0[j - 1]
                v1 = _pad_cols(rw_v1[j - 1], LANES).astype(BF16)
                v2 = _pad_rows(rw_v2[j - 1], LANES).astype(BF16)
            p = {
                "mix": rw_mix[j],
                "proj": rw_proj[j].astype(BF16),
                "w1": jnp.concatenate([rw_w1[j, 0], rw_w1[j, 1]], axis=1).astype(BF16),
                "a1": jnp.concatenate([rw_a1[j, 0], rw_a1[j, 1]], axis=1).astype(BF16),
                "v1": v1,
                "w2": w2.astype(BF16),
                "a2": a2.astype(BF16),
                "v2": v2,
                "vec": jnp.stack([rw_w0[j, 0], rw_w0[j, 1], rw_a0[j, 0], rw_a0[j, 1], v0, rw_kk[j], rw_ka[j],
                                  rw_rk[j]]),
            }
            r, v, kk, z, bv, lw, bb, kd = _rw_features(xs, mod_i, pg, p, vf if j > 0 else None,
                                                       n_lat_tiles=n_lat_tiles, ctx_len=ctx_len)
            if j == 0:
                vf = v
            yf, yb = _rw_scan((r, v, kk, lw, bb, kd), n_lat_chunks=n_lat_chunks, n_ctx_chunks=n_ctx_chunks)
            gspec = _out_group_spec(di)
            extras = [z, bv, rw_lnw[j][None], rw_lnb[j][None], _head_sum_matrix(2 * LANES, RW_HEAD)]
            especs = [gspec, gspec, pl.BlockSpec((1, di), lambda bi, jj: (0, 0)),
                      pl.BlockSpec((1, di), lambda bi, jj: (0, 0)), pl.BlockSpec((2 * LANES, 2 * LANES), lambda bi, jj: (0, 0))]
            xs = _mixer_out("rwkv7_out", _rw_out_kernel, (yf, yb), extras, especs, rw_wo[j].astype(BF16), xs, mod_i,
                            post_g[i][None], n_out_tiles, out_rows)
        else:
            q, iv, sg, lf, kd = _hg_features(xs, mod_i, pg, hg_win[j].astype(BF16), lb_all[i][None])
            yf, yb = _hg_scan(q, iv, lf, kd, n_lat_chunks=n_lat_chunks, n_ctx_chunks=n_ctx_chunks)
            extras = [sg, hg_gn[j][None]]
            especs = [_out_group_spec(di), pl.BlockSpec((1, HG_HEAD), lambda bi, jj: (0, 0))]
            xs = _mixer_out("hgrn2_out", _hg_out_kernel, (yf, yb), extras, especs, hg_wo[j].astype(BF16), xs, mod_i,
                            post_g[i][None], n_out_tiles, out_rows)
    return xs
```

```python
import functools
import math

import jax
import jax.numpy as jnp
from jax import lax
from jax.experimental import pallas as pl
from jax.experimental.pallas import tpu as pltpu

F32 = jnp.float32
BF16 = jnp.bfloat16
HI = lax.Precision.HIGHEST

GRID_W = 64
RW_HEAD = 64
HG_HEAD = 128
CHUNK = 64
TILE = 512
OTILE = 512
HG_TILES_PER_STEP = 3
RW_SCAN_CHUNKS = 4
HG_SCAN_CHUNKS = 4
HG_HEAD_SPLITS = 4
NORM_EPS = 1e-6
LN_X_EPS = 64e-5
LANES = 128
SUBLANES = 8
V7X_VMEM_BYTES = 64 * 1024 * 1024
VMEM_LIMIT = V7X_VMEM_BYTES * 3 // 4


def _silu(x):
    return x * jax.nn.sigmoid(x)


def _dot(a, b):
    return jnp.dot(a, b, preferred_element_type=F32)


def _mod_kernel(s_ref, w_ref, b_ref, o_ref):
    s = _silu(s_ref[...])
    o_ref[0] = jnp.dot(s, w_ref[0], precision=HI, preferred_element_type=F32) + b_ref[0]


def _modulation(cond, mod_w, mod_b):
    depth, d, d3 = mod_w.shape
    rows = cond.shape[0]
    nb = d3 // d
    return pl.pallas_call(
        _mod_kernel,
        grid=(depth, nb),
        in_specs=[
            pl.BlockSpec((rows, d), lambda i, n: (0, 0)),
            pl.BlockSpec((1, d, d), lambda i, n: (i, 0, n)),
            pl.BlockSpec((1, 1, d), lambda i, n: (i, 0, n)),
        ],
        out_specs=pl.BlockSpec((1, rows, d), lambda i, n: (i, 0, n)),
        out_shape=jax.ShapeDtypeStruct((depth, rows, d3), F32),
        name="adaln_modulation",
        compiler_params=pltpu.CompilerParams(vmem_limit_bytes=VMEM_LIMIT),
    )(cond, mod_w, mod_b.reshape(depth, 1, d3))


def _adaln(xb, g, scale, shift):
    ms = jnp.mean(xb * xb, axis=-1, keepdims=True)
    return xb * lax.rsqrt(ms + NORM_EPS) * g * (1.0 + scale) + shift


def _store_groups(o_ref, lead, val):
    for gq in range(val.shape[1] // LANES):
        o_ref[lead + (gq,)] = val[:, gq * LANES:(gq + 1) * LANES].astype(o_ref.dtype)


def _load_groups(ref, lead):
    n = ref.shape[len(lead)]
    return jnp.concatenate([ref[lead + (gq,)] for gq in range(n)], axis=1)


def _head_sum_matrix(width, head):
    r = lax.broadcasted_iota(jnp.int32, (width, width), 0) // head
    c = lax.broadcasted_iota(jnp.int32, (width, width), 1) // head
    return (r == c).astype(BF16)


def _rw_feat_kernel(n_lat_tiles, ctx_len, has_vres, *refs):
    (x_ref, xp_ref, xn_ref, mod_ref, g_ref, mix_ref, w1_ref, a1_ref, v1_ref,
     proj_ref, w2_ref, a2_ref, v2_ref, vec_ref, hs_ref) = refs[:15]
    pos = 15
    vf_ref = None
    if has_vres:
        vf_ref = refs[pos]
        pos += 1
    (r_o, v_o, kk_o, z_o, bv_o, lw_o, bb_o, kd_o) = refs[pos:pos + 8]
    hbuf, xm, tw, ta, tv = refs[pos + 8:]

    j = pl.program_id(1)
    n = pl.program_id(2)
    d = x_ref.shape[-1]
    q4 = d // 4
    nblk = TILE // GRID_W

    @pl.when(n == 0)
    def _prologue():
        m = mod_ref[0, 0]
        shift, scale = m[:, :d], m[:, d:2 * d]
        g = g_ref[...]
        is_lat = j < n_lat_tiles
        prev_ok = jnp.logical_and(j >= 1, is_lat)
        next_ok = j < n_lat_tiles - 1

        def norm_rows(rb, c):
            r0 = pl.multiple_of(rb * GRID_W, GRID_W)
            hbuf[pl.ds(r0 + GRID_W, GRID_W), :] = _adaln(x_ref[0, pl.ds(r0, GRID_W), :], g, scale, shift)
            return c

        lax.fori_loop(0, nblk, norm_rows, 0)
        hp = _adaln(xp_ref[0], g, scale, shift)
        hbuf[0:GRID_W, :] = jnp.where(prev_ok, hp, 0.0)
        hn = _adaln(xn_ref[0], g, scale, shift)
        hbuf[TILE + GRID_W:TILE + 2 * GRID_W, :] = jnp.where(next_ok, hn, 0.0)

        row = lax.broadcasted_iota(jnp.int32, (GRID_W, 1), 0)

        def store_mixed(r0, cs, h_c, hs_c):
            dd = hs_c - h_c
            for k in range(6):
                xm[k, pl.ds(r0, GRID_W), cs] = (h_c + dd * mix_ref[k:k + 1, cs]).astype(BF16)

        def mix_lat(rb, c):
            r0 = pl.multiple_of(rb * GRID_W, GRID_W)
            for qi in range(4):
                cs = slice(qi * q4, (qi + 1) * q4)
                h_c = hbuf[pl.ds(r0 + GRID_W, GRID_W), cs]
                if qi == 0:
                    hs_c = jnp.where(row == 0, 0.0, pltpu.roll(h_c, 1, axis=0))
                elif qi == 1:
                    hs_c = jnp.where(row == GRID_W - 1, 0.0, pltpu.roll(h_c, GRID_W - 1, axis=0))
                elif qi == 2:
                    hs_c = hbuf[pl.ds(r0, GRID_W), cs]
                else:
                    hs_c = hbuf[pl.ds(r0 + 2 * GRID_W, GRID_W), cs]
                store_mixed(r0, cs, h_c, hs_c)
            return c

        def mix_ctx(rb, c):
            r0 = pl.multiple_of(rb * GRID_W, GRID_W)
            t = row + rb * GRID_W
            for qi in range(4):
                cs = slice(qi * q4, (qi + 1) * q4)
                h_c = hbuf[pl.ds(r0 + GRID_W, GRID_W), cs]
                if qi < 2:
                    edge = hbuf[pl.ds(r0 + GRID_W - 1, 1), cs]
                    hs_c = jnp.where(row == 0, edge, pltpu.roll(h_c, 1, axis=0))
                else:
                    edge = hbuf[pl.ds(r0 + 2 * GRID_W, 1), cs]
                    hs_c = jnp.where(row == GRID_W - 1, edge, pltpu.roll(h_c, GRID_W - 1, axis=0))
                    hs_c = jnp.where(t == ctx_len - 1, 0.0, hs_c)
                store_mixed(r0, cs, h_c, hs_c)
            return c

        @pl.when(is_lat)
        def _():
            lax.fori_loop(0, nblk, mix_lat, 0)

        @pl.when(jnp.logical_not(is_lat))
        def _():
            lax.fori_loop(0, nblk, mix_ctx, 0)

        tw[...] = jnp.tanh(_dot(xm[1], w1_ref[...])).astype(BF16)
        ta[...] = _dot(xm[4], a1_ref[...]).astype(BF16)
        if has_vres:
            tv[...] = _dot(xm[3], v1_ref[...]).astype(BF16)

    vec = vec_ref[...]
    r = _dot(xm[0], proj_ref[0, 0])
    k = _dot(xm[2], proj_ref[0, 1])
    v = _dot(xm[3], proj_ref[0, 2])
    z = _silu(_dot(xm[5], proj_ref[0, 3]))
    if has_vres:
        gate = jax.nn.sigmoid(vec[4:5] + _dot(tv[...], v2_ref[...]))
        v = v + (_load_groups(vf_ref, (0,)).astype(F32) - v) * gate
    hs = hs_ref[...]
    kk = k * vec[5:6]
    ss = _dot((kk * kk).astype(BF16), hs)
    kk = kk * lax.rsqrt(jnp.maximum(ss, 1e-24))
    decay_scale = -math.exp(-0.5) * math.log2(math.e)
    ksum = None
    for dr in range(2):
        u = vec[dr:dr + 1] + _dot(tw[...], w2_ref[dr])
        _store_groups(lw_o, (dr, 0), decay_scale * jax.nn.sigmoid(u))
        a = jax.nn.sigmoid(vec[2 + dr:3 + dr] + _dot(ta[...], a2_ref[dr]))
        kd = k * (1.0 + (a - 1.0) * vec[6:7])
        _store_groups(bb_o, (dr, 0), kk * a)
        _store_groups(kd_o, (dr, 0), kd)
        ksum = kd if ksum is None else ksum + kd
    bonus = _dot((r * ksum * vec[7:8]).astype(BF16), hs)
    _store_groups(r_o, (0,), r)
    _store_groups(v_o, (0,), v)
    _store_groups(kk_o, (0,), kk)
    _store_groups(z_o, (0,), z)
    _store_groups(bv_o, (0,), bonus * v)


def _group_major_specs(b, tp, di, tn):
    gq = tn // LANES
    one = pl.BlockSpec((1, gq, TILE, LANES), lambda bi, j, n: (bi, n, j, 0))
    two = pl.BlockSpec((2, 1, gq, TILE, LANES), lambda bi, j, n: (0, bi, n, j, 0))
    s1 = lambda dt: jax.ShapeDtypeStruct((b, di // LANES, tp, LANES), dt)
    s2 = lambda dt: jax.ShapeDtypeStruct((2, b, di // LANES, tp, LANES), dt)
    return one, two, s1, s2


def _rw_features(xs, mod, pre_g, p, vf, *, n_lat_tiles, ctx_len, tn=256):
    b, tp, d = xs.shape
    di = p["proj"].shape[-1]
    nt = di // tn
    ntiles = tp // TILE
    nb64 = tp // GRID_W
    per = TILE // GRID_W
    has_vres = vf is not None

    def tok(bi, j, n):
        return (bi, j, 0)

    in_specs = [
        pl.BlockSpec((1, TILE, d), tok),
        pl.BlockSpec((1, GRID_W, d), lambda bi, j, n: (bi, jnp.maximum(j * per - 1, 0), 0)),
        pl.BlockSpec((1, GRID_W, d), lambda bi, j, n: (bi, jnp.minimum((j + 1) * per, nb64 - 1), 0)),
        pl.BlockSpec((1, 1, 1, mod.shape[-1]), lambda bi, j, n: (bi, j, 0, 0)),
        pl.BlockSpec((1, d), lambda bi, j, n: (0, 0)),
        pl.BlockSpec((6, d), lambda bi, j, n: (0, 0)),
        pl.BlockSpec((d, LANES), lambda bi, j, n: (0, 0)),
        pl.BlockSpec((d, LANES), lambda bi, j, n: (0, 0)),
        pl.BlockSpec((d, LANES), lambda bi, j, n: (0, 0)),
        pl.BlockSpec((1, 4, d, tn), lambda bi, j, n: (0, 0, 0, n)),
        pl.BlockSpec((2, LANES, tn), lambda bi, j, n: (0, 0, n)),
        pl.BlockSpec((2, LANES, tn), lambda bi, j, n: (0, 0, n)),
        pl.BlockSpec((LANES, tn), lambda bi, j, n: (0, n)),
        pl.BlockSpec((8, tn), lambda bi, j, n: (0, n)),
        pl.BlockSpec((tn, tn), lambda bi, j, n: (0, 0)),
    ]
    args = [xs, xs, xs, mod, pre_g, p["mix"], p["w1"], p["a1"], p["v1"], p["proj"][None], p["w2"], p["a2"],
            p["v2"], p["vec"], _head_sum_matrix(tn, RW_HEAD)]
    one, two, s1, s2 = _group_major_specs(b, tp, di, tn)
    if has_vres:
        in_specs.append(one)
        args.append(vf)
    return pl.pallas_call(
        functools.partial(_rw_feat_kernel, n_lat_tiles, ctx_len, has_vres),
        grid=(b, ntiles, nt),
        in_specs=in_specs,
        out_specs=[one, one, one, one, one, two, two, two],
        out_shape=[s1(BF16), s1(BF16), s1(BF16), s1(BF16), s1(BF16), s2(F32), s2(BF16), s2(BF16)],
        scratch_shapes=[
            pltpu.VMEM((TILE + 2 * GRID_W, d), F32),
            pltpu.VMEM((6, TILE, d), BF16),
            pltpu.VMEM((TILE, LANES), BF16),
            pltpu.VMEM((TILE, LANES), BF16),
            pltpu.VMEM((TILE, LANES), BF16),
        ],
        name="rwkv7_features",
        compiler_params=pltpu.CompilerParams(
            dimension_semantics=("parallel", "parallel", "arbitrary"), vmem_limit_bytes=VMEM_LIMIT),
    )(*args)


def _bmm(a, b):
    return lax.dot_general(a, b, (((2,), (1,)), ((0,), (0,))), preferred_element_type=F32)


def _bmm_nt(a, b):
    return lax.dot_general(a, b, (((2,), (2,)), ((0,), (0,))), preferred_element_type=F32)


def _bmm_tn(a, b):
    return lax.dot_general(a, b, (((1,), (1,)), ((0,), (0,))), preferred_element_type=F32)


def _chunk_cumsum(x, rev):
    c = x.shape[1]
    row = lax.broadcasted_iota(jnp.int32, (1, c, 1), 1)
    s = 1
    while s < c:
        if rev:
            x = x + jnp.where(row < c - s, pltpu.roll(x, c - s, axis=1), 0.0)
        else:
            x = x + jnp.where(row >= s, pltpu.roll(x, s, axis=1), 0.0)
        s *= 2
    return x


def _scan_chunk_maps(n_lat_chunks, n_ctx_chunks):
    total = n_lat_chunks + n_ctx_chunks

    def fwd(s):
        return jnp.where(s < total, (s + n_lat_chunks) % total, total)

    def bwd(s):
        return jnp.where(s < total, total - 1 - s, total)

    return total, fwd, bwd


def _scan_specs(b, ng, tp, n_lat_chunks, n_ctx_chunks, n_one, n_two, nchunks):
    assert n_lat_chunks % nchunks == 0 and n_ctx_chunks % nchunks == 0
    rows = CHUNK * nchunks
    total, fwd, bwd = _scan_chunk_maps(n_lat_chunks // nchunks, n_ctx_chunks // nchunks)
    assert tp == (total + 1) * rows
    in_specs = []
    for dr, cm in enumerate((fwd, bwd)):
        one = pl.BlockSpec((1, ng, rows, LANES), lambda bi, s, cm=cm: (bi, 0, cm(s), 0))
        two = pl.BlockSpec((1, 1, ng, rows, LANES), lambda bi, s, cm=cm, dr=dr: (dr, bi, 0, cm(s), 0))
        in_specs += [one] * n_one + [two] * n_two
    out_specs = [pl.BlockSpec((1, ng, rows, LANES), lambda bi, s, cm=cm: (bi, 0, cm(s), 0)) for cm in (fwd, bwd)]
    out_shape = [jax.ShapeDtypeStruct((b, ng, tp, LANES), BF16)] * 2
    return (b, total + 1), in_specs, out_specs, out_shape


def _scan_rows(ci, nchunks):
    fwd = pl.ds(pl.multiple_of(ci * CHUNK, CHUNK), CHUNK)
    bwd = pl.ds(pl.multiple_of((nchunks - 1 - ci) * CHUNK, CHUNK), CHUNK)
    return fwd, bwd


def _scan_or_pad(body, nchunks, y_os):
    s = pl.program_id(1)
    last = pl.num_programs(1) - 1

    @pl.when(s < last)
    def _():
        lax.fori_loop(0, nchunks, body, 0)

    @pl.when(s == last)
    def _():
        for y_o in y_os:
            y_o[...] = jnp.zeros_like(y_o)


def _chunk_by_groups(chunk_fn, loaded, splits):
    n = loaded[0][0].shape[0]
    step = n // splits
    parts = [[], []]
    for part in range(splits):
        gs = slice(part * step, (part + 1) * step)
        for dr in range(2):
            parts[dr].append(chunk_fn(*[a[gs] for a in loaded[dr]], rev=(dr == 1)))
    return [tuple(jnp.concatenate([p[i] for p in parts[dr]], axis=0) for i in range(2)) for dr in range(2)]


def _load_chunk(ref, lead, rs, dtype=F32):
    return ref[lead + (0, slice(None), rs)].astype(dtype)


def _rw_chunk(r, v, kk, bb, kd, lw, st, rev):
    c = CHUNK
    half = RW_HEAD
    npair = r.shape[0]
    ri = lax.broadcasted_iota(jnp.int32, (1, c, LANES), 1)
    ci = lax.broadcasted_iota(jnp.int32, (1, c, LANES), 2) % half
    head_a = lax.broadcasted_iota(jnp.int32, (1, 1, LANES), 2) < half

    def split_rows(x):
        zero = jnp.zeros_like(x)
        return jnp.concatenate([jnp.where(head_a, x, zero), jnp.where(head_a, zero, x)], axis=1)

    if rev:
        incl, strict = ci >= ri, ci > ri
    else:
        incl, strict = ci <= ri, ci < ri
    cum = _chunk_cumsum(lw, rev)
    tot = cum[:, 0:1] if rev else cum[:, c - 1:c]
    e_in = jnp.exp2(cum)
    e_prev = jnp.exp2(cum - lw)
    e_inv = jnp.exp2(-cum)
    e_out = jnp.exp2(tot - cum)
    rt = (r * e_in).astype(BF16)
    at = (-kk * e_prev).astype(BF16)
    bi = (bb * e_inv).astype(BF16)
    ki = (kd * e_inv).astype(BF16)
    bh = (bb * e_out).astype(BF16)
    kh = (kd * e_out).astype(BF16)
    vb = v.astype(BF16)
    sb = st.astype(BF16)
    lhs = jnp.concatenate([at, rt], axis=1)
    h0 = _bmm_nt(lhs, sb)
    m_b = _bmm_nt(lhs, split_rows(bi))
    m_k = _bmm_nt(lhs, split_rows(ki))
    tri = jnp.concatenate([strict, incl], axis=1)
    m_b = m_b.astype(BF16)
    m_b = jnp.where(tri, m_b, jnp.zeros_like(m_b))
    m_k = m_k.astype(BF16)
    m_k = jnp.where(tri, m_k, jnp.zeros_like(m_k))
    ab16, a_rb = m_b[:, :c], m_b[:, c:]
    zero16 = jnp.zeros_like(ab16)
    a1 = jnp.where(ri // 2 == ci // 2, ab16, zero16)
    pmb = jnp.where(ri == ci, jnp.ones_like(ab16), a1)
    off = jnp.logical_and(ri // 4 == ci // 4, ri // 2 != ci // 2)
    x2 = jnp.where(off, ab16, zero16).astype(F32)
    a1f = a1.astype(F32)
    a1_row = jnp.where(head_a, jnp.sum(jnp.where(head_a, a1f, 0.0), axis=2, keepdims=True),
                       jnp.sum(jnp.where(head_a, 0.0, a1f), axis=2, keepdims=True))
    a1_col = jnp.sum(a1f, axis=1, keepdims=True)
    y2 = x2 + a1_row * pltpu.roll(x2, c - 1 if rev else 1, axis=1)
    d4 = y2 + pltpu.roll(y2, 1 if rev else LANES - 1, axis=2) * a1_col
    pmb = jnp.where(off, d4.astype(BF16), pmb)
    m = 4
    while m < c:
        off = jnp.logical_and(ri // (2 * m) == ci // (2 * m), ri // m != ci // m)
        a_off = jnp.where(off, ab16, zero16)
        delta = _bmm(_bmm(pmb, split_rows(a_off)).astype(BF16), split_rows(pmb))
        pmb = jnp.where(off, delta.astype(BF16), pmb)
        m *= 2
    av = _bmm(m_k, split_rows(vb))
    x = h0[:, :c] + av[:, :c]
    u = _bmm(pmb, split_rows(x.astype(BF16)))
    ub = u.astype(BF16)
    o_full = h0[:, c:] + av[:, c:] + _bmm(a_rb, split_rows(ub))
    upd = _bmm_tn(jnp.concatenate([ub, vb], axis=1), jnp.concatenate([bh, kh], axis=1))
    blk_r = lax.broadcasted_iota(jnp.int32, (1, LANES, LANES), 1) // half
    blk_c = lax.broadcasted_iota(jnp.int32, (1, LANES, LANES), 2) // half
    st_new = st * jnp.exp2(tot) + jnp.where(blk_r == blk_c, upd, 0.0)
    return o_full, st_new


def _rw_scan_kernel(*refs):
    ins = refs[:12]
    y_os = refs[12:14]
    st = refs[14]

    @pl.when(pl.program_id(1) == 0)
    def _():
        st[...] = jnp.zeros_like(st)

    def body(ci, carry):
        rows = _scan_rows(ci, RW_SCAN_CHUNKS)
        loaded = []
        for dr in range(2):
            r_r, v_r, kk_r, bb_r, kd_r, lw_r = ins[6 * dr:6 * dr + 6]
            rs = rows[dr]
            loaded.append((_load_chunk(r_r, (), rs), _load_chunk(v_r, (), rs), _load_chunk(kk_r, (), rs),
                           _load_chunk(bb_r, (0,), rs), _load_chunk(kd_r, (0,), rs), _load_chunk(lw_r, (0,), rs),
                           st[dr]))
        results = _chunk_by_groups(_rw_chunk, loaded, 1)
        for dr in range(2):
            y_os[dr][0, :, rows[dr]] = results[dr][0].astype(y_os[dr].dtype)
            st[dr] = results[dr][1]
        return carry

    _scan_or_pad(body, RW_SCAN_CHUNKS, y_os)


def _rw_scan(feats, *, n_lat_chunks, n_ctx_chunks):
    r, v, kk, lw, bb, kd = feats
    b, ng, tp, _ = r.shape
    grid, in_specs, out_specs, out_shape = _scan_specs(b, ng, tp, n_lat_chunks, n_ctx_chunks, 3, 3, RW_SCAN_CHUNKS)
    return pl.pallas_call(
        _rw_scan_kernel,
        grid=grid,
        in_specs=in_specs,
        out_specs=out_specs,
        out_shape=out_shape,
        scratch_shapes=[pltpu.VMEM((2, ng, LANES, LANES), F32)],
        name="rwkv7_scan",
        compiler_params=pltpu.CompilerParams(
            dimension_semantics=("parallel", "arbitrary"), vmem_limit_bytes=VMEM_LIMIT),
    )(r, v, kk, bb, kd, lw, r, v, kk, bb, kd, lw)


def _residual_tail(yg_bf16, wo_ref, x_ref, mod_ref, pg_ref, o_ref):
    d = x_ref.shape[-1]
    o = _dot(yg_bf16, wo_ref[...])
    ms = jnp.mean(o * o, axis=-1, keepdims=True)
    o = o * lax.rsqrt(ms + NORM_EPS) * pg_ref[...]
    gate = mod_ref[0, 0][:, 2 * d:]
    o_ref[0] = x_ref[0] + gate * o


def _rw_out_kernel(yf_ref, yb_ref, z_ref, bv_ref, lnw_ref, lnb_ref, hs_ref, wo_ref, x_ref, mod_ref, pg_ref,
                   o_ref, yg):
    inv = 1.0 / RW_HEAD
    hs = hs_ref[...]
    per = hs.shape[0] // LANES

    def groups(ref, g0):
        return jnp.concatenate([ref[0, g0 + i] for i in range(per)], axis=1)

    for g0 in range(0, yf_ref.shape[1], per):
        cs = slice(g0 * LANES, (g0 + per) * LANES)
        y = groups(yf_ref, g0).astype(F32) + groups(yb_ref, g0).astype(F32)
        mu = _dot(y.astype(BF16), hs) * inv
        yc = y - mu
        var = _dot((yc * yc).astype(BF16), hs) * inv
        yn = yc * lax.rsqrt(var + LN_X_EPS) * lnw_ref[:, cs] + lnb_ref[:, cs] + groups(bv_ref, g0).astype(F32)
        yg[:, cs] = yn.astype(BF16) * groups(z_ref, g0)
    _residual_tail(yg[...], wo_ref, x_ref, mod_ref, pg_ref, o_ref)


def _hg_out_kernel(yf_ref, yb_ref, sg_ref, gn_ref, wo_ref, x_ref, mod_ref, pg_ref, o_ref, yg):
    for hx in range(yf_ref.shape[1]):
        cs = slice(hx * LANES, (hx + 1) * LANES)
        y = yf_ref[0, hx].astype(F32) + yb_ref[0, hx].astype(F32)
        ms = jnp.mean(y * y, axis=-1, keepdims=True)
        yn = y * lax.rsqrt(ms + NORM_EPS) * gn_ref[...]
        yg[:, cs] = yn.astype(BF16) * sg_ref[0, hx]
    _residual_tail(yg[...], wo_ref, x_ref, mod_ref, pg_ref, o_ref)


def _out_group_spec(di):
    return pl.BlockSpec((1, di // LANES, OTILE, LANES), lambda bi, j: (bi, 0, j, 0))


def _mixer_out(name, kernel_fn, ys, extras, extra_specs, wo, xs, mod, post_g, n_tiles, out_rows):
    b, tp, d = xs.shape
    in_place = out_rows == tp
    assert in_place or out_rows == n_tiles * OTILE
    di = wo.shape[0]
    per = TILE // OTILE
    tokd = pl.BlockSpec((1, OTILE, d), lambda bi, j: (bi, j, 0))
    in_specs = [_out_group_spec(di)] * 2 + extra_specs + [
        pl.BlockSpec((di, d), lambda bi, j: (0, 0)),
        tokd,
        pl.BlockSpec((1, 1, 1, mod.shape[-1]), lambda bi, j: (bi, j // per, 0, 0)),
        pl.BlockSpec((1, d), lambda bi, j: (0, 0)),
    ]
    n_in = len(in_specs)
    return pl.pallas_call(
        kernel_fn,
        grid=(b, n_tiles),
        in_specs=in_specs,
        out_specs=tokd,
        out_shape=jax.ShapeDtypeStruct((b, out_rows, d), F32),
        scratch_shapes=[pltpu.VMEM((OTILE, di), BF16)],
        input_output_aliases={n_in - 3: 0} if in_place else {},
        name=name,
        compiler_params=pltpu.CompilerParams(
            dimension_semantics=("parallel", "parallel"), vmem_limit_bytes=VMEM_LIMIT),
    )(*ys, *extras, wo, xs, mod, post_g)


def _hg_feat_kernel(x_ref, mod_ref, g_ref, wq_ref, wf0_ref, wf1_ref, wi_ref, wg_ref, lb_ref,
                    q_o, i_o, sg_o, lf_o, kd_o, hbf):
    n = pl.program_id(1)
    j = pl.program_id(2)
    d = x_ref.shape[-1]
    rows = x_ref.shape[1]
    base = pl.multiple_of(j * rows, rows)

    @pl.when(n == 0)
    def _prologue():
        g = g_ref[...]
        for t in range(rows // TILE):
            m = mod_ref[0, t]
            shift, scale = m[:, :d], m[:, d:2 * d]

            def norm_rows(rb, c, t=t, shift=shift, scale=scale):
                r0 = pl.multiple_of(t * TILE + rb * GRID_W, GRID_W)
                hbf[pl.ds(base + r0, GRID_W), :] = _adaln(x_ref[0, pl.ds(r0, GRID_W), :], g, scale,
                                                          shift).astype(BF16)
                return c

            lax.fori_loop(0, TILE // GRID_W, norm_rows, 0)

    h = hbf[pl.ds(base, rows), :]
    lb = lb_ref[...]
    _store_groups(q_o, (0,), _silu(_dot(h, wq_ref[...])))
    for dr, wf_ref in enumerate((wf0_ref, wf1_ref)):
        f = lb + (1.0 - lb) * jax.nn.sigmoid(_dot(h, wf_ref[...]))
        _store_groups(kd_o, (dr, 0), 1.0 - f)
        _store_groups(lf_o, (dr, 0), jnp.log2(f))
    _store_groups(i_o, (0,), _dot(h, wi_ref[...]))
    _store_groups(sg_o, (0,), _silu(_dot(h, wg_ref[...])))


def _hg_features(xs, mod, pre_g, w_in, lb, *, tn=256):
    b, tp, d = xs.shape
    di = w_in.shape[-1] // 5
    nt = di // tn
    per = HG_TILES_PER_STEP if (tp // TILE) % HG_TILES_PER_STEP == 0 else 1
    rows = per * TILE
    nsteps = tp // rows
    _, _, s1, s2 = _group_major_specs(b, tp, di, tn)
    gq = tn // LANES
    one = pl.BlockSpec((1, gq, rows, LANES), lambda bi, n, j: (bi, n, j, 0))
    two = pl.BlockSpec((2, 1, gq, rows, LANES), lambda bi, n, j: (0, bi, n, j, 0))

    def tok_tile(n, j):
        return jnp.where(n == 0, j, nsteps - 1)

    return pl.pallas_call(
        _hg_feat_kernel,
        grid=(b, nt, nsteps),
        in_specs=[
            pl.BlockSpec((1, rows, d), lambda bi, n, j: (bi, tok_tile(n, j), 0)),
            pl.BlockSpec((1, per, 1, mod.shape[-1]), lambda bi, n, j: (bi, tok_tile(n, j), 0, 0)),
            pl.BlockSpec((1, d), lambda bi, n, j: (0, 0)),
            *[pl.BlockSpec((d, tn), lambda bi, n, j, part=part: (0, part * nt + n)) for part in range(5)],
            pl.BlockSpec((1, tn), lambda bi, n, j: (0, n)),
        ],
        out_specs=[one, one, one, two, two],
        out_shape=[s1(BF16), s1(BF16), s1(BF16), s2(F32), s2(BF16)],
        scratch_shapes=[pltpu.VMEM((tp, d), BF16)],
        name="hgrn2_features",
        compiler_params=pltpu.CompilerParams(
            dimension_semantics=("parallel", "arbitrary", "arbitrary"), vmem_limit_bytes=VMEM_LIMIT),
    )(xs, mod, pre_g, w_in, w_in, w_in, w_in, w_in, lb)


def _hg_chunk(q, k, v, lf2, st, rev):
    c = CHUNK
    sub = SUBLANES
    nh = q.shape[0]
    row = lax.broadcasted_iota(jnp.int32, (1, c, 1), 1)
    ri = lax.broadcasted_iota(jnp.int32, (1, c, c), 1)
    ci = lax.broadcasted_iota(jnp.int32, (1, c, c), 2)
    diag = jnp.sum(q * k, axis=2, keepdims=True)
    att = jnp.where(ri == ci, diag, jnp.zeros((nh, c, c), F32))
    shift = c - 1 if rev else 1
    later1 = (row % 2) == (0 if rev else 1)
    near = jnp.sum(q * jnp.exp2(lf2) * pltpu.roll(k, shift, axis=1), axis=2, keepdims=True)
    att = jnp.where(jnp.logical_and(ci == (ri + 1 if rev else ri - 1), later1), near, att)
    p = lf2 + jnp.where(later1, pltpu.roll(lf2, shift, axis=1), 0.0)
    m = 2
    while m < c:
        pair = jnp.logical_and(ri // (2 * m) == ci // (2 * m), ri // m != ci // m)
        if m >= sub:
            zeros = jnp.zeros((nh, m, LANES), BF16)
            q_parts, k_parts, p_parts = [], [], []
            for blk in range(c // (2 * m)):
                lo = slice(blk * 2 * m, blk * 2 * m + m)
                hi = slice(blk * 2 * m + m, (blk + 1) * 2 * m)
                early, later = (hi, lo) if rev else (lo, hi)
                bidx = blk * 2 * m + (m if rev else m - 1)
                tt = p[:, bidx:bidx + 1, :]
                q_l = (q[:, later] * jnp.exp2(p[:, later])).astype(BF16)
                k_e = (k[:, early] * jnp.exp2(tt - p[:, early])).astype(BF16)
                if rev:
                    q_parts += [q_l, zeros]
                    k_parts += [zeros, k_e]
                    p_parts += [p[:, later] + tt, p[:, early]]
                else:
                    q_parts += [zeros, q_l]
                    k_parts += [k_e, zeros]
                    p_parts += [p[:, early], p[:, later] + tt]
            mm = _bmm_nt(jnp.concatenate(q_parts, axis=1), jnp.concatenate(k_parts, axis=1))
            p = jnp.concatenate(p_parts, axis=1)
        else:
            in_hi = ((row // m) % 2) == 1
            late = jnp.logical_not(in_hi) if rev else in_hi
            p4 = p.reshape(nh, c // sub, sub, LANES)
            srow = lax.broadcasted_iota(jnp.int32, (1, 1, sub, 1), 2)
            tt4 = None
            for blk in range(sub // (2 * m)):
                bidx = blk * 2 * m + (m if rev else m - 1)
                cand = jnp.broadcast_to(p4[:, :, bidx:bidx + 1, :], p4.shape)
                tt4 = cand if tt4 is None else jnp.where(srow // (2 * m) == blk, cand, tt4)
            tt = tt4.reshape(nh, c, LANES)
            ee = jnp.exp2(jnp.where(late, p, tt - p))
            qt = jnp.where(late, q * ee, 0.0).astype(BF16)
            kt = jnp.where(late, 0.0, k * ee).astype(BF16)
            mm = _bmm_nt(qt, kt)
            p = jnp.where(late, p + tt, p)
        att = jnp.where(pair, mm, att)
        m *= 2
    tot = p[:, 0:1] if rev else p[:, c - 1:c]
    vb = v.astype(BF16)
    y = _bmm(att.astype(BF16), vb) + _bmm_nt((q * jnp.exp2(p)).astype(BF16), st.astype(BF16))
    st_new = st * jnp.exp2(tot) + _bmm_tn(vb, (k * jnp.exp2(tot - p)).astype(BF16))
    return y, st_new


def _hg_scan_kernel(*refs):
    ins = refs[:8]
    y_os = refs[8:10]
    st = refs[10]

    @pl.when(pl.program_id(1) == 0)
    def _():
        st[...] = jnp.zeros_like(st)

    def body(ci, carry):
        rows = _scan_rows(ci, HG_SCAN_CHUNKS)
        loaded = []
        for dr in range(2):
            q_r, v_r, kd_r, lf_r = ins[4 * dr:4 * dr + 4]
            rs = rows[dr]
            loaded.append((_load_chunk(q_r, (), rs), _load_chunk(kd_r, (0,), rs), _load_chunk(v_r, (), rs, BF16),
                           _load_chunk(lf_r, (0,), rs), st[dr]))
        results = _chunk_by_groups(_hg_chunk, loaded, HG_HEAD_SPLITS)
        for dr in range(2):
            y_os[dr][0, :, rows[dr]] = results[dr][0].astype(y_os[dr].dtype)
            st[dr] = results[dr][1]
        return carry

    _scan_or_pad(body, HG_SCAN_CHUNKS, y_os)


def _hg_scan(q, v, lf, kd, *, n_lat_chunks, n_ctx_chunks):
    b, ng, tp, _ = q.shape
    grid, in_specs, out_specs, out_shape = _scan_specs(b, ng, tp, n_lat_chunks, n_ctx_chunks, 2, 2, HG_SCAN_CHUNKS)
    return pl.pallas_call(
        _hg_scan_kernel,
        grid=grid,
        in_specs=in_specs,
        out_specs=out_specs,
        out_shape=out_shape,
        scratch_shapes=[pltpu.VMEM((2, ng, LANES, LANES), F32)],
        name="hgrn2_scan",
        compiler_params=pltpu.CompilerParams(
            dimension_semantics=("parallel", "arbitrary"), vmem_limit_bytes=VMEM_LIMIT),
    )(q, v, kd, lf, q, v, kd, lf)


def _pad_rows(w, rows):
    return jnp.pad(w, ((0, rows - w.shape[0]), (0, 0)))


def _pad_cols(w, cols):
    return jnp.pad(w, ((0, 0), (0, cols - w.shape[1])))


def kernel(x, c, ctx, c_ctx, mod_w, mod_b, pre_g, post_g, rw_mix, rw_proj, rw_wo, rw_w0, rw_w1, rw_w2, rw_a0,
           rw_a1, rw_a2, rw_v0, rw_v1, rw_v2, rw_kk, rw_ka, rw_rk, rw_lnw, rw_lnb, hg_win, hg_wo, hg_gn, hg_lb):
    b, seq, d = x.shape
    ctx_len = ctx.shape[1]
    depth = mod_w.shape[0]
    di = rw_proj.shape[-1]
    assert seq % TILE == 0 and ctx_len % CHUNK == 0 and ctx_len <= TILE and seq % GRID_W == 0
    n_lat_tiles = seq // TILE
    tp = seq + TILE
    n_lat_chunks, n_ctx_chunks = seq // CHUNK, ctx_len // CHUNK

    xs = jnp.concatenate([x, ctx, jnp.zeros((b, tp - seq - ctx_len, d), x.dtype)], axis=1)

    rows = ((b + 1 + 7) // 8) * 8
    cond = jnp.concatenate([c, c_ctx[None], jnp.zeros((rows - b - 1, d), c.dtype)], axis=0)
    mod_all = _modulation(cond, mod_w, mod_b)
    mod_lat = jnp.broadcast_to(mod_all[:, :b, None, :], (depth, b, n_lat_tiles, 3 * d))
    mod_ctx = jnp.broadcast_to(mod_all[:, b:b + 1, None, :], (depth, b, 1, 3 * d))
    mod_t = jnp.concatenate([mod_lat, mod_ctx], axis=2)[:, :, :, None, :]

    p_lb = jax.nn.softmax(hg_lb.astype(F32), axis=0)
    lb_all = jnp.cumsum(p_lb, axis=0) - p_lb[0]

    vf = None
    for i in range(depth):
        ctx_out = i < depth - 1
        out_rows = tp if ctx_out else seq
        n_out_tiles = out_rows // OTILE
        j = i // 2
        mod_i = mod_t[i]
        pg = pre_g[i][None]
        if i % 2 == 0:
            r64 = rw_w1.shape[-1]
            w2 = jnp.stack([jnp.pad(rw_w2[j, 0], ((0, LANES - r64), (0, 0))),
                            jnp.pad(rw_w2[j, 1], ((r64, LANES - 2 * r64), (0, 0)))])
            a2 = jnp.stack([jnp.pad(rw_a2[j, 0], ((0, LANES - r64), (0, 0))),
                            jnp.pad(rw_a2[j, 1], ((r64, LANES - 2 * r64), (0, 0)))])
            if j == 0:
                v0 = jnp.zeros((di,), F32)
                v1 = jnp.zeros((d, LANES), BF16)
                v2 = jnp.zeros((LANES, di), BF16)
            else:
                v0 = rw_v0[j - 1]
                v1 = _pad_cols(rw_v1[j - 1], LANES).astype(BF16)
                v2 = _pad_rows(rw_v2[j - 1], LANES).astype(BF16)
            p = {
                "mix": rw_mix[j],
                "proj": rw_proj[j].astype(BF16),
                "w1": jnp.concatenate([rw_w1[j, 0], rw_w1[j, 1]], axis=1).astype(BF16),
                "a1": jnp.concatenate([rw_a1[j, 0], rw_a1[j, 1]], axis=1).astype(BF16),
                "v1": v1,
                "w2": w2.astype(BF16),
                "a2": a2.astype(BF16),
                "v2": v2,
                "vec": jnp.stack([rw_w0[j, 0], rw_w0[j, 1], rw_a0[j, 0], rw_a0[j, 1], v0, rw_kk[j], rw_ka[j],
                                  rw_rk[j]]),
            }
            r, v, kk, z, bv, lw, bb, kd = _rw_features(xs, mod_i, pg, p, vf if j > 0 else None,
                                                       n_lat_tiles=n_lat_tiles, ctx_len=ctx_len)
            if j == 0:
                vf = v
            yf, yb = _rw_scan((r, v, kk, lw, bb, kd), n_lat_chunks=n_lat_chunks, n_ctx_chunks=n_ctx_chunks)
            gspec = _out_group_spec(di)
            extras = [z, bv, rw_lnw[j][None], rw_lnb[j][None], _head_sum_matrix(2 * LANES, RW_HEAD)]
            especs = [gspec, gspec, pl.BlockSpec((1, di), lambda bi, jj: (0, 0)),
                      pl.BlockSpec((1, di), lambda bi, jj: (0, 0)), pl.BlockSpec((2 * LANES, 2 * LANES), lambda bi, jj: (0, 0))]
            xs = _mixer_out("rwkv7_out", _rw_out_kernel, (yf, yb), extras, especs, rw_wo[j].astype(BF16), xs, mod_i,
                            post_g[i][None], n_out_tiles, out_rows)
        else:
            q, iv, sg, lf, kd = _hg_features(xs, mod_i, pg, hg_win[j].astype(BF16), lb_all[i][None])
            yf, yb = _hg_scan(q, iv, lf, kd, n_lat_chunks=n_lat_chunks, n_ctx_chunks=n_ctx_chunks)
            extras = [sg, hg_gn[j][None]]
            especs = [_out_group_spec(di), pl.BlockSpec((1, HG_HEAD), lambda bi, jj: (0, 0))]
            xs = _mixer_out("hgrn2_out", _hg_out_kernel, (yf, yb), extras, especs, hg_wo[j].astype(BF16), xs, mod_i,
                            post_g[i][None], n_out_tiles, out_rows)
    return xs
```

```python
import functools
import math

import jax
import jax.numpy as jnp
from jax import lax
from jax.experimental import pallas as pl
from jax.experimental.pallas import tpu as pltpu

F32 = jnp.float32
BF16 = jnp.bfloat16
HI = lax.Precision.HIGHEST

GRID_W = 64
RW_HEAD = 64
HG_HEAD = 128
CHUNK = 64
TILE = 512
OTILE = 512
HG_TILES_PER_STEP = 3
RW_SCAN_CHUNKS = 4
HG_SCAN_CHUNKS = 4
HG_HEAD_SPLITS = 4
NORM_EPS = 1e-6
LN_X_EPS = 64e-5
LANES = 128
SUBLANES = 8
V7X_VMEM_BYTES = 64 * 1024 * 1024
VMEM_LIMIT = V7X_VMEM_BYTES * 3 // 4


def _silu(x):
    return x * jax.nn.sigmoid(x)


def _dot(a, b):
    return jnp.dot(a, b, preferred_element_type=F32)


def _mod_kernel(s_ref, w_ref, b_ref, o_ref):
    s = _silu(s_ref[...])
    o_ref[0] = jnp.dot(s, w_ref[0], precision=HI, preferred_element_type=F32) + b_ref[0]


def _modulation(cond, mod_w, mod_b):
    depth, d, d3 = mod_w.shape
    rows = cond.shape[0]
    nb = d3 // d
    return pl.pallas_call(
        _mod_kernel,
        grid=(depth, nb),
        in_specs=[
            pl.BlockSpec((rows, d), lambda i, n: (0, 0)),
            pl.BlockSpec((1, d, d), lambda i, n: (i, 0, n)),
            pl.BlockSpec((1, 1, d), lambda i, n: (i, 0, n)),
        ],
        out_specs=pl.BlockSpec((1, rows, d), lambda i, n: (i, 0, n)),
        out_shape=jax.ShapeDtypeStruct((depth, rows, d3), F32),
        name="adaln_modulation",
        compiler_params=pltpu.CompilerParams(vmem_limit_bytes=VMEM_LIMIT),
    )(cond, mod_w, mod_b.reshape(depth, 1, d3))


def _adaln(xb, g, scale, shift):
    ms = jnp.mean(xb * xb, axis=-1, keepdims=True)
    return xb * lax.rsqrt(ms + NORM_EPS) * g * (1.0 + scale) + shift


def _store_groups(o_ref, lead, val):
    for gq in range(val.shape[1] // LANES):
        o_ref[lead + (gq,)] = val[:, gq * LANES:(gq + 1) * LANES].astype(o_ref.dtype)


def _load_groups(ref, lead):
    n = ref.shape[len(lead)]
    return jnp.concatenate([ref[lead + (gq,)] for gq in range(n)], axis=1)


def _head_sum_matrix(width, head):
    r = lax.broadcasted_iota(jnp.int32, (width, width), 0) // head
    c = lax.broadcasted_iota(jnp.int32, (width, width), 1) // head
    return (r == c).astype(BF16)


def _rw_feat_kernel(n_lat_tiles, ctx_len, has_vres, *refs):
    (x_ref, xp_ref, xn_ref, mod_ref, g_ref, mix_ref, w1_ref, a1_ref, v1_ref,
     proj_ref, w2_ref, a2_ref, v2_ref, vec_ref, hs_ref) = refs[:15]
    pos = 15
    vf_ref = None
    if has_vres:
        vf_ref = refs[pos]
        pos += 1
    (r_o, v_o, kk_o, z_o, bv_o, lw_o, bb_o, kd_o) = refs[pos:pos + 8]
    hbuf, xm, tw, ta, tv = refs[pos + 8:]

    j = pl.program_id(1)
    n = pl.program_id(2)
    d = x_ref.shape[-1]
    q4 = d // 4
    nblk = TILE // GRID_W

    @pl.when(n == 0)
    def _prologue():
        m = mod_ref[0, 0]
        shift, scale = m[:, :d], m[:, d:2 * d]
        g = g_ref[...]
        is_lat = j < n_lat_tiles
        prev_ok = jnp.logical_and(j >= 1, is_lat)
        next_ok = j < n_lat_tiles - 1

        def norm_rows(rb, c):
            r0 = pl.multiple_of(rb * GRID_W, GRID_W)
            hbuf[pl.ds(r0 + GRID_W, GRID_W), :] = _adaln(x_ref[0, pl.ds(r0, GRID_W), :], g, scale, shift)
            return c

        lax.fori_loop(0, nblk, norm_rows, 0, unroll=4)
        hp = _adaln(xp_ref[0], g, scale, shift)
        hbuf[0:GRID_W, :] = jnp.where(prev_ok, hp, 0.0)
        hn = _adaln(xn_ref[0], g, scale, shift)
        hbuf[TILE + GRID_W:TILE + 2 * GRID_W, :] = jnp.where(next_ok, hn, 0.0)

        row = lax.broadcasted_iota(jnp.int32, (GRID_W, 1), 0)

        def store_mixed(r0, cs, h_c, hs_c):
            dd = hs_c - h_c
            for k in range(6):
                xm[k, pl.ds(r0, GRID_W), cs] = (h_c + dd * mix_ref[k:k + 1, cs]).astype(BF16)

        def mix_lat(rb, c):
            r0 = pl.multiple_of(rb * GRID_W, GRID_W)
            for qi in range(4):
                cs = slice(qi * q4, (qi + 1) * q4)
                h_c = hbuf[pl.ds(r0 + GRID_W, GRID_W), cs]
                if qi == 0:
                    hs_c = jnp.where(row == 0, 0.0, pltpu.roll(h_c, 1, axis=0))
                elif qi == 1:
                    hs_c = jnp.where(row == GRID_W - 1, 0.0, pltpu.roll(h_c, GRID_W - 1, axis=0))
                elif qi == 2:
                    hs_c = hbuf[pl.ds(r0, GRID_W), cs]
                else:
                    hs_c = hbuf[pl.ds(r0 + 2 * GRID_W, GRID_W), cs]
                store_mixed(r0, cs, h_c, hs_c)
            return c

        def mix_ctx(rb, c):
            r0 = pl.multiple_of(rb * GRID_W, GRID_W)
            t = row + rb * GRID_W
            for qi in range(4):
                cs = slice(qi * q4, (qi + 1) * q4)
                h_c = hbuf[pl.ds(r0 + GRID_W, GRID_W), cs]
                if qi < 2:
                    edge = hbuf[pl.ds(r0 + GRID_W - 1, 1), cs]
                    hs_c = jnp.where(row == 0, edge, pltpu.roll(h_c, 1, axis=0))
                else:
                    edge = hbuf[pl.ds(r0 + 2 * GRID_W, 1), cs]
                    hs_c = jnp.where(row == GRID_W - 1, edge, pltpu.roll(h_c, GRID_W - 1, axis=0))
                    hs_c = jnp.where(t == ctx_len - 1, 0.0, hs_c)
                store_mixed(r0, cs, h_c, hs_c)
            return c

        @pl.when(is_lat)
        def _():
            lax.fori_loop(0, nblk, mix_lat, 0)

        @pl.when(jnp.logical_not(is_lat))
        def _():
            lax.fori_loop(0, nblk, mix_ctx, 0)

        tw[...] = jnp.tanh(_dot(xm[1], w1_ref[...])).astype(BF16)
        ta[...] = _dot(xm[4], a1_ref[...]).astype(BF16)
        if has_vres:
            tv[...] = _dot(xm[3], v1_ref[...]).astype(BF16)

    vec = vec_ref[...]
    r = _dot(xm[0], proj_ref[0, 0])
    k = _dot(xm[2], proj_ref[0, 1])
    v = _dot(xm[3], proj_ref[0, 2])
    z = _silu(_dot(xm[5], proj_ref[0, 3]))
    if has_vres:
        gate = jax.nn.sigmoid(vec[4:5] + _dot(tv[...], v2_ref[...]))
        v = v + (_load_groups(vf_ref, (0,)).astype(F32) - v) * gate
    hs = hs_ref[...]
    kk = k * vec[5:6]
    ss = _dot((kk * kk).astype(BF16), hs)
    kk = kk * lax.rsqrt(jnp.maximum(ss, 1e-24))
    decay_scale = -math.exp(-0.5) * math.log2(math.e)
    ksum = None
    for dr in range(2):
        u = vec[dr:dr + 1] + _dot(tw[...], w2_ref[dr])
        _store_groups(lw_o, (dr, 0), decay_scale * jax.nn.sigmoid(u))
        a = jax.nn.sigmoid(vec[2 + dr:3 + dr] + _dot(ta[...], a2_ref[dr]))
        kd = k * (1.0 + (a - 1.0) * vec[6:7])
        _store_groups(bb_o, (dr, 0), kk * a)
        _store_groups(kd_o, (dr, 0), kd)
        ksum = kd if ksum is None else ksum + kd
    bonus = _dot((r * ksum * vec[7:8]).astype(BF16), hs)
    _store_groups(r_o, (0,), r)
    _store_groups(v_o, (0,), v)
    _store_groups(kk_o, (0,), kk)
    _store_groups(z_o, (0,), z)
    _store_groups(bv_o, (0,), bonus * v)


def _group_major_specs(b, tp, di, tn):
    gq = tn // LANES
    one = pl.BlockSpec((1, gq, TILE, LANES), lambda bi, j, n: (bi, n, j, 0))
    two = pl.BlockSpec((2, 1, gq, TILE, LANES), lambda bi, j, n: (0, bi, n, j, 0))
    s1 = lambda dt: jax.ShapeDtypeStruct((b, di // LANES, tp, LANES), dt)
    s2 = lambda dt: jax.ShapeDtypeStruct((2, b, di // LANES, tp, LANES), dt)
    return one, two, s1, s2


def _rw_features(xs, mod, pre_g, p, vf, *, n_lat_tiles, ctx_len, tn=256):
    b, tp, d = xs.shape
    di = p["proj"].shape[-1]
    nt = di // tn
    ntiles = tp // TILE
    nb64 = tp // GRID_W
    per = TILE // GRID_W
    has_vres = vf is not None

    def tok(bi, j, n):
        return (bi, j, 0)

    in_specs = [
        pl.BlockSpec((1, TILE, d), tok),
        pl.BlockSpec((1, GRID_W, d), lambda bi, j, n: (bi, jnp.maximum(j * per - 1, 0), 0)),
        pl.BlockSpec((1, GRID_W, d), lambda bi, j, n: (bi, jnp.minimum((j + 1) * per, nb64 - 1), 0)),
        pl.BlockSpec((1, 1, 1, mod.shape[-1]), lambda bi, j, n: (bi, j, 0, 0)),
        pl.BlockSpec((1, d), lambda bi, j, n: (0, 0)),
        pl.BlockSpec((6, d), lambda bi, j, n: (0, 0)),
        pl.BlockSpec((d, LANES), lambda bi, j, n: (0, 0)),
        pl.BlockSpec((d, LANES), lambda bi, j, n: (0, 0)),
        pl.BlockSpec((d, LANES), lambda bi, j, n: (0, 0)),
        pl.BlockSpec((1, 4, d, tn), lambda bi, j, n: (0, 0, 0, n)),
        pl.BlockSpec((2, LANES, tn), lambda bi, j, n: (0, 0, n)),
        pl.BlockSpec((2, LANES, tn), lambda bi, j, n: (0, 0, n)),
        pl.BlockSpec((LANES, tn), lambda bi, j, n: (0, n)),
        pl.BlockSpec((8, tn), lambda bi, j, n: (0, n)),
        pl.BlockSpec((tn, tn), lambda bi, j, n: (0, 0)),
    ]
    args = [xs, xs, xs, mod, pre_g, p["mix"], p["w1"], p["a1"], p["v1"], p["proj"][None], p["w2"], p["a2"],
            p["v2"], p["vec"], _head_sum_matrix(tn, RW_HEAD)]
    one, two, s1, s2 = _group_major_specs(b, tp, di, tn)
    if has_vres:
        in_specs.append(one)
        args.append(vf)
    return pl.pallas_call(
        functools.partial(_rw_feat_kernel, n_lat_tiles, ctx_len, has_vres),
        grid=(b, ntiles, nt),
        in_specs=in_specs,
        out_specs=[one, one, one, one, one, two, two, two],
        out_shape=[s1(BF16), s1(BF16), s1(BF16), s1(BF16), s1(BF16), s2(F32), s2(BF16), s2(BF16)],
        scratch_shapes=[
            pltpu.VMEM((TILE + 2 * GRID_W, d), F32),
            pltpu.VMEM((6, TILE, d), BF16),
            pltpu.VMEM((TILE, LANES), BF16),
            pltpu.VMEM((TILE, LANES), BF16),
            pltpu.VMEM((TILE, LANES), BF16),
        ],
        name="rwkv7_features",
        compiler_params=pltpu.CompilerParams(
            dimension_semantics=("parallel", "parallel", "arbitrary"), vmem_limit_bytes=VMEM_LIMIT),
    )(*args)


def _bmm(a, b):
    return lax.dot_general(a, b, (((2,), (1,)), ((0,), (0,))), preferred_element_type=F32)


def _bmm_nt(a, b):
    return lax.dot_general(a, b, (((2,), (2,)), ((0,), (0,))), preferred_element_type=F32)


def _bmm_tn(a, b):
    return lax.dot_general(a, b, (((1,), (1,)), ((0,), (0,))), preferred_element_type=F32)


def _chunk_cumsum(x, rev):
    c = x.shape[1]
    row = lax.broadcasted_iota(jnp.int32, (1, c, 1), 1)
    s = 1
    while s < c:
        if rev:
            x = x + jnp.where(row < c - s, pltpu.roll(x, c - s, axis=1), 0.0)
        else:
            x = x + jnp.where(row >= s, pltpu.roll(x, s, axis=1), 0.0)
        s *= 2
    return x


def _scan_chunk_maps(n_lat_chunks, n_ctx_chunks):
    total = n_lat_chunks + n_ctx_chunks

    def fwd(s):
        return jnp.where(s < total, (s + n_lat_chunks) % total, total)

    def bwd(s):
        return jnp.where(s < total, total - 1 - s, total)

    return total, fwd, bwd


def _scan_specs(b, ng, tp, n_lat_chunks, n_ctx_chunks, n_one, n_two, nchunks):
    assert n_lat_chunks % nchunks == 0 and n_ctx_chunks % nchunks == 0
    rows = CHUNK * nchunks
    total, fwd, bwd = _scan_chunk_maps(n_lat_chunks // nchunks, n_ctx_chunks // nchunks)
    assert tp == (total + 1) * rows
    in_specs = []
    for dr, cm in enumerate((fwd, bwd)):
        one = pl.BlockSpec((1, ng, rows, LANES), lambda bi, s, cm=cm: (bi, 0, cm(s), 0))
        two = pl.BlockSpec((1, 1, ng, rows, LANES), lambda bi, s, cm=cm, dr=dr: (dr, bi, 0, cm(s), 0))
        in_specs += [one] * n_one + [two] * n_two
    out_specs = [pl.BlockSpec((1, ng, rows, LANES), lambda bi, s, cm=cm: (bi, 0, cm(s), 0)) for cm in (fwd, bwd)]
    out_shape = [jax.ShapeDtypeStruct((b, ng, tp, LANES), BF16)] * 2
    return (b, total + 1), in_specs, out_specs, out_shape


def _scan_rows(ci, nchunks):
    fwd = pl.ds(pl.multiple_of(ci * CHUNK, CHUNK), CHUNK)
    bwd = pl.ds(pl.multiple_of((nchunks - 1 - ci) * CHUNK, CHUNK), CHUNK)
    return fwd, bwd


def _scan_or_pad(body, nchunks, y_os):
    s = pl.program_id(1)
    last = pl.num_programs(1) - 1

    @pl.when(s < last)
    def _():
        lax.fori_loop(0, nchunks, body, 0)

    @pl.when(s == last)
    def _():
        for y_o in y_os:
            y_o[...] = jnp.zeros_like(y_o)


def _chunk_by_groups(chunk_fn, loaded, splits):
    n = loaded[0][0].shape[0]
    step = n // splits
    parts = [[], []]
    for part in range(splits):
        gs = slice(part * step, (part + 1) * step)
        for dr in range(2):
            parts[dr].append(chunk_fn(*[a[gs] for a in loaded[dr]], rev=(dr == 1)))
    return [tuple(jnp.concatenate([p[i] for p in parts[dr]], axis=0) for i in range(2)) for dr in range(2)]


def _load_chunk(ref, lead, rs, dtype=F32):
    return ref[lead + (0, slice(None), rs)].astype(dtype)


def _rw_chunk(r, v, kk, bb, kd, lw, st, rev):
    c = CHUNK
    half = RW_HEAD
    npair = r.shape[0]
    ri = lax.broadcasted_iota(jnp.int32, (1, c, LANES), 1)
    ci = lax.broadcasted_iota(jnp.int32, (1, c, LANES), 2) % half
    head_a = lax.broadcasted_iota(jnp.int32, (1, 1, LANES), 2) < half

    def split_rows(x):
        zero = jnp.zeros_like(x)
        return jnp.concatenate([jnp.where(head_a, x, zero), jnp.where(head_a, zero, x)], axis=1)

    if rev:
        incl, strict = ci >= ri, ci > ri
    else:
        incl, strict = ci <= ri, ci < ri
    cum = _chunk_cumsum(lw, rev)
    tot = cum[:, 0:1] if rev else cum[:, c - 1:c]
    e_in = jnp.exp2(cum)
    e_prev = jnp.exp2(cum - lw)
    e_inv = jnp.exp2(-cum)
    e_out = jnp.exp2(tot - cum)
    rt = (r * e_in).astype(BF16)
    at = (-kk * e_prev).astype(BF16)
    bi = (bb * e_inv).astype(BF16)
    ki = (kd * e_inv).astype(BF16)
    bh = (bb * e_out).astype(BF16)
    kh = (kd * e_out).astype(BF16)
    vb = v.astype(BF16)
    sb = st.astype(BF16)
    lhs = jnp.concatenate([at, rt], axis=1)
    h0 = _bmm_nt(lhs, sb)
    m_b = _bmm_nt(lhs, split_rows(bi))
    m_k = _bmm_nt(lhs, split_rows(ki))
    tri = jnp.concatenate([strict, incl], axis=1)
    m_b = m_b.astype(BF16)
    m_b = jnp.where(tri, m_b, jnp.zeros_like(m_b))
    m_k = m_k.astype(BF16)
    m_k = jnp.where(tri, m_k, jnp.zeros_like(m_k))
    ab16, a_rb = m_b[:, :c], m_b[:, c:]
    zero16 = jnp.zeros_like(ab16)
    a1 = jnp.where(ri // 2 == ci // 2, ab16, zero16)
    pmb = jnp.where(ri == ci, jnp.ones_like(ab16), a1)
    off = jnp.logical_and(ri // 4 == ci // 4, ri // 2 != ci // 2)
    x2 = jnp.where(off, ab16, zero16).astype(F32)
    a1f = a1.astype(F32)
    a1_row = jnp.where(head_a, jnp.sum(jnp.where(head_a, a1f, 0.0), axis=2, keepdims=True),
                       jnp.sum(jnp.where(head_a, 0.0, a1f), axis=2, keepdims=True))
    a1_col = jnp.sum(a1f, axis=1, keepdims=True)
    y2 = x2 + a1_row * pltpu.roll(x2, c - 1 if rev else 1, axis=1)
    d4 = y2 + pltpu.roll(y2, 1 if rev else LANES - 1, axis=2) * a1_col
    pmb = jnp.where(off, d4.astype(BF16), pmb)
    m = 4
    while m < c:
        off = jnp.logical_and(ri // (2 * m) == ci // (2 * m), ri // m != ci // m)
        a_off = jnp.where(off, ab16, zero16)
        delta = _bmm(_bmm(pmb, split_rows(a_off)).astype(BF16), split_rows(pmb))
        pmb = jnp.where(off, delta.astype(BF16), pmb)
        m *= 2
    av = _bmm(m_k, split_rows(vb))
    x = h0[:, :c] + av[:, :c]
    u = _bmm(pmb, split_rows(x.astype(BF16)))
    ub = u.astype(BF16)
    o_full = h0[:, c:] + av[:, c:] + _bmm(a_rb, split_rows(ub))
    upd = _bmm_tn(jnp.concatenate([ub, vb], axis=1), jnp.concatenate([bh, kh], axis=1))
    blk_r = lax.broadcasted_iota(jnp.int32, (1, LANES, LANES), 1) // half
    blk_c = lax.broadcasted_iota(jnp.int32, (1, LANES, LANES), 2) // half
    st_new = st * jnp.exp2(tot) + jnp.where(blk_r == blk_c, upd, 0.0)
    return o_full, st_new


def _rw_scan_kernel(*refs):
    ins = refs[:12]
    y_os = refs[12:14]
    st = refs[14]

    @pl.when(pl.program_id(1) == 0)
    def _():
        st[...] = jnp.zeros_like(st)

    def body(ci, carry):
        rows = _scan_rows(ci, RW_SCAN_CHUNKS)
        loaded = []
        for dr in range(2):
            r_r, v_r, kk_r, bb_r, kd_r, lw_r = ins[6 * dr:6 * dr + 6]
            rs = rows[dr]
            loaded.append((_load_chunk(r_r, (), rs), _load_chunk(v_r, (), rs), _load_chunk(kk_r, (), rs),
                           _load_chunk(bb_r, (0,), rs), _load_chunk(kd_r, (0,), rs), _load_chunk(lw_r, (0,), rs),
                           st[dr]))
        results = _chunk_by_groups(_rw_chunk, loaded, 1)
        for dr in range(2):
            y_os[dr][0, :, rows[dr]] = results[dr][0].astype(y_os[dr].dtype)
            st[dr] = results[dr][1]
        return carry

    _scan_or_pad(body, RW_SCAN_CHUNKS, y_os)


def _rw_scan(feats, *, n_lat_chunks, n_ctx_chunks):
    r, v, kk, lw, bb, kd = feats
    b, ng, tp, _ = r.shape
    grid, in_specs, out_specs, out_shape = _scan_specs(b, ng, tp, n_lat_chunks, n_ctx_chunks, 3, 3, RW_SCAN_CHUNKS)
    return pl.pallas_call(
        _rw_scan_kernel,
        grid=grid,
        in_specs=in_specs,
        out_specs=out_specs,
        out_shape=out_shape,
        scratch_shapes=[pltpu.VMEM((2, ng, LANES, LANES), F32)],
        name="rwkv7_scan",
        compiler_params=pltpu.CompilerParams(
            dimension_semantics=("parallel", "arbitrary"), vmem_limit_bytes=VMEM_LIMIT),
    )(r, v, kk, bb, kd, lw, r, v, kk, bb, kd, lw)


def _residual_tail(yg_bf16, wo_ref, x_ref, mod_ref, pg_ref, o_ref):
    d = x_ref.shape[-1]
    o = _dot(yg_bf16, wo_ref[...])
    ms = jnp.mean(o * o, axis=-1, keepdims=True)
    o = o * lax.rsqrt(ms + NORM_EPS) * pg_ref[...]
    gate = mod_ref[0, 0][:, 2 * d:]
    o_ref[0] = x_ref[0] + gate * o


def _rw_out_kernel(yf_ref, yb_ref, z_ref, bv_ref, lnw_ref, lnb_ref, hs_ref, wo_ref, x_ref, mod_ref, pg_ref,
                   o_ref, yg):
    inv = 1.0 / RW_HEAD
    hs = hs_ref[...]
    per = hs.shape[0] // LANES

    def groups(ref, g0):
        return jnp.concatenate([ref[0, g0 + i] for i in range(per)], axis=1)

    for g0 in range(0, yf_ref.shape[1], per):
        cs = slice(g0 * LANES, (g0 + per) * LANES)
        y = groups(yf_ref, g0).astype(F32) + groups(yb_ref, g0).astype(F32)
        mu = _dot(y.astype(BF16), hs) * inv
        yc = y - mu
        var = _dot((yc * yc).astype(BF16), hs) * inv
        yn = yc * lax.rsqrt(var + LN_X_EPS) * lnw_ref[:, cs] + lnb_ref[:, cs] + groups(bv_ref, g0).astype(F32)
        yg[:, cs] = yn.astype(BF16) * groups(z_ref, g0)
    _residual_tail(yg[...], wo_ref, x_ref, mod_ref, pg_ref, o_ref)


def _hg_out_kernel(yf_ref, yb_ref, sg_ref, gn_ref, wo_ref, x_ref, mod_ref, pg_ref, o_ref, yg):
    for hx in range(yf_ref.shape[1]):
        cs = slice(hx * LANES, (hx + 1) * LANES)
        y = yf_ref[0, hx].astype(F32) + yb_ref[0, hx].astype(F32)
        ms = jnp.mean(y * y, axis=-1, keepdims=True)
        yn = y * lax.rsqrt(ms + NORM_EPS) * gn_ref[...]
        yg[:, cs] = yn.astype(BF16) * sg_ref[0, hx]
    _residual_tail(yg[...], wo_ref, x_ref, mod_ref, pg_ref, o_ref)


def _out_group_spec(di):
    return pl.BlockSpec((1, di // LANES, OTILE, LANES), lambda bi, j: (bi, 0, j, 0))


def _mixer_out(name, kernel_fn, ys, extras, extra_specs, wo, xs, mod, post_g, n_tiles, out_rows):
    b, tp, d = xs.shape
    in_place = out_rows == tp
    assert in_place or out_rows == n_tiles * OTILE
    di = wo.shape[0]
    per = TILE // OTILE
    tokd = pl.BlockSpec((1, OTILE, d), lambda bi, j: (bi, j, 0))
    in_specs = [_out_group_spec(di)] * 2 + extra_specs + [
        pl.BlockSpec((di, d), lambda bi, j: (0, 0)),
        tokd,
        pl.BlockSpec((1, 1, 1, mod.shape[-1]), lambda bi, j: (bi, j // per, 0, 0)),
        pl.BlockSpec((1, d), lambda bi, j: (0, 0)),
    ]
    n_in = len(in_specs)
    return pl.pallas_call(
        kernel_fn,
        grid=(b, n_tiles),
        in_specs=in_specs,
        out_specs=tokd,
        out_shape=jax.ShapeDtypeStruct((b, out_rows, d), F32),
        scratch_shapes=[pltpu.VMEM((OTILE, di), BF16)],
        input_output_aliases={n_in - 3: 0} if in_place else {},
        name=name,
        compiler_params=pltpu.CompilerParams(
            dimension_semantics=("parallel", "parallel"), vmem_limit_bytes=VMEM_LIMIT),
    )(*ys, *extras, wo, xs, mod, post_g)


def _hg_feat_kernel(x_ref, mod_ref, g_ref, wq_ref, wf0_ref, wf1_ref, wi_ref, wg_ref, lb_ref,
                    q_o, i_o, sg_o, lf_o, kd_o, hbf):
    n = pl.program_id(1)
    j = pl.program_id(2)
    d = x_ref.shape[-1]
    rows = x_ref.shape[1]
    base = pl.multiple_of(j * rows, rows)

    @pl.when(n == 0)
    def _prologue():
        g = g_ref[...]
        for t in range(rows // TILE):
            m = mod_ref[0, t]
            shift, scale = m[:, :d], m[:, d:2 * d]

            def norm_rows(rb, c, t=t, shift=shift, scale=scale):
                r0 = pl.multiple_of(t * TILE + rb * GRID_W, GRID_W)
                hbf[pl.ds(base + r0, GRID_W), :] = _adaln(x_ref[0, pl.ds(r0, GRID_W), :], g, scale,
                                                          shift).astype(BF16)
                return c

            lax.fori_loop(0, TILE // GRID_W, norm_rows, 0, unroll=4)

    h = hbf[pl.ds(base, rows), :]
    lb = lb_ref[...]
    _store_groups(q_o, (0,), _silu(_dot(h, wq_ref[...])))
    for dr, wf_ref in enumerate((wf0_ref, wf1_ref)):
        f = lb + (1.0 - lb) * jax.nn.sigmoid(_dot(h, wf_ref[...]))
        _store_groups(kd_o, (dr, 0), 1.0 - f)
        _store_groups(lf_o, (dr, 0), jnp.log2(f))
    _store_groups(i_o, (0,), _dot(h, wi_ref[...]))
    _store_groups(sg_o, (0,), _silu(_dot(h, wg_ref[...])))


def _hg_features(xs, mod, pre_g, w_in, lb, *, tn=256):
    b, tp, d = xs.shape
    di = w_in.shape[-1] // 5
    nt = di // tn
    per = HG_TILES_PER_STEP if (tp // TILE) % HG_TILES_PER_STEP == 0 else 1
    rows = per * TILE
    nsteps = tp // rows
    _, _, s1, s2 = _group_major_specs(b, tp, di, tn)
    gq = tn // LANES
    one = pl.BlockSpec((1, gq, rows, LANES), lambda bi, n, j: (bi, n, j, 0))
    two = pl.BlockSpec((2, 1, gq, rows, LANES), lambda bi, n, j: (0, bi, n, j, 0))

    def tok_tile(n, j):
        return jnp.where(n == 0, j, nsteps - 1)

    return pl.pallas_call(
        _hg_feat_kernel,
        grid=(b, nt, nsteps),
        in_specs=[
            pl.BlockSpec((1, rows, d), lambda bi, n, j: (bi, tok_tile(n, j), 0)),
            pl.BlockSpec((1, per, 1, mod.shape[-1]), lambda bi, n, j: (bi, tok_tile(n, j), 0, 0)),
            pl.BlockSpec((1, d), lambda bi, n, j: (0, 0)),
            *[pl.BlockSpec((d, tn), lambda bi, n, j, part=part: (0, part * nt + n)) for part in range(5)],
            pl.BlockSpec((1, tn), lambda bi, n, j: (0, n)),
        ],
        out_specs=[one, one, one, two, two],
        out_shape=[s1(BF16), s1(BF16), s1(BF16), s2(F32), s2(BF16)],
        scratch_shapes=[pltpu.VMEM((tp, d), BF16)],
        name="hgrn2_features",
        compiler_params=pltpu.CompilerParams(
            dimension_semantics=("parallel", "arbitrary", "arbitrary"), vmem_limit_bytes=VMEM_LIMIT),
    )(xs, mod, pre_g, w_in, w_in, w_in, w_in, w_in, lb)


def _hg_chunk(q, k, v, lf2, st, rev):
    c = CHUNK
    sub = SUBLANES
    nh = q.shape[0]
    row = lax.broadcasted_iota(jnp.int32, (1, c, 1), 1)
    ri = lax.broadcasted_iota(jnp.int32, (1, c, c), 1)
    ci = lax.broadcasted_iota(jnp.int32, (1, c, c), 2)
    diag = jnp.sum(q * k, axis=2, keepdims=True)
    att = jnp.where(ri == ci, diag, jnp.zeros((nh, c, c), F32))
    shift = c - 1 if rev else 1
    later1 = (row % 2) == (0 if rev else 1)
    near = jnp.sum(q * jnp.exp2(lf2) * pltpu.roll(k, shift, axis=1), axis=2, keepdims=True)
    att = jnp.where(jnp.logical_and(ci == (ri + 1 if rev else ri - 1), later1), near, att)
    p = lf2 + jnp.where(later1, pltpu.roll(lf2, shift, axis=1), 0.0)
    m = 2
    while m < c:
        pair = jnp.logical_and(ri // (2 * m) == ci // (2 * m), ri // m != ci // m)
        if m >= sub:
            zeros = jnp.zeros((nh, m, LANES), BF16)
            q_parts, k_parts, p_parts = [], [], []
            for blk in range(c // (2 * m)):
                lo = slice(blk * 2 * m, blk * 2 * m + m)
                hi = slice(blk * 2 * m + m, (blk + 1) * 2 * m)
                early, later = (hi, lo) if rev else (lo, hi)
                bidx = blk * 2 * m + (m if rev else m - 1)
                tt = p[:, bidx:bidx + 1, :]
                q_l = (q[:, later] * jnp.exp2(p[:, later])).astype(BF16)
                k_e = (k[:, early] * jnp.exp2(tt - p[:, early])).astype(BF16)
                if rev:
                    q_parts += [q_l, zeros]
                    k_parts += [zeros, k_e]
                    p_parts += [p[:, later] + tt, p[:, early]]
                else:
                    q_parts += [zeros, q_l]
                    k_parts += [k_e, zeros]
                    p_parts += [p[:, early], p[:, later] + tt]
            mm = _bmm_nt(jnp.concatenate(q_parts, axis=1), jnp.concatenate(k_parts, axis=1))
            p = jnp.concatenate(p_parts, axis=1)
        else:
            in_hi = ((row // m) % 2) == 1
            late = jnp.logical_not(in_hi) if rev else in_hi
            p4 = p.reshape(nh, c // sub, sub, LANES)
            srow = lax.broadcasted_iota(jnp.int32, (1, 1, sub, 1), 2)
            tt4 = None
            for blk in range(sub // (2 * m)):
                bidx = blk * 2 * m + (m if rev else m - 1)
                cand = jnp.broadcast_to(p4[:, :, bidx:bidx + 1, :], p4.shape)
                tt4 = cand if tt4 is None else jnp.where(srow // (2 * m) == blk, cand, tt4)
            tt = tt4.reshape(nh, c, LANES)
            ee = jnp.exp2(jnp.where(late, p, tt - p))
            qt = jnp.where(late, q * ee, 0.0).astype(BF16)
            kt = jnp.where(late, 0.0, k * ee).astype(BF16)
            mm = _bmm_nt(qt, kt)
            p = jnp.where(late, p + tt, p)
        att = jnp.where(pair, mm, att)
        m *= 2
    tot = p[:, 0:1] if rev else p[:, c - 1:c]
    vb = v.astype(BF16)
    y = _bmm(att.astype(BF16), vb) + _bmm_nt((q * jnp.exp2(p)).astype(BF16), st.astype(BF16))
    st_new = st * jnp.exp2(tot) + _bmm_tn(vb, (k * jnp.exp2(tot - p)).astype(BF16))
    return y, st_new


def _hg_scan_kernel(*refs):
    ins = refs[:8]
    y_os = refs[8:10]
    st = refs[10]

    @pl.when(pl.program_id(1) == 0)
    def _():
        st[...] = jnp.zeros_like(st)

    def body(ci, carry):
        rows = _scan_rows(ci, HG_SCAN_CHUNKS)
        loaded = []
        for dr in range(2):
            q_r, v_r, kd_r, lf_r = ins[4 * dr:4 * dr + 4]
            rs = rows[dr]
            loaded.append((_load_chunk(q_r, (), rs), _load_chunk(kd_r, (0,), rs), _load_chunk(v_r, (), rs, BF16),
                           _load_chunk(lf_r, (0,), rs), st[dr]))
        results = _chunk_by_groups(_hg_chunk, loaded, HG_HEAD_SPLITS)
        for dr in range(2):
            y_os[dr][0, :, rows[dr]] = results[dr][0].astype(y_os[dr].dtype)
            st[dr] = results[dr][1]
        return carry

    _scan_or_pad(body, HG_SCAN_CHUNKS, y_os)


def _hg_scan(q, v, lf, kd, *, n_lat_chunks, n_ctx_chunks):
    b, ng, tp, _ = q.shape
    grid, in_specs, out_specs, out_shape = _scan_specs(b, ng, tp, n_lat_chunks, n_ctx_chunks, 2, 2, HG_SCAN_CHUNKS)
    return pl.pallas_call(
        _hg_scan_kernel,
        grid=grid,
        in_specs=in_specs,
        out_specs=out_specs,
        out_shape=out_shape,
        scratch_shapes=[pltpu.VMEM((2, ng, LANES, LANES), F32)],
        name="hgrn2_scan",
        compiler_params=pltpu.CompilerParams(
            dimension_semantics=("parallel", "arbitrary"), vmem_limit_bytes=VMEM_LIMIT),
    )(q, v, kd, lf, q, v, kd, lf)


def _pad_rows(w, rows):
    return jnp.pad(w, ((0, rows - w.shape[0]), (0, 0)))


def _pad_cols(w, cols):
    return jnp.pad(w, ((0, 0), (0, cols - w.shape[1])))


def kernel(x, c, ctx, c_ctx, mod_w, mod_b, pre_g, post_g, rw_mix, rw_proj, rw_wo, rw_w0, rw_w1, rw_w2, rw_a0,
           rw_a1, rw_a2, rw_v0, rw_v1, rw_v2, rw_kk, rw_ka, rw_rk, rw_lnw, rw_lnb, hg_win, hg_wo, hg_gn, hg_lb):
    b, seq, d = x.shape
    ctx_len = ctx.shape[1]
    depth = mod_w.shape[0]
    di = rw_proj.shape[-1]
    assert seq % TILE == 0 and ctx_len % CHUNK == 0 and ctx_len <= TILE and seq % GRID_W == 0
    n_lat_tiles = seq // TILE
    tp = seq + TILE
    n_lat_chunks, n_ctx_chunks = seq // CHUNK, ctx_len // CHUNK

    xs = jnp.concatenate([x, ctx, jnp.zeros((b, tp - seq - ctx_len, d), x.dtype)], axis=1)

    rows = ((b + 1 + 7) // 8) * 8
    cond = jnp.concatenate([c, c_ctx[None], jnp.zeros((rows - b - 1, d), c.dtype)], axis=0)
    mod_all = _modulation(cond, mod_w, mod_b)
    mod_lat = jnp.broadcast_to(mod_all[:, :b, None, :], (depth, b, n_lat_tiles, 3 * d))
    mod_ctx = jnp.broadcast_to(mod_all[:, b:b + 1, None, :], (depth, b, 1, 3 * d))
    mod_t = jnp.concatenate([mod_lat, mod_ctx], axis=2)[:, :, :, None, :]

    p_lb = jax.nn.softmax(hg_lb.astype(F32), axis=0)
    lb_all = jnp.cumsum(p_lb, axis=0) - p_lb[0]

    vf = None
    for i in range(depth):
        ctx_out = i < depth - 1
        out_rows = tp if ctx_out else seq
        n_out_tiles = out_rows // OTILE
        j = i // 2
        mod_i = mod_t[i]
        pg = pre_g[i][None]
        if i % 2 == 0:
            r64 = rw_w1.shape[-1]
            w2 = jnp.stack([jnp.pad(rw_w2[j, 0], ((0, LANES - r64), (0, 0))),
                            jnp.pad(rw_w2[j, 1], ((r64, LANES - 2 * r64), (0, 0)))])
            a2 = jnp.stack([jnp.pad(rw_a2[j, 0], ((0, LANES - r64), (0, 0))),
                            jnp.pad(rw_a2[j, 1], ((r64, LANES - 2 * r64), (0, 0)))])
            if j == 0:
                v0 = jnp.zeros((di,), F32)
                v1 = jnp.zeros((d, LANES), BF16)
                v2 = jnp.zeros((LANES, di), BF16)
            else:
                v0 = rw_v0[j - 1]
                v1 = _pad_cols(rw_v1[j - 1], LANES).astype(BF16)
                v2 = _pad_rows(rw_v2[j - 1], LANES).astype(BF16)
            p = {
                "mix": rw_mix[j],
                "proj": rw_proj[j].astype(BF16),
                "w1": jnp.concatenate([rw_w1[j, 0], rw_w1[j, 1]], axis=1).astype(BF16),
                "a1": jnp.concatenate([rw_a1[j, 0], rw_a1[j, 1]], axis=1).astype(BF16),
                "v1": v1,
                "w2": w2.astype(BF16),
                "a2": a2.astype(BF16),
                "v2": v2,
                "vec": jnp.stack([rw_w0[j, 0], rw_w0[j, 1], rw_a0[j, 0], rw_a0[j, 1], v0, rw_kk[j], rw_ka[j],
                                  rw_rk[j]]),
            }
            r, v, kk, z, bv, lw, bb, kd = _rw_features(xs, mod_i, pg, p, vf if j > 0 else None,
                                                       n_lat_tiles=n_lat_tiles, ctx_len=ctx_len)
            if j == 0:
                vf = v
            yf, yb = _rw_scan((r, v, kk, lw, bb, kd), n_lat_chunks=n_lat_chunks, n_ctx_chunks=n_ctx_chunks)
            gspec = _out_group_spec(di)
            extras = [z, bv, rw_lnw[j][None], rw_lnb[j][None], _head_sum_matrix(2 * LANES, RW_HEAD)]
            especs = [gspec, gspec, pl.BlockSpec((1, di), lambda bi, jj: (0, 0)),
                      pl.BlockSpec((1, di), lambda bi, jj: (0, 0)), pl.BlockSpec((2 * LANES, 2 * LANES), lambda bi, jj: (0, 0))]
            xs = _mixer_out("rwkv7_out", _rw_out_kernel, (yf, yb), extras, especs, rw_wo[j].astype(BF16), xs, mod_i,
                            post_g[i][None], n_out_tiles, out_rows)
        else:
            q, iv, sg, lf, kd = _hg_features(xs, mod_i, pg, hg_win[j].astype(BF16), lb_all[i][None])
            yf, yb = _hg_scan(q, iv, lf, kd, n_lat_chunks=n_lat_chunks, n_ctx_chunks=n_ctx_chunks)
            extras = [sg, hg_gn[j][None]]
            especs = [_out_group_spec(di), pl.BlockSpec((1, HG_HEAD), lambda bi, jj: (0, 0))]
            xs = _mixer_out("hgrn2_out", _hg_out_kernel, (yf, yb), extras, especs, hg_wo[j].astype(BF16), xs, mod_i,
                            post_g[i][None], n_out_tiles, out_rows)
    return xs
```

```python
import functools
import math

import jax
import jax.numpy as jnp
from jax import lax
from jax.experimental import pallas as pl
from jax.experimental.pallas import tpu as pltpu

F32 = jnp.float32
BF16 = jnp.bfloat16
HI = lax.Precision.HIGHEST

GRID_W = 64
RW_HEAD = 64
HG_HEAD = 128
CHUNK = 64
TILE = 512
OTILE = 512
HG_TILES_PER_STEP = 3
RW_SCAN_CHUNKS = 4
HG_SCAN_CHUNKS = 4
HG_HEAD_SPLITS = 4
NORM_EPS = 1e-6
LN_X_EPS = 64e-5
LANES = 128
SUBLANES = 8
V7X_VMEM_BYTES = 64 * 1024 * 1024
VMEM_LIMIT = V7X_VMEM_BYTES * 3 // 4


def _silu(x):
    return x * jax.nn.sigmoid(x)


def _dot(a, b):
    return jnp.dot(a, b, preferred_element_type=F32)


def _mod_kernel(s_ref, w_ref, b_ref, o_ref):
    s = _silu(s_ref[...])
    o_ref[0] = jnp.dot(s, w_ref[0], precision=HI, preferred_element_type=F32) + b_ref[0]


def _modulation(cond, mod_w, mod_b):
    depth, d, d3 = mod_w.shape
    rows = cond.shape[0]
    nb = d3 // d
    return pl.pallas_call(
        _mod_kernel,
        grid=(depth, nb),
        in_specs=[
            pl.BlockSpec((rows, d), lambda i, n: (0, 0)),
            pl.BlockSpec((1, d, d), lambda i, n: (i, 0, n)),
            pl.BlockSpec((1, 1, d), lambda i, n: (i, 0, n)),
        ],
        out_specs=pl.BlockSpec((1, rows, d), lambda i, n: (i, 0, n)),
        out_shape=jax.ShapeDtypeStruct((depth, rows, d3), F32),
        name="adaln_modulation",
        compiler_params=pltpu.CompilerParams(vmem_limit_bytes=VMEM_LIMIT),
    )(cond, mod_w, mod_b.reshape(depth, 1, d3))


def _adaln(xb, g, scale, shift):
    ms = jnp.mean(xb * xb, axis=-1, keepdims=True)
    return xb * lax.rsqrt(ms + NORM_EPS) * g * (1.0 + scale) + shift


def _store_groups(o_ref, lead, val):
    for gq in range(val.shape[1] // LANES):
        o_ref[lead + (gq,)] = val[:, gq * LANES:(gq + 1) * LANES].astype(o_ref.dtype)


def _load_groups(ref, lead):
    n = ref.shape[len(lead)]
    return jnp.concatenate([ref[lead + (gq,)] for gq in range(n)], axis=1)


def _head_sum_matrix(width, head):
    r = lax.broadcasted_iota(jnp.int32, (width, width), 0) // head
    c = lax.broadcasted_iota(jnp.int32, (width, width), 1) // head
    return (r == c).astype(BF16)


def _rw_feat_kernel(n_lat_tiles, ctx_len, has_vres, *refs):
    (x_ref, xp_ref, xn_ref, mod_ref, g_ref, mix_ref, w1_ref, a1_ref, v1_ref,
     proj_ref, w2_ref, a2_ref, v2_ref, vec_ref, hs_ref) = refs[:15]
    pos = 15
    vf_ref = None
    if has_vres:
        vf_ref = refs[pos]
        pos += 1
    (r_o, v_o, kk_o, z_o, bv_o, lw_o, bb_o, kd_o) = refs[pos:pos + 8]
    hbuf, xm, tw, ta, tv = refs[pos + 8:]

    j = pl.program_id(1)
    n = pl.program_id(2)
    d = x_ref.shape[-1]
    q4 = d // 4
    nblk = TILE // GRID_W

    @pl.when(n == 0)
    def _prologue():
        m = mod_ref[0, 0]
        shift, scale = m[:, :d], m[:, d:2 * d]
        g = g_ref[...]
        is_lat = j < n_lat_tiles
        prev_ok = jnp.logical_and(j >= 1, is_lat)
        next_ok = j < n_lat_tiles - 1

        def norm_rows(rb, c):
            r0 = pl.multiple_of(rb * GRID_W, GRID_W)
            hbuf[pl.ds(r0 + GRID_W, GRID_W), :] = _adaln(x_ref[0, pl.ds(r0, GRID_W), :], g, scale, shift)
            return c

        lax.fori_loop(0, nblk, norm_rows, 0, unroll=4)
        hp = _adaln(xp_ref[0], g, scale, shift)
        hbuf[0:GRID_W, :] = jnp.where(prev_ok, hp, 0.0)
        hn = _adaln(xn_ref[0], g, scale, shift)
        hbuf[TILE + GRID_W:TILE + 2 * GRID_W, :] = jnp.where(next_ok, hn, 0.0)

        row = lax.broadcasted_iota(jnp.int32, (GRID_W, 1), 0)

        def store_mixed(r0, cs, h_c, hs_c):
            dd = hs_c - h_c
            for k in range(6):
                xm[k, pl.ds(r0, GRID_W), cs] = (h_c + dd * mix_ref[k:k + 1, cs]).astype(BF16)

        def mix_lat(rb, c):
            r0 = pl.multiple_of(rb * GRID_W, GRID_W)
            for qi in range(4):
                cs = slice(qi * q4, (qi + 1) * q4)
                h_c = hbuf[pl.ds(r0 + GRID_W, GRID_W), cs]
                if qi == 0:
                    hs_c = jnp.where(row == 0, 0.0, pltpu.roll(h_c, 1, axis=0))
                elif qi == 1:
                    hs_c = jnp.where(row == GRID_W - 1, 0.0, pltpu.roll(h_c, GRID_W - 1, axis=0))
                elif qi == 2:
                    hs_c = hbuf[pl.ds(r0, GRID_W), cs]
                else:
                    hs_c = hbuf[pl.ds(r0 + 2 * GRID_W, GRID_W), cs]
                store_mixed(r0, cs, h_c, hs_c)
            return c

        def mix_ctx(rb, c):
            r0 = pl.multiple_of(rb * GRID_W, GRID_W)
            t = row + rb * GRID_W
            for qi in range(4):
                cs = slice(qi * q4, (qi + 1) * q4)
                h_c = hbuf[pl.ds(r0 + GRID_W, GRID_W), cs]
                if qi < 2:
                    edge = hbuf[pl.ds(r0 + GRID_W - 1, 1), cs]
                    hs_c = jnp.where(row == 0, edge, pltpu.roll(h_c, 1, axis=0))
                else:
                    edge = hbuf[pl.ds(r0 + 2 * GRID_W, 1), cs]
                    hs_c = jnp.where(row == GRID_W - 1, edge, pltpu.roll(h_c, GRID_W - 1, axis=0))
                    hs_c = jnp.where(t == ctx_len - 1, 0.0, hs_c)
                store_mixed(r0, cs, h_c, hs_c)
            return c

        @pl.when(is_lat)
        def _():
            lax.fori_loop(0, nblk, mix_lat, 0)

        @pl.when(jnp.logical_not(is_lat))
        def _():
            lax.fori_loop(0, nblk, mix_ctx, 0)

        tw[...] = jnp.tanh(_dot(xm[1], w1_ref[...])).astype(BF16)
        ta[...] = _dot(xm[4], a1_ref[...]).astype(BF16)
        if has_vres:
            tv[...] = _dot(xm[3], v1_ref[...]).astype(BF16)

    vec = vec_ref[...]
    r = _dot(xm[0], proj_ref[0, 0])
    k = _dot(xm[2], proj_ref[0, 1])
    v = _dot(xm[3], proj_ref[0, 2])
    z = _silu(_dot(xm[5], proj_ref[0, 3]))
    if has_vres:
        gate = jax.nn.sigmoid(vec[4:5] + _dot(tv[...], v2_ref[...]))
        v = v + (_load_groups(vf_ref, (0,)).astype(F32) - v) * gate
    hs = hs_ref[...]
    kk = k * vec[5:6]
    ss = _dot((kk * kk).astype(BF16), hs)
    kk = kk * lax.rsqrt(jnp.maximum(ss, 1e-24))
    decay_scale = -math.exp(-0.5) * math.log2(math.e)
    ksum = None
    for dr in range(2):
        u = vec[dr:dr + 1] + _dot(tw[...], w2_ref[dr])
        _store_groups(lw_o, (dr, 0), decay_scale * jax.nn.sigmoid(u))
        a = jax.nn.sigmoid(vec[2 + dr:3 + dr] + _dot(ta[...], a2_ref[dr]))
        kd = k * (1.0 + (a - 1.0) * vec[6:7])
        _store_groups(bb_o, (dr, 0), kk * a)
        _store_groups(kd_o, (dr, 0), kd)
        ksum = kd if ksum is None else ksum + kd
    bonus = _dot((r * ksum * vec[7:8]).astype(BF16), hs)
    _store_groups(r_o, (0,), r)
    _store_groups(v_o, (0,), v)
    _store_groups(kk_o, (0,), kk)
    _store_groups(z_o, (0,), z)
    _store_groups(bv_o, (0,), bonus * v)


def _group_major_specs(b, tp, di, tn):
    gq = tn // LANES
    one = pl.BlockSpec((1, gq, TILE, LANES), lambda bi, j, n: (bi, n, j, 0))
    two = pl.BlockSpec((2, 1, gq, TILE, LANES), lambda bi, j, n: (0, bi, n, j, 0))
    s1 = lambda dt: jax.ShapeDtypeStruct((b, di // LANES, tp, LANES), dt)
    s2 = lambda dt: jax.ShapeDtypeStruct((2, b, di // LANES, tp, LANES), dt)
    return one, two, s1, s2


def _rw_features(xs, mod, pre_g, p, vf, *, n_lat_tiles, ctx_len, tn=256):
    b, tp, d = xs.shape
    di = p["proj"].shape[-1]
    nt = di // tn
    ntiles = tp // TILE
    nb64 = tp // GRID_W
    per = TILE // GRID_W
    has_vres = vf is not None

    def tok(bi, j, n):
        return (bi, j, 0)

    in_specs = [
        pl.BlockSpec((1, TILE, d), tok),
        pl.BlockSpec((1, GRID_W, d), lambda bi, j, n: (bi, jnp.maximum(j * per - 1, 0), 0)),
        pl.BlockSpec((1, GRID_W, d), lambda bi, j, n: (bi, jnp.minimum((j + 1) * per, nb64 - 1), 0)),
        pl.BlockSpec((1, 1, 1, mod.shape[-1]), lambda bi, j, n: (bi, j, 0, 0)),
        pl.BlockSpec((1, d), lambda bi, j, n: (0, 0)),
        pl.BlockSpec((6, d), lambda bi, j, n: (0, 0)),
        pl.BlockSpec((d, LANES), lambda bi, j, n: (0, 0)),
        pl.BlockSpec((d, LANES), lambda bi, j, n: (0, 0)),
        pl.BlockSpec((d, LANES), lambda bi, j, n: (0, 0)),
        pl.BlockSpec((1, 4, d, tn), lambda bi, j, n: (0, 0, 0, n)),
        pl.BlockSpec((2, LANES, tn), lambda bi, j, n: (0, 0, n)),
        pl.BlockSpec((2, LANES, tn), lambda bi, j, n: (0, 0, n)),
        pl.BlockSpec((LANES, tn), lambda bi, j, n: (0, n)),
        pl.BlockSpec((8, tn), lambda bi, j, n: (0, n)),
        pl.BlockSpec((tn, tn), lambda bi, j, n: (0, 0)),
    ]
    args = [xs, xs, xs, mod, pre_g, p["mix"], p["w1"], p["a1"], p["v1"], p["proj"][None], p["w2"], p["a2"],
            p["v2"], p["vec"], _head_sum_matrix(tn, RW_HEAD)]
    one, two, s1, s2 = _group_major_specs(b, tp, di, tn)
    if has_vres:
        in_specs.append(one)
        args.append(vf)
    return pl.pallas_call(
        functools.partial(_rw_feat_kernel, n_lat_tiles, ctx_len, has_vres),
        grid=(b, ntiles, nt),
        in_specs=in_specs,
        out_specs=[one, one, one, one, one, two, two, two],
        out_shape=[s1(BF16), s1(BF16), s1(BF16), s1(BF16), s1(BF16), s2(F32), s2(BF16), s2(BF16)],
        scratch_shapes=[
            pltpu.VMEM((TILE + 2 * GRID_W, d), F32),
            pltpu.VMEM((6, TILE, d), BF16),
            pltpu.VMEM((TILE, LANES), BF16),
            pltpu.VMEM((TILE, LANES), BF16),
            pltpu.VMEM((TILE, LANES), BF16),
        ],
        name="rwkv7_features",
        compiler_params=pltpu.CompilerParams(
            dimension_semantics=("parallel", "parallel", "arbitrary"), vmem_limit_bytes=VMEM_LIMIT),
    )(*args)


def _bmm(a, b):
    return lax.dot_general(a, b, (((2,), (1,)), ((0,), (0,))), preferred_element_type=F32)


def _bmm_nt(a, b):
    return lax.dot_general(a, b, (((2,), (2,)), ((0,), (0,))), preferred_element_type=F32)


def _bmm_tn(a, b):
    return lax.dot_general(a, b, (((1,), (1,)), ((0,), (0,))), preferred_element_type=F32)


def _chunk_cumsum(x, rev):
    c = x.shape[1]
    row = lax.broadcasted_iota(jnp.int32, (1, c, 1), 1)
    s = 1
    while s < c:
        if rev:
            x = x + jnp.where(row < c - s, pltpu.roll(x, c - s, axis=1), 0.0)
        else:
            x = x + jnp.where(row >= s, pltpu.roll(x, s, axis=1), 0.0)
        s *= 2
    return x


def _scan_chunk_maps(n_lat_chunks, n_ctx_chunks):
    total = n_lat_chunks + n_ctx_chunks

    def fwd(s):
        return jnp.where(s < total, (s + n_lat_chunks) % total, total)

    def bwd(s):
        return jnp.where(s < total, total - 1 - s, total)

    return total, fwd, bwd


def _scan_specs(b, ng, tp, n_lat_chunks, n_ctx_chunks, n_one, n_two, nchunks):
    assert n_lat_chunks % nchunks == 0 and n_ctx_chunks % nchunks == 0
    rows = CHUNK * nchunks
    total, fwd, bwd = _scan_chunk_maps(n_lat_chunks // nchunks, n_ctx_chunks // nchunks)
    assert tp == (total + 1) * rows
    in_specs = []
    for dr, cm in enumerate((fwd, bwd)):
        one = pl.BlockSpec((1, ng, rows, LANES), lambda bi, s, cm=cm: (bi, 0, cm(s), 0))
        two = pl.BlockSpec((1, 1, ng, rows, LANES), lambda bi, s, cm=cm, dr=dr: (dr, bi, 0, cm(s), 0))
        in_specs += [one] * n_one + [two] * n_two
    out_specs = [pl.BlockSpec((1, ng, rows, LANES), lambda bi, s, cm=cm: (bi, 0, cm(s), 0)) for cm in (fwd, bwd)]
    out_shape = [jax.ShapeDtypeStruct((b, ng, tp, LANES), BF16)] * 2
    return (b, total + 1), in_specs, out_specs, out_shape


def _scan_rows(ci, nchunks):
    fwd = pl.ds(pl.multiple_of(ci * CHUNK, CHUNK), CHUNK)
    bwd = pl.ds(pl.multiple_of((nchunks - 1 - ci) * CHUNK, CHUNK), CHUNK)
    return fwd, bwd


def _scan_or_pad(body, nchunks, y_os):
    s = pl.program_id(1)
    last = pl.num_programs(1) - 1

    @pl.when(s < last)
    def _():
        lax.fori_loop(0, nchunks, body, 0, unroll=2)

    @pl.when(s == last)
    def _():
        for y_o in y_os:
            y_o[...] = jnp.zeros_like(y_o)


def _chunk_by_groups(chunk_fn, loaded, splits):
    n = loaded[0][0].shape[0]
    step = n // splits
    parts = [[], []]
    for part in range(splits):
        gs = slice(part * step, (part + 1) * step)
        for dr in range(2):
            parts[dr].append(chunk_fn(*[a[gs] for a in loaded[dr]], rev=(dr == 1)))
    return [tuple(jnp.concatenate([p[i] for p in parts[dr]], axis=0) for i in range(2)) for dr in range(2)]


def _load_chunk(ref, lead, rs, dtype=F32):
    return ref[lead + (0, slice(None), rs)].astype(dtype)


def _rw_chunk(r, v, kk, bb, kd, lw, st, rev):
    c = CHUNK
    half = RW_HEAD
    npair = r.shape[0]
    ri = lax.broadcasted_iota(jnp.int32, (1, c, LANES), 1)
    ci = lax.broadcasted_iota(jnp.int32, (1, c, LANES), 2) % half
    head_a = lax.broadcasted_iota(jnp.int32, (1, 1, LANES), 2) < half

    def split_rows(x):
        zero = jnp.zeros_like(x)
        return jnp.concatenate([jnp.where(head_a, x, zero), jnp.where(head_a, zero, x)], axis=1)

    if rev:
        incl, strict = ci >= ri, ci > ri
    else:
        incl, strict = ci <= ri, ci < ri
    cum = _chunk_cumsum(lw, rev)
    tot = cum[:, 0:1] if rev else cum[:, c - 1:c]
    e_in = jnp.exp2(cum)
    e_prev = jnp.exp2(cum - lw)
    e_inv = jnp.exp2(-cum)
    e_out = jnp.exp2(tot - cum)
    rt = (r * e_in).astype(BF16)
    at = (-kk * e_prev).astype(BF16)
    bi = (bb * e_inv).astype(BF16)
    ki = (kd * e_inv).astype(BF16)
    bh = (bb * e_out).astype(BF16)
    kh = (kd * e_out).astype(BF16)
    vb = v.astype(BF16)
    sb = st.astype(BF16)
    lhs = jnp.concatenate([at, rt], axis=1)
    h0 = _bmm_nt(lhs, sb)
    m_b = _bmm_nt(lhs, split_rows(bi))
    m_k = _bmm_nt(lhs, split_rows(ki))
    tri = jnp.concatenate([strict, incl], axis=1)
    m_b = m_b.astype(BF16)
    m_b = jnp.where(tri, m_b, jnp.zeros_like(m_b))
    m_k = m_k.astype(BF16)
    m_k = jnp.where(tri, m_k, jnp.zeros_like(m_k))
    ab16, a_rb = m_b[:, :c], m_b[:, c:]
    zero16 = jnp.zeros_like(ab16)
    a1 = jnp.where(ri // 2 == ci // 2, ab16, zero16)
    pmb = jnp.where(ri == ci, jnp.ones_like(ab16), a1)
    off = jnp.logical_and(ri // 4 == ci // 4, ri // 2 != ci // 2)
    x2 = jnp.where(off, ab16, zero16).astype(F32)
    a1f = a1.astype(F32)
    a1_row = jnp.where(head_a, jnp.sum(jnp.where(head_a, a1f, 0.0), axis=2, keepdims=True),
                       jnp.sum(jnp.where(head_a, 0.0, a1f), axis=2, keepdims=True))
    a1_col = jnp.sum(a1f, axis=1, keepdims=True)
    y2 = x2 + a1_row * pltpu.roll(x2, c - 1 if rev else 1, axis=1)
    d4 = y2 + pltpu.roll(y2, 1 if rev else LANES - 1, axis=2) * a1_col
    pmb = jnp.where(off, d4.astype(BF16), pmb)
    m = 4
    while m < c:
        off = jnp.logical_and(ri // (2 * m) == ci // (2 * m), ri // m != ci // m)
        a_off = jnp.where(off, ab16, zero16)
        delta = _bmm(_bmm(pmb, split_rows(a_off)).astype(BF16), split_rows(pmb))
        pmb = jnp.where(off, delta.astype(BF16), pmb)
        m *= 2
    av = _bmm(m_k, split_rows(vb))
    x = h0[:, :c] + av[:, :c]
    u = _bmm(pmb, split_rows(x.astype(BF16)))
    ub = u.astype(BF16)
    o_full = h0[:, c:] + av[:, c:] + _bmm(a_rb, split_rows(ub))
    upd = _bmm_tn(jnp.concatenate([ub, vb], axis=1), jnp.concatenate([bh, kh], axis=1))
    blk_r = lax.broadcasted_iota(jnp.int32, (1, LANES, LANES), 1) // half
    blk_c = lax.broadcasted_iota(jnp.int32, (1, LANES, LANES), 2) // half
    st_new = st * jnp.exp2(tot) + jnp.where(blk_r == blk_c, upd, 0.0)
    return o_full, st_new


def _rw_scan_kernel(*refs):
    ins = refs[:12]
    y_os = refs[12:14]
    st = refs[14]

    @pl.when(pl.program_id(1) == 0)
    def _():
        st[...] = jnp.zeros_like(st)

    def body(ci, carry):
        rows = _scan_rows(ci, RW_SCAN_CHUNKS)
        loaded = []
        for dr in range(2):
            r_r, v_r, kk_r, bb_r, kd_r, lw_r = ins[6 * dr:6 * dr + 6]
            rs = rows[dr]
            loaded.append((_load_chunk(r_r, (), rs), _load_chunk(v_r, (), rs), _load_chunk(kk_r, (), rs),
                           _load_chunk(bb_r, (0,), rs), _load_chunk(kd_r, (0,), rs), _load_chunk(lw_r, (0,), rs),
                           st[dr]))
        results = _chunk_by_groups(_rw_chunk, loaded, 1)
        for dr in range(2):
            y_os[dr][0, :, rows[dr]] = results[dr][0].astype(y_os[dr].dtype)
            st[dr] = results[dr][1]
        return carry

    _scan_or_pad(body, RW_SCAN_CHUNKS, y_os)


def _rw_scan(feats, *, n_lat_chunks, n_ctx_chunks):
    r, v, kk, lw, bb, kd = feats
    b, ng, tp, _ = r.shape
    grid, in_specs, out_specs, out_shape = _scan_specs(b, ng, tp, n_lat_chunks, n_ctx_chunks, 3, 3, RW_SCAN_CHUNKS)
    return pl.pallas_call(
        _rw_scan_kernel,
        grid=grid,
        in_specs=in_specs,
        out_specs=out_specs,
        out_shape=out_shape,
        scratch_shapes=[pltpu.VMEM((2, ng, LANES, LANES), F32)],
        name="rwkv7_scan",
        compiler_params=pltpu.CompilerParams(
            dimension_semantics=("parallel", "arbitrary"), vmem_limit_bytes=VMEM_LIMIT),
    )(r, v, kk, bb, kd, lw, r, v, kk, bb, kd, lw)


def _residual_tail(yg_bf16, wo_ref, x_ref, mod_ref, pg_ref, o_ref):
    d = x_ref.shape[-1]
    o = _dot(yg_bf16, wo_ref[...])
    ms = jnp.mean(o * o, axis=-1, keepdims=True)
    o = o * lax.rsqrt(ms + NORM_EPS) * pg_ref[...]
    gate = mod_ref[0, 0][:, 2 * d:]
    o_ref[0] = x_ref[0] + gate * o


def _rw_out_kernel(yf_ref, yb_ref, z_ref, bv_ref, lnw_ref, lnb_ref, hs_ref, wo_ref, x_ref, mod_ref, pg_ref,
                   o_ref, yg):
    inv = 1.0 / RW_HEAD
    hs = hs_ref[...]
    per = hs.shape[0] // LANES

    def groups(ref, g0):
        return jnp.concatenate([ref[0, g0 + i] for i in range(per)], axis=1)

    for g0 in range(0, yf_ref.shape[1], per):
        cs = slice(g0 * LANES, (g0 + per) * LANES)
        y = groups(yf_ref, g0).astype(F32) + groups(yb_ref, g0).astype(F32)
        mu = _dot(y.astype(BF16), hs) * inv
        yc = y - mu
        var = _dot((yc * yc).astype(BF16), hs) * inv
        yn = yc * lax.rsqrt(var + LN_X_EPS) * lnw_ref[:, cs] + lnb_ref[:, cs] + groups(bv_ref, g0).astype(F32)
        yg[:, cs] = yn.astype(BF16) * groups(z_ref, g0)
    _residual_tail(yg[...], wo_ref, x_ref, mod_ref, pg_ref, o_ref)


def _hg_out_kernel(yf_ref, yb_ref, sg_ref, gn_ref, wo_ref, x_ref, mod_ref, pg_ref, o_ref, yg):
    for hx in range(yf_ref.shape[1]):
        cs = slice(hx * LANES, (hx + 1) * LANES)
        y = yf_ref[0, hx].astype(F32) + yb_ref[0, hx].astype(F32)
        ms = jnp.mean(y * y, axis=-1, keepdims=True)
        yn = y * lax.rsqrt(ms + NORM_EPS) * gn_ref[...]
        yg[:, cs] = yn.astype(BF16) * sg_ref[0, hx]
    _residual_tail(yg[...], wo_ref, x_ref, mod_ref, pg_ref, o_ref)


def _out_group_spec(di):
    return pl.BlockSpec((1, di // LANES, OTILE, LANES), lambda bi, j: (bi, 0, j, 0))


def _mixer_out(name, kernel_fn, ys, extras, extra_specs, wo, xs, mod, post_g, n_tiles, out_rows):
    b, tp, d = xs.shape
    in_place = out_rows == tp
    assert in_place or out_rows == n_tiles * OTILE
    di = wo.shape[0]
    per = TILE // OTILE
    tokd = pl.BlockSpec((1, OTILE, d), lambda bi, j: (bi, j, 0))
    in_specs = [_out_group_spec(di)] * 2 + extra_specs + [
        pl.BlockSpec((di, d), lambda bi, j: (0, 0)),
        tokd,
        pl.BlockSpec((1, 1, 1, mod.shape[-1]), lambda bi, j: (bi, j // per, 0, 0)),
        pl.BlockSpec((1, d), lambda bi, j: (0, 0)),
    ]
    n_in = len(in_specs)
    return pl.pallas_call(
        kernel_fn,
        grid=(b, n_tiles),
        in_specs=in_specs,
        out_specs=tokd,
        out_shape=jax.ShapeDtypeStruct((b, out_rows, d), F32),
        scratch_shapes=[pltpu.VMEM((OTILE, di), BF16)],
        input_output_aliases={n_in - 3: 0} if in_place else {},
        name=name,
        compiler_params=pltpu.CompilerParams(
            dimension_semantics=("parallel", "parallel"), vmem_limit_bytes=VMEM_LIMIT),
    )(*ys, *extras, wo, xs, mod, post_g)


def _hg_feat_kernel(x_ref, mod_ref, g_ref, wq_ref, wf0_ref, wf1_ref, wi_ref, wg_ref, lb_ref,
                    q_o, i_o, sg_o, lf_o, kd_o, hbf):
    n = pl.program_id(1)
    j = pl.program_id(2)
    d = x_ref.shape[-1]
    rows = x_ref.shape[1]
    base = pl.multiple_of(j * rows, rows)

    @pl.when(n == 0)
    def _prologue():
        g = g_ref[...]
        for t in range(rows // TILE):
            m = mod_ref[0, t]
            shift, scale = m[:, :d], m[:, d:2 * d]

            def norm_rows(rb, c, t=t, shift=shift, scale=scale):
                r0 = pl.multiple_of(t * TILE + rb * GRID_W, GRID_W)
                hbf[pl.ds(base + r0, GRID_W), :] = _adaln(x_ref[0, pl.ds(r0, GRID_W), :], g, scale,
                                                          shift).astype(BF16)
                return c

            lax.fori_loop(0, TILE // GRID_W, norm_rows, 0, unroll=4)

    h = hbf[pl.ds(base, rows), :]
    lb = lb_ref[...]
    _store_groups(q_o, (0,), _silu(_dot(h, wq_ref[...])))
    for dr, wf_ref in enumerate((wf0_ref, wf1_ref)):
        f = lb + (1.0 - lb) * jax.nn.sigmoid(_dot(h, wf_ref[...]))
        _store_groups(kd_o, (dr, 0), 1.0 - f)
        _store_groups(lf_o, (dr, 0), jnp.log2(f))
    _store_groups(i_o, (0,), _dot(h, wi_ref[...]))
    _store_groups(sg_o, (0,), _silu(_dot(h, wg_ref[...])))


def _hg_features(xs, mod, pre_g, w_in, lb, *, tn=256):
    b, tp, d = xs.shape
    di = w_in.shape[-1] // 5
    nt = di // tn
    per = HG_TILES_PER_STEP if (tp // TILE) % HG_TILES_PER_STEP == 0 else 1
    rows = per * TILE
    nsteps = tp // rows
    _, _, s1, s2 = _group_major_specs(b, tp, di, tn)
    gq = tn // LANES
    one = pl.BlockSpec((1, gq, rows, LANES), lambda bi, n, j: (bi, n, j, 0))
    two = pl.BlockSpec((2, 1, gq, rows, LANES), lambda bi, n, j: (0, bi, n, j, 0))

    def tok_tile(n, j):
        return jnp.where(n == 0, j, nsteps - 1)

    return pl.pallas_call(
        _hg_feat_kernel,
        grid=(b, nt, nsteps),
        in_specs=[
            pl.BlockSpec((1, rows, d), lambda bi, n, j: (bi, tok_tile(n, j), 0)),
            pl.BlockSpec((1, per, 1, mod.shape[-1]), lambda bi, n, j: (bi, tok_tile(n, j), 0, 0)),
            pl.BlockSpec((1, d), lambda bi, n, j: (0, 0)),
            *[pl.BlockSpec((d, tn), lambda bi, n, j, part=part: (0, part * nt + n)) for part in range(5)],
            pl.BlockSpec((1, tn), lambda bi, n, j: (0, n)),
        ],
        out_specs=[one, one, one, two, two],
        out_shape=[s1(BF16), s1(BF16), s1(BF16), s2(F32), s2(BF16)],
        scratch_shapes=[pltpu.VMEM((tp, d), BF16)],
        name="hgrn2_features",
        compiler_params=pltpu.CompilerParams(
            dimension_semantics=("parallel", "arbitrary", "arbitrary"), vmem_limit_bytes=VMEM_LIMIT),
    )(xs, mod, pre_g, w_in, w_in, w_in, w_in, w_in, lb)


def _hg_chunk(q, k, v, lf2, st, rev):
    c = CHUNK
    sub = SUBLANES
    nh = q.shape[0]
    row = lax.broadcasted_iota(jnp.int32, (1, c, 1), 1)
    ri = lax.broadcasted_iota(jnp.int32, (1, c, c), 1)
    ci = lax.broadcasted_iota(jnp.int32, (1, c, c), 2)
    diag = jnp.sum(q * k, axis=2, keepdims=True)
    att = jnp.where(ri == ci, diag, jnp.zeros((nh, c, c), F32))
    shift = c - 1 if rev else 1
    later1 = (row % 2) == (0 if rev else 1)
    near = jnp.sum(q * jnp.exp2(lf2) * pltpu.roll(k, shift, axis=1), axis=2, keepdims=True)
    att = jnp.where(jnp.logical_and(ci == (ri + 1 if rev else ri - 1), later1), near, att)
    p = lf2 + jnp.where(later1, pltpu.roll(lf2, shift, axis=1), 0.0)
    m = 2
    while m < c:
        pair = jnp.logical_and(ri // (2 * m) == ci // (2 * m), ri // m != ci // m)
        if m >= sub:
            zeros = jnp.zeros((nh, m, LANES), BF16)
            q_parts, k_parts, p_parts = [], [], []
            for blk in range(c // (2 * m)):
                lo = slice(blk * 2 * m, blk * 2 * m + m)
                hi = slice(blk * 2 * m + m, (blk + 1) * 2 * m)
                early, later = (hi, lo) if rev else (lo, hi)
                bidx = blk * 2 * m + (m if rev else m - 1)
                tt = p[:, bidx:bidx + 1, :]
                q_l = (q[:, later] * jnp.exp2(p[:, later])).astype(BF16)
                k_e = (k[:, early] * jnp.exp2(tt - p[:, early])).astype(BF16)
                if rev:
                    q_parts += [q_l, zeros]
                    k_parts += [zeros, k_e]
                    p_parts += [p[:, later] + tt, p[:, early]]
                else:
                    q_parts += [zeros, q_l]
                    k_parts += [k_e, zeros]
                    p_parts += [p[:, early], p[:, later] + tt]
            mm = _bmm_nt(jnp.concatenate(q_parts, axis=1), jnp.concatenate(k_parts, axis=1))
            p = jnp.concatenate(p_parts, axis=1)
        else:
            in_hi = ((row // m) % 2) == 1
            late = jnp.logical_not(in_hi) if rev else in_hi
            p4 = p.reshape(nh, c // sub, sub, LANES)
            srow = lax.broadcasted_iota(jnp.int32, (1, 1, sub, 1), 2)
            tt4 = None
            for blk in range(sub // (2 * m)):
                bidx = blk * 2 * m + (m if rev else m - 1)
                cand = jnp.broadcast_to(p4[:, :, bidx:bidx + 1, :], p4.shape)
                tt4 = cand if tt4 is None else jnp.where(srow // (2 * m) == blk, cand, tt4)
            tt = tt4.reshape(nh, c, LANES)
            ee = jnp.exp2(jnp.where(late, p, tt - p))
            qt = jnp.where(late, q * ee, 0.0).astype(BF16)
            kt = jnp.where(late, 0.0, k * ee).astype(BF16)
            mm = _bmm_nt(qt, kt)
            p = jnp.where(late, p + tt, p)
        att = jnp.where(pair, mm, att)
        m *= 2
    tot = p[:, 0:1] if rev else p[:, c - 1:c]
    vb = v.astype(BF16)
    y = _bmm(att.astype(BF16), vb) + _bmm_nt((q * jnp.exp2(p)).astype(BF16), st.astype(BF16))
    st_new = st * jnp.exp2(tot) + _bmm_tn(vb, (k * jnp.exp2(tot - p)).astype(BF16))
    return y, st_new


def _hg_scan_kernel(*refs):
    ins = refs[:8]
    y_os = refs[8:10]
    st = refs[10]

    @pl.when(pl.program_id(1) == 0)
    def _():
        st[...] = jnp.zeros_like(st)

    def body(ci, carry):
        rows = _scan_rows(ci, HG_SCAN_CHUNKS)
        loaded = []
        for dr in range(2):
            q_r, v_r, kd_r, lf_r = ins[4 * dr:4 * dr + 4]
            rs = rows[dr]
            loaded.append((_load_chunk(q_r, (), rs), _load_chunk(kd_r, (0,), rs), _load_chunk(v_r, (), rs, BF16),
                           _load_chunk(lf_r, (0,), rs), st[dr]))
        results = _chunk_by_groups(_hg_chunk, loaded, HG_HEAD_SPLITS)
        for dr in range(2):
            y_os[dr][0, :, rows[dr]] = results[dr][0].astype(y_os[dr].dtype)
            st[dr] = results[dr][1]
        return carry

    _scan_or_pad(body, HG_SCAN_CHUNKS, y_os)


def _hg_scan(q, v, lf, kd, *, n_lat_chunks, n_ctx_chunks):
    b, ng, tp, _ = q.shape
    grid, in_specs, out_specs, out_shape = _scan_specs(b, ng, tp, n_lat_chunks, n_ctx_chunks, 2, 2, HG_SCAN_CHUNKS)
    return pl.pallas_call(
        _hg_scan_kernel,
        grid=grid,
        in_specs=in_specs,
        out_specs=out_specs,
        out_shape=out_shape,
        scratch_shapes=[pltpu.VMEM((2, ng, LANES, LANES), F32)],
        name="hgrn2_scan",
        compiler_params=pltpu.CompilerParams(
            dimension_semantics=("parallel", "arbitrary"), vmem_limit_bytes=VMEM_LIMIT),
    )(q, v, kd, lf, q, v, kd, lf)


def _pad_rows(w, rows):
    return jnp.pad(w, ((0, rows - w.shape[0]), (0, 0)))


def _pad_cols(w, cols):
    return jnp.pad(w, ((0, 0), (0, cols - w.shape[1])))


def kernel(x, c, ctx, c_ctx, mod_w, mod_b, pre_g, post_g, rw_mix, rw_proj, rw_wo, rw_w0, rw_w1, rw_w2, rw_a0,
           rw_a1, rw_a2, rw_v0, rw_v1, rw_v2, rw_kk, rw_ka, rw_rk, rw_lnw, rw_lnb, hg_win, hg_wo, hg_gn, hg_lb):
    b, seq, d = x.shape
    ctx_len = ctx.shape[1]
    depth = mod_w.shape[0]
    di = rw_proj.shape[-1]
    assert seq % TILE == 0 and ctx_len % CHUNK == 0 and ctx_len <= TILE and seq % GRID_W == 0
    n_lat_tiles = seq // TILE
    tp = seq + TILE
    n_lat_chunks, n_ctx_chunks = seq // CHUNK, ctx_len // CHUNK

    xs = jnp.concatenate([x, ctx, jnp.zeros((b, tp - seq - ctx_len, d), x.dtype)], axis=1)

    rows = ((b + 1 + 7) // 8) * 8
    cond = jnp.concatenate([c, c_ctx[None], jnp.zeros((rows - b - 1, d), c.dtype)], axis=0)
    mod_all = _modulation(cond, mod_w, mod_b)
    mod_lat = jnp.broadcast_to(mod_all[:, :b, None, :], (depth, b, n_lat_tiles, 3 * d))
    mod_ctx = jnp.broadcast_to(mod_all[:, b:b + 1, None, :], (depth, b, 1, 3 * d))
    mod_t = jnp.concatenate([mod_lat, mod_ctx], axis=2)[:, :, :, None, :]

    p_lb = jax.nn.softmax(hg_lb.astype(F32), axis=0)
    lb_all = jnp.cumsum(p_lb, axis=0) - p_lb[0]

    vf = None
    for i in range(depth):
        ctx_out = i < depth - 1
        out_rows = tp if ctx_out else seq
        n_out_tiles = out_rows // OTILE
        j = i // 2
        mod_i = mod_t[i]
        pg = pre_g[i][None]
        if i % 2 == 0:
            r64 = rw_w1.shape[-1]
            w2 = jnp.stack([jnp.pad(rw_w2[j, 0], ((0, LANES - r64), (0, 0))),
                            jnp.pad(rw_w2[j, 1], ((r64, LANES - 2 * r64), (0, 0)))])
            a2 = jnp.stack([jnp.pad(rw_a2[j, 0], ((0, LANES - r64), (0, 0))),
                            jnp.pad(rw_a2[j, 1], ((r64, LANES - 2 * r64), (0, 0)))])
            if j == 0:
                v0 = jnp.zeros((di,), F32)
                v1 = jnp.zeros((d, LANES), BF16)
                v2 = jnp.zeros((LANES, di), BF16)
            else:
                v0 = rw_v0[j - 1]
                v1 = _pad_cols(rw_v1[j - 1], LANES).astype(BF16)
                v2 = _pad_rows(rw_v2[j - 1], LANES).astype(BF16)
            p = {
                "mix": rw_mix[j],
                "proj": rw_proj[j].astype(BF16),
                "w1": jnp.concatenate([rw_w1[j, 0], rw_w1[j, 1]], axis=1).astype(BF16),
                "a1": jnp.concatenate([rw_a1[j, 0], rw_a1[j, 1]], axis=1).astype(BF16),
                "v1": v1,
                "w2": w2.astype(BF16),
                "a2": a2.astype(BF16),
                "v2": v2,
                "vec": jnp.stack([rw_w0[j, 0], rw_w0[j, 1], rw_a0[j, 0], rw_a0[j, 1], v0, rw_kk[j], rw_ka[j],
                                  rw_rk[j]]),
            }
            r, v, kk, z, bv, lw, bb, kd = _rw_features(xs, mod_i, pg, p, vf if j > 0 else None,
                                                       n_lat_tiles=n_lat_tiles, ctx_len=ctx_len)
            if j == 0:
                vf = v
            yf, yb = _rw_scan((r, v, kk, lw, bb, kd), n_lat_chunks=n_lat_chunks, n_ctx_chunks=n_ctx_chunks)
            gspec = _out_group_spec(di)
            extras = [z, bv, rw_lnw[j][None], rw_lnb[j][None], _head_sum_matrix(2 * LANES, RW_HEAD)]
            especs = [gspec, gspec, pl.BlockSpec((1, di), lambda bi, jj: (0, 0)),
                      pl.BlockSpec((1, di), lambda bi, jj: (0, 0)), pl.BlockSpec((2 * LANES, 2 * LANES), lambda bi, jj: (0, 0))]
            xs = _mixer_out("rwkv7_out", _rw_out_kernel, (yf, yb), extras, especs, rw_wo[j].astype(BF16), xs, mod_i,
                            post_g[i][None], n_out_tiles, out_rows)
        else:
            q, iv, sg, lf, kd = _hg_features(xs, mod_i, pg, hg_win[j].astype(BF16), lb_all[i][None])
            yf, yb = _hg_scan(q, iv, lf, kd, n_lat_chunks=n_lat_chunks, n_ctx_chunks=n_ctx_chunks)
            extras = [sg, hg_gn[j][None]]
            especs = [_out_group_spec(di), pl.BlockSpec((1, HG_HEAD), lambda bi, jj: (0, 0))]
            xs = _mixer_out("hgrn2_out", _hg_out_kernel, (yf, yb), extras, especs, hg_wo[j].astype(BF16), xs, mod_i,
                            post_g[i][None], n_out_tiles, out_rows)
    return xs
```

```python
import functools
import math

import jax
import jax.numpy as jnp
from jax import lax
from jax.experimental import pallas as pl
from jax.experimental.pallas import tpu as pltpu

F32 = jnp.float32
BF16 = jnp.bfloat16
HI = lax.Precision.HIGHEST

GRID_W = 64
RW_HEAD = 64
HG_HEAD = 128
CHUNK = 64
TILE = 512
OTILE = 512
HG_TILES_PER_STEP = 3
RW_SCAN_CHUNKS = 4
HG_SCAN_CHUNKS = 4
HG_HEAD_SPLITS = 4
NORM_EPS = 1e-6
LN_X_EPS = 64e-5
LANES = 128
SUBLANES = 8
V7X_VMEM_BYTES = 64 * 1024 * 1024
VMEM_LIMIT = V7X_VMEM_BYTES * 3 // 4


def _silu(x):
    return x * jax.nn.sigmoid(x)


def _dot(a, b):
    return jnp.dot(a, b, preferred_element_type=F32)


def _mod_kernel(s_ref, w_ref, b_ref, o_ref):
    s = _silu(s_ref[...])
    o_ref[0] = jnp.dot(s, w_ref[0], precision=HI, preferred_element_type=F32) + b_ref[0]


def _modulation(cond, mod_w, mod_b):
    depth, d, d3 = mod_w.shape
    rows = cond.shape[0]
    nb = d3 // d
    return pl.pallas_call(
        _mod_kernel,
        grid=(depth, nb),
        in_specs=[
            pl.BlockSpec((rows, d), lambda i, n: (0, 0)),
            pl.BlockSpec((1, d, d), lambda i, n: (i, 0, n)),
            pl.BlockSpec((1, 1, d), lambda i, n: (i, 0, n)),
        ],
        out_specs=pl.BlockSpec((1, rows, d), lambda i, n: (i, 0, n)),
        out_shape=jax.ShapeDtypeStruct((depth, rows, d3), F32),
        name="adaln_modulation",
        compiler_params=pltpu.CompilerParams(vmem_limit_bytes=VMEM_LIMIT),
    )(cond, mod_w, mod_b.reshape(depth, 1, d3))


def _adaln(xb, g, scale, shift):
    ms = jnp.mean(xb * xb, axis=-1, keepdims=True)
    return xb * lax.rsqrt(ms + NORM_EPS) * g * (1.0 + scale) + shift


def _store_groups(o_ref, lead, val):
    for gq in range(val.shape[1] // LANES):
        o_ref[lead + (gq,)] = val[:, gq * LANES:(gq + 1) * LANES].astype(o_ref.dtype)


def _load_groups(ref, lead):
    n = ref.shape[len(lead)]
    return jnp.concatenate([ref[lead + (gq,)] for gq in range(n)], axis=1)


def _head_sum_matrix(width, head):
    r = lax.broadcasted_iota(jnp.int32, (width, width), 0) // head
    c = lax.broadcasted_iota(jnp.int32, (width, width), 1) // head
    return (r == c).astype(BF16)


def _rw_feat_kernel(n_lat_tiles, ctx_len, has_vres, *refs):
    (x_ref, xp_ref, xn_ref, mod_ref, g_ref, mix_ref, w1_ref, a1_ref, v1_ref,
     proj_ref, w2_ref, a2_ref, v2_ref, vec_ref, hs_ref) = refs[:15]
    pos = 15
    vf_ref = None
    if has_vres:
        vf_ref = refs[pos]
        pos += 1
    (r_o, v_o, kk_o, z_o, bv_o, lw_o, bb_o, kd_o) = refs[pos:pos + 8]
    hbuf, xm, tw, ta, tv = refs[pos + 8:]

    j = pl.program_id(1)
    n = pl.program_id(2)
    d = x_ref.shape[-1]
    q4 = d // 4
    nblk = TILE // GRID_W

    @pl.when(n == 0)
    def _prologue():
        m = mod_ref[0, 0]
        shift, scale = m[:, :d], m[:, d:2 * d]
        g = g_ref[...]
        is_lat = j < n_lat_tiles
        prev_ok = jnp.logical_and(j >= 1, is_lat)
        next_ok = j < n_lat_tiles - 1

        def norm_rows(rb, c):
            r0 = pl.multiple_of(rb * GRID_W, GRID_W)
            hbuf[pl.ds(r0 + GRID_W, GRID_W), :] = _adaln(x_ref[0, pl.ds(r0, GRID_W), :], g, scale, shift)
            return c

        lax.fori_loop(0, nblk, norm_rows, 0, unroll=4)
        hp = _adaln(xp_ref[0], g, scale, shift)
        hbuf[0:GRID_W, :] = jnp.where(prev_ok, hp, 0.0)
        hn = _adaln(xn_ref[0], g, scale, shift)
        hbuf[TILE + GRID_W:TILE + 2 * GRID_W, :] = jnp.where(next_ok, hn, 0.0)

        row = lax.broadcasted_iota(jnp.int32, (GRID_W, 1), 0)

        def store_mixed(r0, cs, h_c, hs_c):
            dd = hs_c - h_c
            for k in range(6):
                xm[k, pl.ds(r0, GRID_W), cs] = (h_c + dd * mix_ref[k:k + 1, cs]).astype(BF16)

        def mix_lat(rb, c):
            r0 = pl.multiple_of(rb * GRID_W, GRID_W)
            for qi in range(4):
                cs = slice(qi * q4, (qi + 1) * q4)
                h_c = hbuf[pl.ds(r0 + GRID_W, GRID_W), cs]
                if qi == 0:
                    hs_c = jnp.where(row == 0, 0.0, pltpu.roll(h_c, 1, axis=0))
                elif qi == 1:
                    hs_c = jnp.where(row == GRID_W - 1, 0.0, pltpu.roll(h_c, GRID_W - 1, axis=0))
                elif qi == 2:
                    hs_c = hbuf[pl.ds(r0, GRID_W), cs]
                else:
                    hs_c = hbuf[pl.ds(r0 + 2 * GRID_W, GRID_W), cs]
                store_mixed(r0, cs, h_c, hs_c)
            return c

        def mix_ctx(rb, c):
            r0 = pl.multiple_of(rb * GRID_W, GRID_W)
            t = row + rb * GRID_W
            for qi in range(4):
                cs = slice(qi * q4, (qi + 1) * q4)
                h_c = hbuf[pl.ds(r0 + GRID_W, GRID_W), cs]
                if qi < 2:
                    edge = hbuf[pl.ds(r0 + GRID_W - 1, 1), cs]
                    hs_c = jnp.where(row == 0, edge, pltpu.roll(h_c, 1, axis=0))
                else:
                    edge = hbuf[pl.ds(r0 + 2 * GRID_W, 1), cs]
                    hs_c = jnp.where(row == GRID_W - 1, edge, pltpu.roll(h_c, GRID_W - 1, axis=0))
                    hs_c = jnp.where(t == ctx_len - 1, 0.0, hs_c)
                store_mixed(r0, cs, h_c, hs_c)
            return c

        @pl.when(is_lat)
        def _():
            lax.fori_loop(0, nblk, mix_lat, 0)

        @pl.when(jnp.logical_not(is_lat))
        def _():
            lax.fori_loop(0, nblk, mix_ctx, 0)

        tw[...] = jnp.tanh(_dot(xm[1], w1_ref[...])).astype(BF16)
        ta[...] = _dot(xm[4], a1_ref[...]).astype(BF16)
        if has_vres:
            tv[...] = _dot(xm[3], v1_ref[...]).astype(BF16)

    vec = vec_ref[...]
    r = _dot(xm[0], proj_ref[0, 0])
    k = _dot(xm[2], proj_ref[0, 1])
    v = _dot(xm[3], proj_ref[0, 2])
    z = _silu(_dot(xm[5], proj_ref[0, 3]))
    if has_vres:
        gate = jax.nn.sigmoid(vec[4:5] + _dot(tv[...], v2_ref[...]))
        v = v + (_load_groups(vf_ref, (0,)).astype(F32) - v) * gate
    hs = hs_ref[...]
    kk = k * vec[5:6]
    ss = _dot((kk * kk).astype(BF16), hs)
    kk = kk * lax.rsqrt(jnp.maximum(ss, 1e-24))
    decay_scale = -math.exp(-0.5) * math.log2(math.e)
    ksum = None
    for dr in range(2):
        u = vec[dr:dr + 1] + _dot(tw[...], w2_ref[dr])
        _store_groups(lw_o, (dr, 0), decay_scale * jax.nn.sigmoid(u))
        a = jax.nn.sigmoid(vec[2 + dr:3 + dr] + _dot(ta[...], a2_ref[dr]))
        kd = k * (1.0 + (a - 1.0) * vec[6:7])
        _store_groups(bb_o, (dr, 0), kk * a)
        _store_groups(kd_o, (dr, 0), kd)
        ksum = kd if ksum is None else ksum + kd
    bonus = _dot((r * ksum * vec[7:8]).astype(BF16), hs)
    _store_groups(r_o, (0,), r)
    _store_groups(v_o, (0,), v)
    _store_groups(kk_o, (0,), kk)
    _store_groups(z_o, (0,), z)
    _store_groups(bv_o, (0,), bonus * v)


def _group_major_specs(b, tp, di, tn):
    gq = tn // LANES
    one = pl.BlockSpec((1, gq, TILE, LANES), lambda bi, j, n: (bi, n, j, 0))
    two = pl.BlockSpec((2, 1, gq, TILE, LANES), lambda bi, j, n: (0, bi, n, j, 0))
    s1 = lambda dt: jax.ShapeDtypeStruct((b, di // LANES, tp, LANES), dt)
    s2 = lambda dt: jax.ShapeDtypeStruct((2, b, di // LANES, tp, LANES), dt)
    return one, two, s1, s2


def _rw_features(xs, mod, pre_g, p, vf, *, n_lat_tiles, ctx_len, tn=256):
    b, tp, d = xs.shape
    di = p["proj"].shape[-1]
    nt = di // tn
    ntiles = tp // TILE
    nb64 = tp // GRID_W
    per = TILE // GRID_W
    has_vres = vf is not None

    def tok(bi, j, n):
        return (bi, j, 0)

    in_specs = [
        pl.BlockSpec((1, TILE, d), tok),
        pl.BlockSpec((1, GRID_W, d), lambda bi, j, n: (bi, jnp.maximum(j * per - 1, 0), 0)),
        pl.BlockSpec((1, GRID_W, d), lambda bi, j, n: (bi, jnp.minimum((j + 1) * per, nb64 - 1), 0)),
        pl.BlockSpec((1, 1, 1, mod.shape[-1]), lambda bi, j, n: (bi, j, 0, 0)),
        pl.BlockSpec((1, d), lambda bi, j, n: (0, 0)),
        pl.BlockSpec((6, d), lambda bi, j, n: (0, 0)),
        pl.BlockSpec((d, LANES), lambda bi, j, n: (0, 0)),
        pl.BlockSpec((d, LANES), lambda bi, j, n: (0, 0)),
        pl.BlockSpec((d, LANES), lambda bi, j, n: (0, 0)),
        pl.BlockSpec((1, 4, d, tn), lambda bi, j, n: (0, 0, 0, n)),
        pl.BlockSpec((2, LANES, tn), lambda bi, j, n: (0, 0, n)),
        pl.BlockSpec((2, LANES, tn), lambda bi, j, n: (0, 0, n)),
        pl.BlockSpec((LANES, tn), lambda bi, j, n: (0, n)),
        pl.BlockSpec((8, tn), lambda bi, j, n: (0, n)),
        pl.BlockSpec((tn, tn), lambda bi, j, n: (0, 0)),
    ]
    args = [xs, xs, xs, mod, pre_g, p["mix"], p["w1"], p["a1"], p["v1"], p["proj"][None], p["w2"], p["a2"],
            p["v2"], p["vec"], _head_sum_matrix(tn, RW_HEAD)]
    one, two, s1, s2 = _group_major_specs(b, tp, di, tn)
    if has_vres:
        in_specs.append(one)
        args.append(vf)
    return pl.pallas_call(
        functools.partial(_rw_feat_kernel, n_lat_tiles, ctx_len, has_vres),
        grid=(b, ntiles, nt),
        in_specs=in_specs,
        out_specs=[one, one, one, one, one, two, two, two],
        out_shape=[s1(BF16), s1(BF16), s1(BF16), s1(BF16), s1(BF16), s2(F32), s2(BF16), s2(BF16)],
        scratch_shapes=[
            pltpu.VMEM((TILE + 2 * GRID_W, d), F32),
            pltpu.VMEM((6, TILE, d), BF16),
            pltpu.VMEM((TILE, LANES), BF16),
            pltpu.VMEM((TILE, LANES), BF16),
            pltpu.VMEM((TILE, LANES), BF16),
        ],
        name="rwkv7_features",
        compiler_params=pltpu.CompilerParams(
            dimension_semantics=("parallel", "parallel", "arbitrary"), vmem_limit_bytes=VMEM_LIMIT),
    )(*args)


def _bmm(a, b):
    return lax.dot_general(a, b, (((2,), (1,)), ((0,), (0,))), preferred_element_type=F32)


def _bmm_nt(a, b):
    return lax.dot_general(a, b, (((2,), (2,)), ((0,), (0,))), preferred_element_type=F32)


def _bmm_tn(a, b):
    return lax.dot_general(a, b, (((1,), (1,)), ((0,), (0,))), preferred_element_type=F32)


def _chunk_cumsum(x, rev):
    c = x.shape[1]
    row = lax.broadcasted_iota(jnp.int32, (1, c, 1), 1)
    s = 1
    while s < c:
        if rev:
            x = x + jnp.where(row < c - s, pltpu.roll(x, c - s, axis=1), 0.0)
        else:
            x = x + jnp.where(row >= s, pltpu.roll(x, s, axis=1), 0.0)
        s *= 2
    return x


def _scan_chunk_maps(n_lat_chunks, n_ctx_chunks):
    total = n_lat_chunks + n_ctx_chunks

    def fwd(s):
        return jnp.where(s < total, (s + n_lat_chunks) % total, total)

    def bwd(s):
        return jnp.where(s < total, total - 1 - s, total)

    return total, fwd, bwd


def _scan_specs(b, ng, tp, n_lat_chunks, n_ctx_chunks, n_one, n_two, nchunks):
    assert n_lat_chunks % nchunks == 0 and n_ctx_chunks % nchunks == 0
    rows = CHUNK * nchunks
    total, fwd, bwd = _scan_chunk_maps(n_lat_chunks // nchunks, n_ctx_chunks // nchunks)
    assert tp == (total + 1) * rows
    in_specs = []
    for dr, cm in enumerate((fwd, bwd)):
        one = pl.BlockSpec((1, ng, rows, LANES), lambda bi, s, cm=cm: (bi, 0, cm(s), 0))
        two = pl.BlockSpec((1, 1, ng, rows, LANES), lambda bi, s, cm=cm, dr=dr: (dr, bi, 0, cm(s), 0))
        in_specs += [one] * n_one + [two] * n_two
    out_specs = [pl.BlockSpec((1, ng, rows, LANES), lambda bi, s, cm=cm: (bi, 0, cm(s), 0)) for cm in (fwd, bwd)]
    out_shape = [jax.ShapeDtypeStruct((b, ng, tp, LANES), BF16)] * 2
    return (b, total + 1), in_specs, out_specs, out_shape


def _scan_rows(ci, nchunks):
    fwd = pl.ds(pl.multiple_of(ci * CHUNK, CHUNK), CHUNK)
    bwd = pl.ds(pl.multiple_of((nchunks - 1 - ci) * CHUNK, CHUNK), CHUNK)
    return fwd, bwd


def _scan_or_pad(body, nchunks, y_os):
    s = pl.program_id(1)
    last = pl.num_programs(1) - 1

    @pl.when(s < last)
    def _():
        lax.fori_loop(0, nchunks, body, 0, unroll=2)

    @pl.when(s == last)
    def _():
        for y_o in y_os:
            y_o[...] = jnp.zeros_like(y_o)


def _chunk_by_groups(chunk_fn, loaded, splits):
    n = loaded[0][0].shape[0]
    step = n // splits
    parts = [[], []]
    for part in range(splits):
        gs = slice(part * step, (part + 1) * step)
        for dr in range(2):
            parts[dr].append(chunk_fn(*[a[gs] for a in loaded[dr]], rev=(dr == 1)))
    return [tuple(jnp.concatenate([p[i] for p in parts[dr]], axis=0) for i in range(2)) for dr in range(2)]


def _load_chunk(ref, lead, rs, dtype=F32):
    return ref[lead + (0, slice(None), rs)].astype(dtype)


def _rw_chunk(r, v, kk, bb, kd, lw, st, rev):
    c = CHUNK
    half = RW_HEAD
    npair = r.shape[0]
    ri = lax.broadcasted_iota(jnp.int32, (1, c, LANES), 1)
    ci = lax.broadcasted_iota(jnp.int32, (1, c, LANES), 2) % half
    head_a = lax.broadcasted_iota(jnp.int32, (1, 1, LANES), 2) < half

    def split_rows(x):
        zero = jnp.zeros_like(x)
        return jnp.concatenate([jnp.where(head_a, x, zero), jnp.where(head_a, zero, x)], axis=1)

    if rev:
        incl, strict = ci >= ri, ci > ri
    else:
        incl, strict = ci <= ri, ci < ri
    cum = _chunk_cumsum(lw, rev)
    tot = cum[:, 0:1] if rev else cum[:, c - 1:c]
    e_in = jnp.exp2(cum)
    e_prev = jnp.exp2(cum - lw)
    e_inv = jnp.exp2(-cum)
    e_out = jnp.exp2(tot - cum)
    rt = (r * e_in).astype(BF16)
    at = (-kk * e_prev).astype(BF16)
    bi = (bb * e_inv).astype(BF16)
    ki = (kd * e_inv).astype(BF16)
    bh = (bb * e_out).astype(BF16)
    kh = (kd * e_out).astype(BF16)
    vb = v.astype(BF16)
    sb = st.astype(BF16)
    lhs = jnp.concatenate([at, rt], axis=1)
    h0 = _bmm_nt(lhs, sb)
    m_b = _bmm_nt(lhs, split_rows(bi))
    m_k = _bmm_nt(lhs, split_rows(ki))
    tri = jnp.concatenate([strict, incl], axis=1)
    m_b = m_b.astype(BF16)
    m_b = jnp.where(tri, m_b, jnp.zeros_like(m_b))
    m_k = m_k.astype(BF16)
    m_k = jnp.where(tri, m_k, jnp.zeros_like(m_k))
    ab16, a_rb = m_b[:, :c], m_b[:, c:]
    zero16 = jnp.zeros_like(ab16)
    a1 = jnp.where(ri // 2 == ci // 2, ab16, zero16)
    pmb = jnp.where(ri == ci, jnp.ones_like(ab16), a1)
    off = jnp.logical_and(ri // 4 == ci // 4, ri // 2 != ci // 2)
    x2 = jnp.where(off, ab16, zero16).astype(F32)
    a1f = a1.astype(F32)
    a1_row = jnp.where(head_a, jnp.sum(jnp.where(head_a, a1f, 0.0), axis=2, keepdims=True),
                       jnp.sum(jnp.where(head_a, 0.0, a1f), axis=2, keepdims=True))
    a1_col = jnp.sum(a1f, axis=1, keepdims=True)
    y2 = x2 + a1_row * pltpu.roll(x2, c - 1 if rev else 1, axis=1)
    d4 = y2 + pltpu.roll(y2, 1 if rev else LANES - 1, axis=2) * a1_col
    pmb = jnp.where(off, d4.astype(BF16), pmb)
    m = 4
    while m < c:
        off = jnp.logical_and(ri // (2 * m) == ci // (2 * m), ri // m != ci // m)
        a_off = jnp.where(off, ab16, zero16)
        delta = _bmm(_bmm(pmb, split_rows(a_off)).astype(BF16), split_rows(pmb))
        pmb = jnp.where(off, delta.astype(BF16), pmb)
        m *= 2
    av = _bmm(m_k, split_rows(vb))
    x = h0[:, :c] + av[:, :c]
    u = _bmm(pmb, split_rows(x.astype(BF16)))
    ub = u.astype(BF16)
    o_full = h0[:, c:] + av[:, c:] + _bmm(a_rb, split_rows(ub))
    upd = _bmm_tn(jnp.concatenate([ub, vb], axis=1), jnp.concatenate([bh, kh], axis=1))
    blk_r = lax.broadcasted_iota(jnp.int32, (1, LANES, LANES), 1) // half
    blk_c = lax.broadcasted_iota(jnp.int32, (1, LANES, LANES), 2) // half
    st_new = st * jnp.exp2(tot) + jnp.where(blk_r == blk_c, upd, 0.0)
    return o_full, st_new


def _rw_scan_kernel(*refs):
    ins = refs[:12]
    y_os = refs[12:14]
    st = refs[14]

    @pl.when(pl.program_id(1) == 0)
    def _():
        st[...] = jnp.zeros_like(st)

    def body(ci, carry):
        rows = _scan_rows(ci, RW_SCAN_CHUNKS)
        loaded = []
        for dr in range(2):
            r_r, v_r, kk_r, bb_r, kd_r, lw_r = ins[6 * dr:6 * dr + 6]
            rs = rows[dr]
            loaded.append((_load_chunk(r_r, (), rs), _load_chunk(v_r, (), rs), _load_chunk(kk_r, (), rs),
                           _load_chunk(bb_r, (0,), rs), _load_chunk(kd_r, (0,), rs), _load_chunk(lw_r, (0,), rs),
                           st[dr]))
        results = _chunk_by_groups(_rw_chunk, loaded, 1)
        for dr in range(2):
            y_os[dr][0, :, rows[dr]] = results[dr][0].astype(y_os[dr].dtype)
            st[dr] = results[dr][1]
        return carry

    _scan_or_pad(body, RW_SCAN_CHUNKS, y_os)


def _rw_scan(feats, *, n_lat_chunks, n_ctx_chunks):
    r, v, kk, lw, bb, kd = feats
    b, ng, tp, _ = r.shape
    grid, in_specs, out_specs, out_shape = _scan_specs(b, ng, tp, n_lat_chunks, n_ctx_chunks, 3, 3, RW_SCAN_CHUNKS)
    return pl.pallas_call(
        _rw_scan_kernel,
        grid=grid,
        in_specs=in_specs,
        out_specs=out_specs,
        out_shape=out_shape,
        scratch_shapes=[pltpu.VMEM((2, ng, LANES, LANES), F32)],
        name="rwkv7_scan",
        compiler_params=pltpu.CompilerParams(
            dimension_semantics=("parallel", "arbitrary"), vmem_limit_bytes=VMEM_LIMIT),
    )(r, v, kk, bb, kd, lw, r, v, kk, bb, kd, lw)


def _residual_tail(yg_bf16, wo_ref, x_ref, mod_ref, pg_ref, o_ref):
    d = x_ref.shape[-1]
    o = _dot(yg_bf16, wo_ref[...])
    ms = jnp.mean(o * o, axis=-1, keepdims=True)
    o = o * lax.rsqrt(ms + NORM_EPS) * pg_ref[...]
    gate = mod_ref[0, 0][:, 2 * d:]
    o_ref[0] = x_ref[0] + gate * o


def _rw_out_kernel(yf_ref, yb_ref, z_ref, bv_ref, lnw_ref, lnb_ref, hs_ref, wo_ref, x_ref, mod_ref, pg_ref,
                   o_ref, yg):
    inv = 1.0 / RW_HEAD
    hs = hs_ref[...]
    per = hs.shape[0] // LANES

    def groups(ref, g0):
        return jnp.concatenate([ref[0, g0 + i] for i in range(per)], axis=1)

    for g0 in range(0, yf_ref.shape[1], per):
        cs = slice(g0 * LANES, (g0 + per) * LANES)
        y = groups(yf_ref, g0).astype(F32) + groups(yb_ref, g0).astype(F32)
        mu = _dot(y.astype(BF16), hs) * inv
        yc = y - mu
        var = _dot((yc * yc).astype(BF16), hs) * inv
        yn = yc * lax.rsqrt(var + LN_X_EPS) * lnw_ref[:, cs] + lnb_ref[:, cs] + groups(bv_ref, g0).astype(F32)
        yg[:, cs] = yn.astype(BF16) * groups(z_ref, g0)
    _residual_tail(yg[...], wo_ref, x_ref, mod_ref, pg_ref, o_ref)


def _hg_out_kernel(yf_ref, yb_ref, sg_ref, gn_ref, wo_ref, x_ref, mod_ref, pg_ref, o_ref, yg):
    for hx in range(yf_ref.shape[1]):
        cs = slice(hx * LANES, (hx + 1) * LANES)
        y = yf_ref[0, hx].astype(F32) + yb_ref[0, hx].astype(F32)
        ms = jnp.mean(y * y, axis=-1, keepdims=True)
        yn = y * lax.rsqrt(ms + NORM_EPS) * gn_ref[...]
        yg[:, cs] = yn.astype(BF16) * sg_ref[0, hx]
    _residual_tail(yg[...], wo_ref, x_ref, mod_ref, pg_ref, o_ref)


def _out_group_spec(di):
    return pl.BlockSpec((1, di // LANES, OTILE, LANES), lambda bi, j: (bi, 0, j, 0))


def _mixer_out(name, kernel_fn, ys, extras, extra_specs, wo, xs, mod, post_g, n_tiles, out_rows):
    b, tp, d = xs.shape
    in_place = out_rows == tp
    assert in_place or out_rows == n_tiles * OTILE
    di = wo.shape[0]
    per = TILE // OTILE
    tokd = pl.BlockSpec((1, OTILE, d), lambda bi, j: (bi, j, 0))
    in_specs = [_out_group_spec(di)] * 2 + extra_specs + [
        pl.BlockSpec((di, d), lambda bi, j: (0, 0), pipeline_mode=pl.Buffered(1)),
        tokd,
        pl.BlockSpec((1, 1, 1, mod.shape[-1]), lambda bi, j: (bi, j // per, 0, 0)),
        pl.BlockSpec((1, d), lambda bi, j: (0, 0)),
    ]
    n_in = len(in_specs)
    return pl.pallas_call(
        kernel_fn,
        grid=(b, n_tiles),
        in_specs=in_specs,
        out_specs=tokd,
        out_shape=jax.ShapeDtypeStruct((b, out_rows, d), F32),
        scratch_shapes=[pltpu.VMEM((OTILE, di), BF16)],
        input_output_aliases={n_in - 3: 0} if in_place else {},
        name=name,
        compiler_params=pltpu.CompilerParams(
            dimension_semantics=("parallel", "parallel"), vmem_limit_bytes=VMEM_LIMIT),
    )(*ys, *extras, wo, xs, mod, post_g)


def _hg_feat_kernel(x_ref, mod_ref, g_ref, wq_ref, wf0_ref, wf1_ref, wi_ref, wg_ref, lb_ref,
                    q_o, i_o, sg_o, lf_o, kd_o, hbf):
    n = pl.program_id(1)
    j = pl.program_id(2)
    d = x_ref.shape[-1]
    rows = x_ref.shape[1]
    base = pl.multiple_of(j * rows, rows)

    @pl.when(n == 0)
    def _prologue():
        g = g_ref[...]
        for t in range(rows // TILE):
            m = mod_ref[0, t]
            shift, scale = m[:, :d], m[:, d:2 * d]

            def norm_rows(rb, c, t=t, shift=shift, scale=scale):
                r0 = pl.multiple_of(t * TILE + rb * GRID_W, GRID_W)
                hbf[pl.ds(base + r0, GRID_W), :] = _adaln(x_ref[0, pl.ds(r0, GRID_W), :], g, scale,
                                                          shift).astype(BF16)
                return c

            lax.fori_loop(0, TILE // GRID_W, norm_rows, 0, unroll=4)

    h = hbf[pl.ds(base, rows), :]
    lb = lb_ref[...]
    _store_groups(q_o, (0,), _silu(_dot(h, wq_ref[...])))
    for dr, wf_ref in enumerate((wf0_ref, wf1_ref)):
        f = lb + (1.0 - lb) * jax.nn.sigmoid(_dot(h, wf_ref[...]))
        _store_groups(kd_o, (dr, 0), 1.0 - f)
        _store_groups(lf_o, (dr, 0), jnp.log2(f))
    _store_groups(i_o, (0,), _dot(h, wi_ref[...]))
    _store_groups(sg_o, (0,), _silu(_dot(h, wg_ref[...])))


def _hg_features(xs, mod, pre_g, w_in, lb, *, tn=256):
    b, tp, d = xs.shape
    di = w_in.shape[-1] // 5
    nt = di // tn
    per = HG_TILES_PER_STEP if (tp // TILE) % HG_TILES_PER_STEP == 0 else 1
    rows = per * TILE
    nsteps = tp // rows
    _, _, s1, s2 = _group_major_specs(b, tp, di, tn)
    gq = tn // LANES
    one = pl.BlockSpec((1, gq, rows, LANES), lambda bi, n, j: (bi, n, j, 0))
    two = pl.BlockSpec((2, 1, gq, rows, LANES), lambda bi, n, j: (0, bi, n, j, 0))

    def tok_tile(n, j):
        return jnp.where(n == 0, j, nsteps - 1)

    return pl.pallas_call(
        _hg_feat_kernel,
        grid=(b, nt, nsteps),
        in_specs=[
            pl.BlockSpec((1, rows, d), lambda bi, n, j: (bi, tok_tile(n, j), 0)),
            pl.BlockSpec((1, per, 1, mod.shape[-1]), lambda bi, n, j: (bi, tok_tile(n, j), 0, 0)),
            pl.BlockSpec((1, d), lambda bi, n, j: (0, 0)),
            *[pl.BlockSpec((d, tn), lambda bi, n, j, part=part: (0, part * nt + n)) for part in range(5)],
            pl.BlockSpec((1, tn), lambda bi, n, j: (0, n)),
        ],
        out_specs=[one, one, one, two, two],
        out_shape=[s1(BF16), s1(BF16), s1(BF16), s2(F32), s2(BF16)],
        scratch_shapes=[pltpu.VMEM((tp, d), BF16)],
        name="hgrn2_features",
        compiler_params=pltpu.CompilerParams(
            dimension_semantics=("parallel", "arbitrary", "arbitrary"), vmem_limit_bytes=VMEM_LIMIT),
    )(xs, mod, pre_g, w_in, w_in, w_in, w_in, w_in, lb)


def _hg_chunk(q, k, v, lf2, st, rev):
    c = CHUNK
    sub = SUBLANES
    nh = q.shape[0]
    row = lax.broadcasted_iota(jnp.int32, (1, c, 1), 1)
    ri = lax.broadcasted_iota(jnp.int32, (1, c, c), 1)
    ci = lax.broadcasted_iota(jnp.int32, (1, c, c), 2)
    diag = jnp.sum(q * k, axis=2, keepdims=True)
    att = jnp.where(ri == ci, diag, jnp.zeros((nh, c, c), F32))
    shift = c - 1 if rev else 1
    later1 = (row % 2) == (0 if rev else 1)
    near = jnp.sum(q * jnp.exp2(lf2) * pltpu.roll(k, shift, axis=1), axis=2, keepdims=True)
    att = jnp.where(jnp.logical_and(ci == (ri + 1 if rev else ri - 1), later1), near, att)
    p = lf2 + jnp.where(later1, pltpu.roll(lf2, shift, axis=1), 0.0)
    m = 2
    while m < c:
        pair = jnp.logical_and(ri // (2 * m) == ci // (2 * m), ri // m != ci // m)
        if m >= sub:
            zeros = jnp.zeros((nh, m, LANES), BF16)
            q_parts, k_parts, p_parts = [], [], []
            for blk in range(c // (2 * m)):
                lo = slice(blk * 2 * m, blk * 2 * m + m)
                hi = slice(blk * 2 * m + m, (blk + 1) * 2 * m)
                early, later = (hi, lo) if rev else (lo, hi)
                bidx = blk * 2 * m + (m if rev else m - 1)
                tt = p[:, bidx:bidx + 1, :]
                q_l = (q[:, later] * jnp.exp2(p[:, later])).astype(BF16)
                k_e = (k[:, early] * jnp.exp2(tt - p[:, early])).astype(BF16)
                if rev:
                    q_parts += [q_l, zeros]
                    k_parts += [zeros, k_e]
                    p_parts += [p[:, later] + tt, p[:, early]]
                else:
                    q_parts += [zeros, q_l]
                    k_parts += [k_e, zeros]
                    p_parts += [p[:, early], p[:, later] + tt]
            mm = _bmm_nt(jnp.concatenate(q_parts, axis=1), jnp.concatenate(k_parts, axis=1))
            p = jnp.concatenate(p_parts, axis=1)
        else:
            in_hi = ((row // m) % 2) == 1
            late = jnp.logical_not(in_hi) if rev else in_hi
            p4 = p.reshape(nh, c // sub, sub, LANES)
            srow = lax.broadcasted_iota(jnp.int32, (1, 1, sub, 1), 2)
            tt4 = None
            for blk in range(sub // (2 * m)):
                bidx = blk * 2 * m + (m if rev else m - 1)
                cand = jnp.broadcast_to(p4[:, :, bidx:bidx + 1, :], p4.shape)
                tt4 = cand if tt4 is None else jnp.where(srow // (2 * m) == blk, cand, tt4)
            tt = tt4.reshape(nh, c, LANES)
            ee = jnp.exp2(jnp.where(late, p, tt - p))
            qt = jnp.where(late, q * ee, 0.0).astype(BF16)
            kt = jnp.where(late, 0.0, k * ee).astype(BF16)
            mm = _bmm_nt(qt, kt)
            p = jnp.where(late, p + tt, p)
        att = jnp.where(pair, mm, att)
        m *= 2
    tot = p[:, 0:1] if rev else p[:, c - 1:c]
    vb = v.astype(BF16)
    y = _bmm(att.astype(BF16), vb) + _bmm_nt((q * jnp.exp2(p)).astype(BF16), st.astype(BF16))
    st_new = st * jnp.exp2(tot) + _bmm_tn(vb, (k * jnp.exp2(tot - p)).astype(BF16))
    return y, st_new


def _hg_scan_kernel(*refs):
    ins = refs[:8]
    y_os = refs[8:10]
    st = refs[10]

    @pl.when(pl.program_id(1) == 0)
    def _():
        st[...] = jnp.zeros_like(st)

    def body(ci, carry):
        rows = _scan_rows(ci, HG_SCAN_CHUNKS)
        loaded = []
        for dr in range(2):
            q_r, v_r, kd_r, lf_r = ins[4 * dr:4 * dr + 4]
            rs = rows[dr]
            loaded.append((_load_chunk(q_r, (), rs), _load_chunk(kd_r, (0,), rs), _load_chunk(v_r, (), rs, BF16),
                           _load_chunk(lf_r, (0,), rs), st[dr]))
        results = _chunk_by_groups(_hg_chunk, loaded, HG_HEAD_SPLITS)
        for dr in range(2):
            y_os[dr][0, :, rows[dr]] = results[dr][0].astype(y_os[dr].dtype)
            st[dr] = results[dr][1]
        return carry

    _scan_or_pad(body, HG_SCAN_CHUNKS, y_os)


def _hg_scan(q, v, lf, kd, *, n_lat_chunks, n_ctx_chunks):
    b, ng, tp, _ = q.shape
    grid, in_specs, out_specs, out_shape = _scan_specs(b, ng, tp, n_lat_chunks, n_ctx_chunks, 2, 2, HG_SCAN_CHUNKS)
    return pl.pallas_call(
        _hg_scan_kernel,
        grid=grid,
        in_specs=in_specs,
        out_specs=out_specs,
        out_shape=out_shape,
        scratch_shapes=[pltpu.VMEM((2, ng, LANES, LANES), F32)],
        name="hgrn2_scan",
        compiler_params=pltpu.CompilerParams(
            dimension_semantics=("parallel", "arbitrary"), vmem_limit_bytes=VMEM_LIMIT),
    )(q, v, kd, lf, q, v, kd, lf)


def _pad_rows(w, rows):
    return jnp.pad(w, ((0, rows - w.shape[0]), (0, 0)))


def _pad_cols(w, cols):
    return jnp.pad(w, ((0, 0), (0, cols - w.shape[1])))


def kernel(x, c, ctx, c_ctx, mod_w, mod_b, pre_g, post_g, rw_mix, rw_proj, rw_wo, rw_w0, rw_w1, rw_w2, rw_a0,
           rw_a1, rw_a2, rw_v0, rw_v1, rw_v2, rw_kk, rw_ka, rw_rk, rw_lnw, rw_lnb, hg_win, hg_wo, hg_gn, hg_lb):
    b, seq, d = x.shape
    ctx_len = ctx.shape[1]
    depth = mod_w.shape[0]
    di = rw_proj.shape[-1]
    assert seq % TILE == 0 and ctx_len % CHUNK == 0 and ctx_len <= TILE and seq % GRID_W == 0
    n_lat_tiles = seq // TILE
    tp = seq + TILE
    n_lat_chunks, n_ctx_chunks = seq // CHUNK, ctx_len // CHUNK

    xs = jnp.concatenate([x, ctx, jnp.zeros((b, tp - seq - ctx_len, d), x.dtype)], axis=1)

    rows = ((b + 1 + 7) // 8) * 8
    cond = jnp.concatenate([c, c_ctx[None], jnp.zeros((rows - b - 1, d), c.dtype)], axis=0)
    mod_all = _modulation(cond, mod_w, mod_b)
    mod_lat = jnp.broadcast_to(mod_all[:, :b, None, :], (depth, b, n_lat_tiles, 3 * d))
    mod_ctx = jnp.broadcast_to(mod_all[:, b:b + 1, None, :], (depth, b, 1, 3 * d))
    mod_t = jnp.concatenate([mod_lat, mod_ctx], axis=2)[:, :, :, None, :]

    p_lb = jax.nn.softmax(hg_lb.astype(F32), axis=0)
    lb_all = jnp.cumsum(p_lb, axis=0) - p_lb[0]

    vf = None
    for i in range(depth):
        ctx_out = i < depth - 1
        out_rows = tp if ctx_out else seq
        n_out_tiles = out_rows // OTILE
        j = i // 2
        mod_i = mod_t[i]
        pg = pre_g[i][None]
        if i % 2 == 0:
            r64 = rw_w1.shape[-1]
            w2 = jnp.stack([jnp.pad(rw_w2[j, 0], ((0, LANES - r64), (0, 0))),
                            jnp.pad(rw_w2[j, 1], ((r64, LANES - 2 * r64), (0, 0)))])
            a2 = jnp.stack([jnp.pad(rw_a2[j, 0], ((0, LANES - r64), (0, 0))),
                            jnp.pad(rw_a2[j, 1], ((r64, LANES - 2 * r64), (0, 0)))])
            if j == 0:
                v0 = jnp.zeros((di,), F32)
                v1 = jnp.zeros((d, LANES), BF16)
                v2 = jnp.zeros((LANES, di), BF16)
            else:
                v0 = rw_v0[j - 1]
                v1 = _pad_cols(rw_v1[j - 1], LANES).astype(BF16)
                v2 = _pad_rows(rw_v2[j - 1], LANES).astype(BF16)
            p = {
                "mix": rw_mix[j],
                "proj": rw_proj[j].astype(BF16),
                "w1": jnp.concatenate([rw_w1[j, 0], rw_w1[j, 1]], axis=1).astype(BF16),
                "a1": jnp.concatenate([rw_a1[j, 0], rw_a1[j, 1]], axis=1).astype(BF16),
                "v1": v1,
                "w2": w2.astype(BF16),
                "a2": a2.astype(BF16),
                "v2": v2,
                "vec": jnp.stack([rw_w0[j, 0], rw_w0[j, 1], rw_a0[j, 0], rw_a0[j, 1], v0, rw_kk[j], rw_ka[j],
                                  rw_rk[j]]),
            }
            r, v, kk, z, bv, lw, bb, kd = _rw_features(xs, mod_i, pg, p, vf if j > 0 else None,
                                                       n_lat_tiles=n_lat_tiles, ctx_len=ctx_len)
            if j == 0:
                vf = v
            yf, yb = _rw_scan((r, v, kk, lw, bb, kd), n_lat_chunks=n_lat_chunks, n_ctx_chunks=n_ctx_chunks)
            gspec = _out_group_spec(di)
            extras = [z, bv, rw_lnw[j][None], rw_lnb[j][None], _head_sum_matrix(2 * LANES, RW_HEAD)]
            especs = [gspec, gspec, pl.BlockSpec((1, di), lambda bi, jj: (0, 0)),
                      pl.BlockSpec((1, di), lambda bi, jj: (0, 0)), pl.BlockSpec((2 * LANES, 2 * LANES), lambda bi, jj: (0, 0))]
            xs = _mixer_out("rwkv7_out", _rw_out_kernel, (yf, yb), extras, especs, rw_wo[j].astype(BF16), xs, mod_i,
                            post_g[i][None], n_out_tiles, out_rows)
        else:
            q, iv, sg, lf, kd = _hg_features(xs, mod_i, pg, hg_win[j].astype(BF16), lb_all[i][None])
            yf, yb = _hg_scan(q, iv, lf, kd, n_lat_chunks=n_lat_chunks, n_ctx_chunks=n_ctx_chunks)
            extras = [sg, hg_gn[j][None]]
            especs = [_out_group_spec(di), pl.BlockSpec((1, HG_HEAD), lambda bi, jj: (0, 0))]
            xs = _mixer_out("hgrn2_out", _hg_out_kernel, (yf, yb), extras, especs, hg_wo[j].astype(BF16), xs, mod_i,
                            post_g[i][None], n_out_tiles, out_rows)
    return xs
```
